```python
import math
import jax, jax.numpy as jnp
from jax import lax
import numpy as np

D_MODEL = 1024
BATCH = 8
SEQ = 4096
DEPTH = 4

MLA_HEADS = 8
MLA_Q_RANK = 256
MLA_KV_RANK = 128
MLA_NOPE = 64
MLA_ROPE = 32
MLA_V = 64
ROPE_BASE = 10000.0
DIL_PATTERNS = ((128, 1), (512, 4), (2048, 16))
DIL_GROUPS = 3
DIL_HEADS = 4
DIL_QK = 64
DIL_V = 128
DIFF_HEADS = 4
DIFF_QK = 64
DIFF_V = 2 * DIFF_QK
REL_BUCKETS = 32
REL_MAX_DIST = 128
DIL_BIAS_COLS = DIL_GROUPS * DIL_HEADS
N_BIAS = DIL_BIAS_COLS + 2 * DIFF_HEADS
N_BRANCH = 3
BRANCH_W = 512
D_FF = 4 * D_MODEL
Q_BLOCK = 128
EPS = 1e-6
NEG = -1e30

COL_SIZES = (
    MLA_Q_RANK, MLA_KV_RANK, MLA_ROPE,
    DIL_GROUPS * DIL_HEADS * DIL_QK, DIL_GROUPS * DIL_HEADS * DIL_QK, DIL_HEADS * DIL_V,
    2 * DIFF_HEADS * DIFF_QK, 2 * DIFF_HEADS * DIFF_QK, DIFF_HEADS * DIFF_V,
    N_BRANCH * D_MODEL,
)
IN_COLS = sum(COL_SIZES)
COL_SPLITS = tuple(int(v) for v in np.cumsum(COL_SIZES)[:-1])

kernel_name = 'hybrid_mla_dilated_diff_gated_encoder'

f32 = jnp.float32


def _rmsnorm(x, g):
    x32 = x.astype(f32)
    y = x32 * lax.rsqrt(jnp.mean(x32 * x32, axis=-1, keepdims=True) + EPS)
    return (y * g.astype(f32)).astype(x.dtype)


def _rope(x, pos):
    half = x.shape[-1] // 2
    inv = ROPE_BASE ** (-jnp.arange(half, dtype=f32) / half)
    ang = pos.astype(f32)[:, :, None] * inv
    cos = jnp.cos(ang)[:, :, None, :]
    sin = jnp.sin(ang)[:, :, None, :]
    x32 = x.astype(f32)
    x1, x2 = x32[..., :half], x32[..., half:]
    return jnp.concatenate([x1 * cos - x2 * sin, x2 * cos + x1 * sin], axis=-1).astype(x.dtype)


def _rel_bucket(rel):
    nb = REL_BUCKETS // 2
    max_exact = nb // 2
    ret = jnp.where(rel > 0, nb, 0)
    n = jnp.abs(rel)
    large = max_exact + (jnp.log(jnp.maximum(n, 1).astype(f32) / max_exact)
                         / math.log(REL_MAX_DIST / max_exact) * (nb - max_exact)).astype(jnp.int32)
    large = jnp.minimum(large, nb - 1)
    return ret + jnp.where(n < max_exact, n, large)


def _to_blocks(x):
    b, s = x.shape[:2]
    x = x.reshape((b, s // Q_BLOCK, Q_BLOCK) + x.shape[2:])
    return jnp.moveaxis(x, 1, 0)


def _from_blocks(y):
    y = jnp.moveaxis(y, 0, 1)
    return y.reshape((y.shape[0], y.shape[1] * y.shape[2]) + y.shape[3:])


def _mla(c_q, c_kv, k_r, pos, w_uq, g_q, w_ukv, g_kv):
    q = jnp.einsum('bsr,rhd->bshd', _rmsnorm(c_q, g_q), w_uq)
    kv = jnp.einsum('bsr,rhd->bshd', _rmsnorm(c_kv, g_kv), w_ukv)
    q = jnp.concatenate([q[..., :MLA_NOPE], _rope(q[..., MLA_NOPE:], pos)], axis=-1)
    k_rope = _rope(k_r[:, :, None, :], pos)
    k = jnp.concatenate([kv[..., :MLA_NOPE],
                         jnp.broadcast_to(k_rope, kv.shape[:3] + (MLA_ROPE,))], axis=-1)
    v = kv[..., MLA_NOPE:]
    scale = (MLA_NOPE + MLA_ROPE) ** -0.5

    def block(qb):
        s = jnp.einsum('bqhd,bkhd->bhqk', qb, k).astype(f32) * scale
        p = jax.nn.softmax(s, axis=-1).astype(v.dtype)
        return jnp.einsum('bhqk,bkhd->bqhd', p, v)

    o = _from_blocks(lax.map(block, _to_blocks(q)))
    return o.reshape(o.shape[:2] + (MLA_HEADS * MLA_V,))


def _dilated_group(q, k, v, tab_g, window, dil):
    b, n_seq, h, _ = q.shape
    half = window // (2 * dil)
    bs = half
    L = n_seq // dil
    nblk = -(-L // bs)
    Lp = nblk * bs

    def strided(x, lo, hi):
        x = jnp.swapaxes(x.reshape((b, L, dil) + x.shape[2:]), 1, 2)
        return jnp.pad(x, ((0, 0), (0, 0), (lo, hi), (0, 0), (0, 0)))

    def bands(x):
        xs = strided(x, bs, Lp - L + bs).reshape((b, dil, nblk + 2, bs) + x.shape[2:])
        return jnp.concatenate([xs[:, :, :-2], xs[:, :, 1:-1], xs[:, :, 2:]], axis=3)

    qs = strided(q, 0, Lp - L).reshape((b, dil, nblk, bs) + q.shape[2:])
    kb = bands(k)
    vb = bands(v)
    qi = jnp.arange(bs)[:, None]
    kj = jnp.arange(3 * bs)[None, :] - bs
    rel = kj - qi
    kidx = jnp.arange(nblk)[:, None, None] * bs + kj[None]
    mask = (jnp.abs(rel) <= half)[None] & (kidx >= 0) & (kidx < L)
    bias = jnp.transpose(tab_g.astype(f32)[_rel_bucket(rel * dil)], (2, 0, 1))
    s = jnp.einsum('brnqhd,brnkhd->brnhqk', qs, kb).astype(f32) * (DIL_QK ** -0.5) + bias
    s = jnp.where(mask[None, None, :, None], s, NEG)
    lse = jax.nn.logsumexp(s, axis=-1)
    p = jnp.exp(s - lse[..., None]).astype(v.dtype)
    o = jnp.einsum('brnhqk,brnkhd->brnqhd', p, vb)

    def unstrided(y):
        y = y.reshape((b, dil, Lp) + y.shape[4:])[:, :, :L]
        return jnp.swapaxes(y, 1, 2).reshape((b, n_seq) + y.shape[3:])

    return unstrided(o), unstrided(jnp.swapaxes(lse, 3, 4)[..., None])[..., 0]


def _dilated(q, k, v, bias_tab):
    outs, lses = [], []
    for g in range(DIL_GROUPS):
        window, dil = DIL_PATTERNS[g]
        o, l = _dilated_group(q[:, :, g], k[:, :, g], v,
                              bias_tab[:, g * DIL_HEADS:(g + 1) * DIL_HEADS], window, dil)
        outs.append(o)
        lses.append(l)
    alpha = jax.nn.softmax(jnp.stack(lses, axis=0), axis=0)
    o = jnp.einsum('gbsh,gbshd->bshd', alpha, jnp.stack(outs, axis=0).astype(f32)).astype(v.dtype)
    return o.reshape(o.shape[:2] + (DIL_HEADS * DIL_V,))


def _diff(q, k, v, pos, bias_tab, lq1, lk1, lq2, lk2, g_sub, layer):
    lam_init = 0.8 - 0.6 * math.exp(-0.3 * layer)
    lam = (jnp.exp(jnp.sum(lq1.astype(f32) * lk1.astype(f32)))
           - jnp.exp(jnp.sum(lq2.astype(f32) * lk2.astype(f32))) + lam_init)
    tab = bias_tab[:, DIL_BIAS_COLS:].astype(f32).reshape(REL_BUCKETS, 2, DIFF_HEADS)
    scale = DIFF_QK ** -0.5

    def block(args):
        qb, pb = args
        s = jnp.einsum('bqmhd,bkmhd->bmhqk', qb, k).astype(f32) * scale
        rel = pos[:, None, :] - pb[:, :, None]
        s = s + jnp.transpose(tab[_rel_bucket(rel)], (0, 3, 4, 1, 2))
        p = jax.nn.softmax(s, axis=-1)
        w = (p[:, 0] - lam * p[:, 1]).astype(v.dtype)
        return jnp.einsum('bhqk,bkhd->bqhd', w, v)

    o = _from_blocks(lax.map(block, (_to_blocks(q), _to_blocks(pos))))
    o = (_rmsnorm(o, g_sub).astype(f32) * (1.0 - lam_init)).astype(v.dtype)
    return o.reshape(o.shape[:2] + (DIFF_HEADS * DIFF_V,))


def setup_inputs(seed: int = 0) -> dict:
    key = jax.random.key(seed)
    ks = jax.random.split(key, 24)

    def nrm(k, shape, scale):
        return jax.random.normal(k, shape, dtype=f32) * scale

    def gain(k, shape):
        return 1.0 + 0.05 * jax.random.normal(k, shape, dtype=f32)

    positions = jnp.broadcast_to(jnp.arange(SEQ, dtype=jnp.int32)[None, :], (BATCH, SEQ))
    return {
        'x': nrm(ks[0], (BATCH, SEQ, D_MODEL), 1.0),
        'positions': positions,
        'rel_bias': nrm(ks[1], (REL_BUCKETS, N_BIAS), 0.3),
        'g_mix_pre': gain(ks[2], (DEPTH, D_MODEL)),
        'w_in': nrm(ks[3], (DEPTH, D_MODEL, IN_COLS), D_MODEL ** -0.5),
        'g_q': gain(ks[4], (DEPTH, MLA_Q_RANK)),
        'w_uq': nrm(ks[5], (DEPTH, MLA_Q_RANK, MLA_HEADS, MLA_NOPE + MLA_ROPE), MLA_Q_RANK ** -0.5),
        'g_kv': gain(ks[6], (DEPTH, MLA_KV_RANK)),
        'w_ukv': nrm(ks[7], (DEPTH, MLA_KV_RANK, MLA_HEADS, MLA_NOPE + MLA_V), MLA_KV_RANK ** -0.5),
        'lam_q1': nrm(ks[8], (DEPTH, DIFF_QK), 0.1),
        'lam_k1': nrm(ks[9], (DEPTH, DIFF_QK), 0.1),
        'lam_q2': nrm(ks[10], (DEPTH, DIFF_QK), 0.1),
        'lam_k2': nrm(ks[11], (DEPTH, DIFF_QK), 0.1),
        'g_diff_sub': gain(ks[12], (DEPTH, DIFF_V)),
        'w_branch': nrm(ks[13], (DEPTH, N_BRANCH, BRANCH_W, D_MODEL), BRANCH_W ** -0.5),
        'w_out': nrm(ks[14], (DEPTH, D_MODEL, D_MODEL), D_MODEL ** -0.5),
        'g_mix_post': gain(ks[15], (DEPTH, D_MODEL)),
        'g_mlp_pre': gain(ks[16], (DEPTH, D_MODEL)),
        'w_up': nrm(ks[17], (DEPTH, D_MODEL, D_FF), D_MODEL ** -0.5),
        'w_down': nrm(ks[18], (DEPTH, D_FF, D_MODEL), D_FF ** -0.5),
        'g_mlp_post': gain(ks[19], (DEPTH, D_MODEL)),
    }


def reference(x, positions, rel_bias, g_mix_pre, w_in, g_q, w_uq, g_kv, w_ukv,
              lam_q1, lam_k1, lam_q2, lam_k2, g_diff_sub, w_branch, w_out, g_mix_post,
              g_mlp_pre, w_up, w_down, g_mlp_post):
    b, s, _ = x.shape
    for l in range(DEPTH):
        h = _rmsnorm(x, g_mix_pre[l])
        z = h @ w_in[l]
        (c_q, c_kv, k_r, dq, dk, dv, fq, fk, fv, gz) = jnp.split(z, COL_SPLITS, axis=-1)
        o_a = _mla(c_q, c_kv, k_r, positions, w_uq[l], g_q[l], w_ukv[l], g_kv[l])
        o_b = _dilated(dq.reshape(b, s, DIL_GROUPS, DIL_HEADS, DIL_QK),
                       dk.reshape(b, s, DIL_GROUPS, DIL_HEADS, DIL_QK),
                       dv.reshape(b, s, DIL_HEADS, DIL_V), rel_bias)
        o_c = _diff(fq.reshape(b, s, 2, DIFF_HEADS, DIFF_QK),
                    fk.reshape(b, s, 2, DIFF_HEADS, DIFF_QK),
                    fv.reshape(b, s, DIFF_HEADS, DIFF_V), positions, rel_bias,
                    lam_q1[l], lam_k1[l], lam_q2[l], lam_k2[l], g_diff_sub[l], l)
        gates = jax.nn.sigmoid(gz.astype(f32)).astype(x.dtype).reshape(b, s, N_BRANCH, D_MODEL)
        merged = (gates[:, :, 0] * (o_a @ w_branch[l, 0])
                  + gates[:, :, 1] * (o_b @ w_branch[l, 1])
                  + gates[:, :, 2] * (o_c @ w_branch[l, 2]))
        x = x + _rmsnorm(merged @ w_out[l], g_mix_post[l])
        h2 = _rmsnorm(x, g_mlp_pre[l])
        f = jnp.square(jax.nn.relu(h2 @ w_up[l])) @ w_down[l]
        x = x + _rmsnorm(f, g_mlp_post[l])
    return x
```

```python
import functools
import math

import jax
import jax.numpy as jnp
import numpy as np
from jax import lax
from jax.experimental import pallas as pl
from jax.experimental.pallas import tpu as pltpu

f32 = jnp.float32
bf16 = jnp.bfloat16

D_MODEL = 1024
MLA_HEADS = 8
MLA_Q_RANK = 256
MLA_KV_RANK = 128
MLA_NOPE = 64
MLA_ROPE = 32
MLA_V = 64
ROPE_BASE = 10000.0
DIL_PATTERNS = ((128, 1), (512, 4), (2048, 16))
DIL_GROUPS = 3
DIL_HEADS = 4
DIL_QK = 64
DIL_V = 128
DIL_HALF = 64
DIFF_HEADS = 4
DIFF_QK = 64
DIFF_V = 128
REL_BUCKETS = 32
REL_MAX_DIST = 128
DIL_BIAS_COLS = DIL_GROUPS * DIL_HEADS
N_BRANCH = 3
BRANCH_W = 512
D_FF = 4 * D_MODEL
EPS = 1e-6
NEG = -1e30

LANES = 128
A_COLS = 512
QKV_COLS = 3584
GATE_COLS = N_BRANCH * D_MODEL
IN_PAD_COLS = A_COLS + QKV_COLS + GATE_COLS
DQ_BLK, DK_BLK, DV_BLK, FQ_BLK, FK_BLK, FV_BLK = 0, 6, 12, 16, 20, 24

VMEM_LIMIT = 48 * 1024 * 1024


def _rms(x, g):
    return x * lax.rsqrt(jnp.mean(x * x, axis=-1, keepdims=True) + EPS) * g


def _nt_dot(a, b):
    return lax.dot_general(a, b, (((1,), (1,)), ((), ())), preferred_element_type=f32)


IN_TM = 512
IN_TN = 512
_IN_QKV_J0 = A_COLS // IN_TN
_IN_GATE_J0 = (A_COLS + QKV_COLS) // IN_TN


def _in_proj_kernel(x_ref, g_ref, w_ref, a_ref, qkv_ref, gate_ref, h_ref):
    j = pl.program_id(1)

    @pl.when(j == 0)
    def _():
        h_ref[...] = _rms(x_ref[...], g_ref[...]).astype(bf16)

    z = jnp.dot(h_ref[...], w_ref[...], preferred_element_type=f32)

    @pl.when(j == 0)
    def _():
        a_ref[...] = z

    @pl.when((j >= _IN_QKV_J0) & (j < _IN_GATE_J0))
    def _():
        qkv_ref[...] = z.astype(bf16)

    @pl.when(j >= _IN_GATE_J0)
    def _():
        gate_ref[...] = jax.nn.sigmoid(z)


def _in_proj(x, g, w):
    t = x.shape[0]
    n_j = IN_PAD_COLS // IN_TN
    n_qkv = QKV_COLS // IN_TN
    n_gate = GATE_COLS // IN_TN
    return pl.pallas_call(
        _in_proj_kernel,
        grid=(t // IN_TM, n_j),
        in_specs=[
            pl.BlockSpec((IN_TM, D_MODEL), lambda i, j: (i, 0)),
            pl.BlockSpec((1, D_MODEL), lambda i, j: (0, 0)),
            pl.BlockSpec((D_MODEL, IN_TN), lambda i, j: (0, j)),
        ],
        out_specs=[
            pl.BlockSpec((IN_TM, IN_TN), lambda i, j: (i, 0)),
            pl.BlockSpec((IN_TM, IN_TN), lambda i, j: (i, jnp.clip(j - _IN_QKV_J0, 0, n_qkv - 1))),
            pl.BlockSpec((IN_TM, IN_TN), lambda i, j: (i, jnp.clip(j - _IN_GATE_J0, 0, n_gate - 1))),
        ],
        out_shape=[
            jax.ShapeDtypeStruct((t, A_COLS), f32),
            jax.ShapeDtypeStruct((t, QKV_COLS), bf16),
            jax.ShapeDtypeStruct((t, GATE_COLS), f32),
        ],
        scratch_shapes=[pltpu.VMEM((IN_TM, D_MODEL), bf16)],
        compiler_params=pltpu.CompilerParams(
            dimension_semantics=("arbitrary", "arbitrary"), vmem_limit_bytes=VMEM_LIMIT),
        name="in_proj",
    )(x, g, w)


PREP_TM = 512
MLA_SCALE = (MLA_NOPE + MLA_ROPE) ** -0.5
ROPE_HALF = MLA_ROPE // 2


def _rope_lanes(x, rope):
    c = rope[:, 0:LANES]
    sa = rope[:, LANES:2 * LANES]
    sb = rope[:, 2 * LANES:3 * LANES]
    return (x * c + pltpu.roll(x, ROPE_HALF, 1) * sa + pltpu.roll(x, LANES - ROPE_HALF, 1) * sb)


def _mla_prep_kernel(a_ref, gq_ref, gkv_ref, wq_ref, wk_ref, wv_ref, rope_ref, q_ref, k_ref, v_ref):
    a = a_ref[...]
    rope = rope_ref[...]
    cq = _rms(a[:, :MLA_Q_RANK], gq_ref[...]).astype(bf16)
    ckv = _rms(a[:, MLA_Q_RANK:MLA_Q_RANK + MLA_KV_RANK], gkv_ref[...]).astype(bf16)
    k_rope = _rope_lanes(a[:, A_COLS - LANES:], rope)
    qf = jnp.dot(cq, wq_ref[...], preferred_element_type=f32)
    kf = jnp.dot(ckv, wk_ref[...], preferred_element_type=f32)
    v_ref[...] = jnp.dot(ckv, wv_ref[...], preferred_element_type=f32).astype(bf16)
    for h in range(MLA_HEADS):
        sl = slice(h * LANES, (h + 1) * LANES)
        q_ref[:, sl] = (_rope_lanes(qf[:, sl], rope) * MLA_SCALE).astype(bf16)
        k_ref[:, sl] = (kf[:, sl] + k_rope).astype(bf16)


def _mla_prep(a, gq, gkv, wq, wk, wv, rope):
    t = a.shape[0]
    hq = MLA_HEADS * LANES
    hv = MLA_HEADS * MLA_V
    const = lambda i: (0, 0)
    return pl.pallas_call(
        _mla_prep_kernel,
        grid=(t // PREP_TM,),
        in_specs=[
            pl.BlockSpec((PREP_TM, A_COLS), lambda i: (i, 0)),
            pl.BlockSpec((1, MLA_Q_RANK), const),
            pl.BlockSpec((1, MLA_KV_RANK), const),
            pl.BlockSpec((MLA_Q_RANK, hq), const),
            pl.BlockSpec((MLA_KV_RANK, hq), const),
            pl.BlockSpec((MLA_KV_RANK, hv), const),
            pl.BlockSpec((PREP_TM, 3 * LANES), lambda i: (i, 0)),
        ],
        out_specs=[
            pl.BlockSpec((PREP_TM, hq), lambda i: (i, 0)),
            pl.BlockSpec((PREP_TM, hq), lambda i: (i, 0)),
            pl.BlockSpec((PREP_TM, hv), lambda i: (i, 0)),
        ],
        out_shape=[
            jax.ShapeDtypeStruct((t, hq), bf16),
            jax.ShapeDtypeStruct((t, hq), bf16),
            jax.ShapeDtypeStruct((t, hv), bf16),
        ],
        compiler_params=pltpu.CompilerParams(
            dimension_semantics=("arbitrary",), vmem_limit_bytes=VMEM_LIMIT),
        name="mla_prep",
    )(a, gq, gkv, wq, wk, wv, rope)


ATT_TQ = 256
ATT_TK = 256


def _dense_softmax_pv(q, k_ref, k_c0, v_ref, v_c0, s_ref, bias_fn, n_chunks):
    tq = q.shape[0]
    n_sub = ATT_TK // LANES

    def scores(c, m_part):
        r0 = pl.multiple_of(c * ATT_TK, ATT_TK)
        s = _nt_dot(q, k_ref[pl.ds(r0, ATT_TK), k_c0:k_c0 + LANES])
        if bias_fn is not None:
            s = s + bias_fn(c)
        s_ref[c] = s
        for u in range(n_sub):
            m_part = jnp.maximum(m_part, s[:, u * LANES:(u + 1) * LANES])
        return m_part

    m_part = lax.fori_loop(0, n_chunks, scores, jnp.full((tq, LANES), -jnp.inf, f32))
    m = jnp.max(m_part, axis=-1, keepdims=True)

    def weighted(c, carry):
        l_part, acc = carry
        p = jnp.exp(s_ref[c] - m)
        for u in range(n_sub):
            l_part = l_part + p[:, u * LANES:(u + 1) * LANES]
        r0 = pl.multiple_of(c * ATT_TK, ATT_TK)
        acc = acc + jnp.dot(p.astype(bf16), v_ref[pl.ds(r0, ATT_TK), v_c0:v_c0 + LANES],
                            preferred_element_type=f32)
        return l_part, acc

    zero = jnp.zeros((tq, LANES), f32)
    l_part, acc = lax.fori_loop(0, n_chunks, weighted, (zero, zero))
    return acc, jnp.sum(l_part, axis=-1, keepdims=True)


def _mla_attn_kernel(q_ref, k_ref, v_ref, o_ref, s_ref, *, n_chunks):
    outs = []
    for hh in range(2):
        q = q_ref[:, hh * LANES:(hh + 1) * LANES]
        acc, l = _dense_softmax_pv(q, k_ref, hh * LANES, v_ref, 0, s_ref, None, n_chunks)
        outs.append(acc / l)
    lane = lax.broadcasted_iota(jnp.int32, outs[0].shape, 1)
    o_ref[...] = jnp.where(lane < MLA_V, outs[0], outs[1]).astype(o_ref.dtype)


def _mla_attn(q, k, v, batch, seq):
    t = q.shape[0]
    nq = seq // ATT_TQ
    n_chunks = seq // ATT_TK
    n_pairs = MLA_HEADS // 2
    return pl.pallas_call(
        functools.partial(_mla_attn_kernel, n_chunks=n_chunks),
        grid=(batch, n_pairs, nq),
        in_specs=[
            pl.BlockSpec((ATT_TQ, 2 * LANES), lambda b, j, i: (b * nq + i, j)),
            pl.BlockSpec((seq, 2 * LANES), lambda b, j, i: (b, j)),
            pl.BlockSpec((seq, LANES), lambda b, j, i: (b, j)),
        ],
        out_specs=pl.BlockSpec((ATT_TQ, LANES), lambda b, j, i: (b * nq + i, j)),
        out_shape=jax.ShapeDtypeStruct((t, MLA_HEADS * MLA_V), bf16),
        scratch_shapes=[pltpu.VMEM((n_chunks, ATT_TQ, ATT_TK), f32)],
        compiler_params=pltpu.CompilerParams(
            dimension_semantics=("arbitrary", "arbitrary", "arbitrary"), vmem_limit_bytes=VMEM_LIMIT),
        name="mla_attn",
    )(q, k, v)


DIFF_SCALE = DIFF_QK ** -0.5
N_BIAS_TILES = 5


def _diff_attn_kernel(lam_ref, q0_ref, q1_ref, k0_ref, k1_ref, v_ref, bias_ref, gsub_ref, o_ref, s_ref,
                      *, n_chunks):
    qi = pl.program_id(2)
    lv = lam_ref[...]
    lam_init = lv[4:5, 0:1]
    lam = (jnp.exp(jnp.sum(lv[0:1] * lv[1:2], axis=-1, keepdims=True))
           - jnp.exp(jnp.sum(lv[2:3] * lv[3:4], axis=-1, keepdims=True)) + lam_init)
    lane = lax.broadcasted_iota(jnp.int32, (ATT_TQ, LANES), 1)
    q_refs = (q0_ref, q1_ref)
    k_refs = (k0_ref, k1_ref)
    for hh in range(2):
        head_lanes = (lane >= DIFF_QK) == (hh == 1)
        maps = []
        for m in range(2):
            q = q_refs[m][...]
            q = jnp.where(head_lanes, q * DIFF_SCALE, jnp.zeros_like(q))

            def bias_fn(c, m=m, hh=hh):
                return bias_ref[m, hh, jnp.clip(c - qi, -2, 2) + 2]

            acc, l = _dense_softmax_pv(q, k_refs[m], 0, v_ref, hh * DIFF_V, s_ref, bias_fn, n_chunks)
            maps.append(acc / l)
        o = maps[0] - lam * maps[1]
        o = _rms(o, gsub_ref[...]) * (1.0 - lam_init)
        o_ref[:, hh * DIFF_V:(hh + 1) * DIFF_V] = o.astype(o_ref.dtype)


def _diff_attn(lamvec, qkv, bias_tiles, gsub, batch, seq):
    t = qkv.shape[0]
    nq = seq // ATT_TQ
    n_chunks = seq // ATT_TK
    n_pairs = DIFF_HEADS // 2
    return pl.pallas_call(
        functools.partial(_diff_attn_kernel, n_chunks=n_chunks),
        grid=(batch, n_pairs, nq),
        in_specs=[
            pl.BlockSpec((8, DIFF_QK), lambda b, j, i: (0, 0)),
            pl.BlockSpec((ATT_TQ, LANES), lambda b, j, i: (b * nq + i, FQ_BLK + j)),
            pl.BlockSpec((ATT_TQ, LANES), lambda b, j, i: (b * nq + i, FQ_BLK + n_pairs + j)),
            pl.BlockSpec((seq, LANES), lambda b, j, i: (b, FK_BLK + j)),
            pl.BlockSpec((seq, LANES), lambda b, j, i: (b, FK_BLK + n_pairs + j)),
            pl.BlockSpec((seq, 2 * DIFF_V), lambda b, j, i: (b, FV_BLK // 2 + j)),
            pl.BlockSpec((2, 2, N_BIAS_TILES, ATT_TQ, ATT_TK), lambda b, j, i: (0, j, 0, 0, 0)),
            pl.BlockSpec((1, DIFF_V), lambda b, j, i: (0, 0)),
        ],
        out_specs=pl.BlockSpec((ATT_TQ, 2 * DIFF_V), lambda b, j, i: (b * nq + i, j)),
        out_shape=jax.ShapeDtypeStruct((t, DIFF_HEADS * DIFF_V), bf16),
        scratch_shapes=[pltpu.VMEM((n_chunks, ATT_TQ, ATT_TK), f32)],
        compiler_params=pltpu.CompilerParams(
            dimension_semantics=("arbitrary", "arbitrary", "arbitrary"), vmem_limit_bytes=VMEM_LIMIT),
        name="diff_attn",
    )(lamvec, qkv, qkv, qkv, qkv, qkv, bias_tiles, gsub)


DIL_TQ = 128
DIL_TW = DIL_TQ + 2 * DIL_HALF
DIL_SCALE = DIL_QK ** -0.5


def _dil_attn_kernel(q_ref, k_ref, v_ref, bm_ref, o_ref, lse_ref, *, sub_len, tl):
    tb = pl.program_id(2)
    lane = lax.broadcasted_iota(jnp.int32, (DIL_TQ, LANES), 1)
    col = lax.broadcasted_iota(jnp.int32, (DIL_TQ, DIL_TW), 1)

    def tile(i, carry):
        r_loc = pl.multiple_of(i * DIL_TQ, DIL_TQ)
        t0 = pl.multiple_of(tb * tl + i * DIL_TQ, DIL_TQ)
        lo = pl.multiple_of(jnp.maximum(t0 - DIL_HALF, 0), DIL_HALF)
        hi = pl.multiple_of(jnp.minimum(t0 + DIL_TQ, sub_len - DIL_HALF), DIL_HALF)
        c_lo = jnp.where(t0 == 0, DIL_HALF, 0)
        c_hi = jnp.where(t0 + DIL_TQ >= sub_len, DIL_HALF + DIL_TQ, DIL_TW)
        valid = (col >= c_lo) & (col < c_hi)
        qt = q_ref[pl.ds(r_loc, DIL_TQ), :]
        for h in range(DIL_HEADS):
            c0 = (h // 2) * LANES
            qh = qt[:, c0:c0 + LANES]
            qh = jnp.where((lane >= DIL_QK) == (h % 2 == 1), qh * DIL_SCALE, jnp.zeros_like(qh))
            ksl = slice(c0, c0 + LANES)
            vsl = slice(h * DIL_V, (h + 1) * DIL_V)
            kw = jnp.concatenate([k_ref[pl.ds(lo, DIL_HALF), ksl], k_ref[pl.ds(t0, DIL_TQ), ksl],
                                  k_ref[pl.ds(hi, DIL_HALF), ksl]], axis=0)
            vw = jnp.concatenate([v_ref[pl.ds(lo, DIL_HALF), vsl], v_ref[pl.ds(t0, DIL_TQ), vsl],
                                  v_ref[pl.ds(hi, DIL_HALF), vsl]], axis=0)
            s = _nt_dot(qh, kw) + bm_ref[h]
            s = jnp.where(valid, s, NEG)
            m = jnp.max(s, axis=-1, keepdims=True)
            p = jnp.exp(s - m)
            l = jnp.sum(p, axis=-1, keepdims=True)
            o = jnp.dot(p.astype(bf16), vw, preferred_element_type=f32) / l
            o_ref[pl.ds(r_loc, DIL_TQ), vsl] = o
            lse_ref[pl.ds(r_loc, DIL_TQ), vsl] = jnp.broadcast_to(m + jnp.log(l), (DIL_TQ, DIL_V))
        return carry

    lax.fori_loop(0, tl // DIL_TQ, tile, 0)


def _dil_attn(qkv, biasmask_g, g, dil, batch, seq):
    sub_len = seq // dil
    tl = min(sub_len, 1024)
    qkv3 = qkv.reshape(batch, sub_len, dil * QKV_COLS)
    qk_w = DIL_HEADS * DIL_QK
    v_w = DIL_HEADS * DIL_V
    q_blocks = QKV_COLS // qk_w
    v_blocks = QKV_COLS // v_w
    dk0 = DK_BLK * LANES // qk_w
    dv0 = DV_BLK * LANES // v_w
    out_shape = jax.ShapeDtypeStruct((batch, sub_len, dil * v_w), f32)
    o, lse = pl.pallas_call(
        functools.partial(_dil_attn_kernel, sub_len=sub_len, tl=tl),
        grid=(batch, dil, sub_len // tl),
        in_specs=[
            pl.BlockSpec((None, tl, qk_w), lambda b, r, i: (b, i, r * q_blocks + g)),
            pl.BlockSpec((None, sub_len, qk_w), lambda b, r, i: (b, 0, r * q_blocks + dk0 + g)),
            pl.BlockSpec((None, sub_len, v_w), lambda b, r, i: (b, 0, r * v_blocks + dv0)),
            pl.BlockSpec((DIL_HEADS, DIL_TQ, DIL_TW), lambda b, r, i: (0, 0, 0)),
        ],
        out_specs=[
            pl.BlockSpec((None, tl, v_w), lambda b, r, i: (b, i, r)),
            pl.BlockSpec((None, tl, v_w), lambda b, r, i: (b, i, r)),
        ],
        out_shape=[out_shape, out_shape],
        compiler_params=pltpu.CompilerParams(
            dimension_semantics=("arbitrary", "arbitrary", "arbitrary"), vmem_limit_bytes=VMEM_LIMIT),
        name=f"dil_attn_g{g}",
    )(qkv3, qkv3, qkv3, biasmask_g)
    return o.reshape(batch * seq, v_w), lse.reshape(batch * seq, v_w)


MERGE_TM = 256


def _merge_kernel(x_ref, oa_ref, oc_ref, o0_ref, o1_ref, o2_ref, l0_ref, l1_ref, l2_ref, gate_ref,
                  wb_ref, wo_ref, g_ref, out_ref):
    l0, l1, l2 = l0_ref[...], l1_ref[...], l2_ref[...]
    m = jnp.maximum(jnp.maximum(l0, l1), l2)
    e0, e1, e2 = jnp.exp(l0 - m), jnp.exp(l1 - m), jnp.exp(l2 - m)
    ob = (e0 * o0_ref[...] + e1 * o1_ref[...] + e2 * o2_ref[...]) / (e0 + e1 + e2)
    branches = (oa_ref[...], ob.astype(bf16), oc_ref[...])
    merged = None
    for n in range(N_BRANCH):
        y = jnp.dot(branches[n], wb_ref[n], preferred_element_type=f32)
        y = gate_ref[:, n * D_MODEL:(n + 1) * D_MODEL] * y
        merged = y if merged is None else merged + y
    y = jnp.dot(merged.astype(bf16), wo_ref[...], preferred_element_type=f32)
    out_ref[...] = x_ref[...] + _rms(y, g_ref[...])


def _merge(x, oa, oc, dil_outs, gates, wb, wo, g):
    t = x.shape[0]
    row = lambda i: (i, 0)
    bw = pl.BlockSpec((MERGE_TM, BRANCH_W), row)
    (o0, l0), (o1, l1), (o2, l2) = dil_outs
    return pl.pallas_call(
        _merge_kernel,
        grid=(t // MERGE_TM,),
        in_specs=[
            pl.BlockSpec((MERGE_TM, D_MODEL), row),
            bw, bw, bw, bw, bw, bw, bw, bw,
            pl.BlockSpec((MERGE_TM, GATE_COLS), row),
            pl.BlockSpec((N_BRANCH, BRANCH_W, D_MODEL), lambda i: (0, 0, 0)),
            pl.BlockSpec((D_MODEL, D_MODEL), lambda i: (0, 0)),
            pl.BlockSpec((1, D_MODEL), lambda i: (0, 0)),
        ],
        out_specs=pl.BlockSpec((MERGE_TM, D_MODEL), row),
        out_shape=jax.ShapeDtypeStruct((t, D_MODEL), f32),
        compiler_params=pltpu.CompilerParams(
            dimension_semantics=("arbitrary",), vmem_limit_bytes=VMEM_LIMIT),
        name="merge",
    )(x, oa, oc, o0, o1, o2, l0, l1, l2, gates, wb, wo, g)


MLP_TM = 1024
MLP_TF = 512


def _mlp_kernel(x_ref, gpre_ref, wup_ref, wdn_ref, gpost_ref, out_ref, h_ref, acc_ref):
    k = pl.program_id(1)

    @pl.when(k == 0)
    def _():
        h_ref[...] = _rms(x_ref[...], gpre_ref[...]).astype(bf16)
        acc_ref[...] = jnp.zeros_like(acc_ref)

    u = jnp.maximum(jnp.dot(h_ref[...], wup_ref[...], preferred_element_type=f32), 0.0)
    acc_ref[...] += jnp.dot((u * u).astype(bf16), wdn_ref[...], preferred_element_type=f32)

    @pl.when(k == pl.num_programs(1) - 1)
    def _():
        out_ref[...] = x_ref[...] + _rms(acc_ref[...], gpost_ref[...])


def _mlp(x, gpre, wup, wdn, gpost):
    t = x.shape[0]
    return pl.pallas_call(
        _mlp_kernel,
        grid=(t // MLP_TM, D_FF // MLP_TF),
        in_specs=[
            pl.BlockSpec((MLP_TM, D_MODEL), lambda i, k: (i, 0)),
            pl.BlockSpec((1, D_MODEL), lambda i, k: (0, 0)),
            pl.BlockSpec((D_MODEL, MLP_TF), lambda i, k: (0, k)),
            pl.BlockSpec((MLP_TF, D_MODEL), lambda i, k: (k, 0)),
            pl.BlockSpec((1, D_MODEL), lambda i, k: (0, 0)),
        ],
        out_specs=pl.BlockSpec((MLP_TM, D_MODEL), lambda i, k: (i, 0)),
        out_shape=jax.ShapeDtypeStruct((t, D_MODEL), f32),
        scratch_shapes=[pltpu.VMEM((MLP_TM, D_MODEL), bf16), pltpu.VMEM((MLP_TM, D_MODEL), f32)],
        compiler_params=pltpu.CompilerParams(
            dimension_semantics=("arbitrary", "arbitrary"), vmem_limit_bytes=VMEM_LIMIT),
        name="mlp",
    )(x, gpre, wup, wdn, gpost)


def _rel_bucket(rel):
    nb = REL_BUCKETS // 2
    max_exact = nb // 2
    ret = jnp.where(rel > 0, nb, 0)
    n = jnp.abs(rel)
    large = max_exact + (jnp.log(jnp.maximum(n, 1).astype(f32) / max_exact)
                         / math.log(REL_MAX_DIST / max_exact) * (nb - max_exact)).astype(jnp.int32)
    large = jnp.minimum(large, nb - 1)
    return ret + jnp.where(n < max_exact, n, large)


def _pad_in_weights(w_in):
    sizes = (MLA_Q_RANK, MLA_KV_RANK, MLA_ROPE)
    c_q = w_in[..., :sizes[0]]
    c_kv = w_in[..., sizes[0]:sizes[0] + sizes[1]]
    k_r = w_in[..., sizes[0] + sizes[1]:sum(sizes)]
    rest = w_in[..., sum(sizes):]
    z = lambda n: jnp.zeros(w_in.shape[:-1] + (n,), w_in.dtype)
    return jnp.concatenate([c_q, c_kv, z(MLA_NOPE), k_r, z(LANES - MLA_NOPE - MLA_ROPE), rest],
                           axis=-1).astype(bf16)


def _pad_mla_weights(w_uq, w_ukv):
    depth = w_uq.shape[0]
    wq = jnp.pad(w_uq, ((0, 0), (0, 0), (0, 0), (0, LANES - MLA_NOPE - MLA_ROPE)))
    wk = jnp.pad(w_ukv[..., :MLA_NOPE], ((0, 0), (0, 0), (0, 0), (0, LANES - MLA_NOPE)))
    wv = w_ukv[..., MLA_NOPE:]
    return (wq.reshape(depth, MLA_Q_RANK, MLA_HEADS * LANES).astype(bf16),
            wk.reshape(depth, MLA_KV_RANK, MLA_HEADS * LANES).astype(bf16),
            wv.reshape(depth, MLA_KV_RANK, MLA_HEADS * MLA_V).astype(bf16))


def _rope_tables(positions):
    inv = ROPE_BASE ** (-jnp.arange(ROPE_HALF, dtype=f32) / ROPE_HALF)
    ang = positions.reshape(-1).astype(f32)[:, None] * inv
    cos, sin = jnp.cos(ang), jnp.sin(ang)
    t = ang.shape[0]
    one = jnp.ones((t, MLA_NOPE), f32)
    z = lambda n: jnp.zeros((t, n), f32)
    tail = LANES - MLA_NOPE - MLA_ROPE
    c = jnp.concatenate([one, cos, cos, z(tail)], axis=-1)
    sa = jnp.concatenate([z(MLA_NOPE + ROPE_HALF), sin, z(tail)], axis=-1)
    sb = jnp.concatenate([z(MLA_NOPE), -sin, z(ROPE_HALF + tail)], axis=-1)
    return jnp.concatenate([c, sa, sb], axis=-1)


def _diff_bias_tiles(rel_bias):
    tab = rel_bias[:, DIL_BIAS_COLS:].astype(f32).reshape(REL_BUCKETS, 2, DIFF_HEADS)
    i = jnp.arange(ATT_TQ)[:, None]
    j = jnp.arange(ATT_TK)[None, :]
    offs = jnp.array([-2, -1, 0, 1, 2])[:, None, None] * ATT_TK
    bucket = _rel_bucket(offs + j - i)
    return jnp.transpose(tab[bucket], (3, 4, 0, 1, 2))


def _dil_bias_masks(rel_bias):
    i = jnp.arange(DIL_TQ)[:, None]
    c = jnp.arange(DIL_TW)[None, :]
    rel = c - DIL_HALF - i
    out = []
    for g, (_, dil) in enumerate(DIL_PATTERNS):
        tab = rel_bias[:, g * DIL_HEADS:(g + 1) * DIL_HEADS].astype(f32)
        b = jnp.transpose(tab[_rel_bucket(rel * dil)], (2, 0, 1))
        out.append(jnp.where((jnp.abs(rel) <= DIL_HALF)[None], b, NEG))
    return jnp.stack(out, axis=0)


def kernel(x, positions, rel_bias, g_mix_pre, w_in, g_q, w_uq, g_kv, w_ukv, lam_q1, lam_k1, lam_q2, lam_k2,
           g_diff_sub, w_branch, w_out, g_mix_post, g_mlp_pre, w_up, w_down, g_mlp_post):
    batch, seq, d = x.shape
    depth = w_in.shape[0]
    assert d == D_MODEL and seq % 1024 == 0 and seq // DIL_PATTERNS[-1][1] >= 2 * DIL_TQ

    w_in_p = _pad_in_weights(w_in)
    wq_p, wk_p, wv_p = _pad_mla_weights(w_uq, w_ukv)
    wb = w_branch.astype(bf16)
    wo = w_out.astype(bf16)
    wup = w_up.astype(bf16)
    wdn = w_down.astype(bf16)
    rope = _rope_tables(positions)
    diff_bias = _diff_bias_tiles(rel_bias)
    dil_bias = _dil_bias_masks(rel_bias)

    lam_init = jnp.array([0.8 - 0.6 * math.exp(-0.3 * l) for l in range(depth)], f32)
    lam_row = jnp.zeros((depth, 1, DIFF_QK), f32).at[:, 0, 0].set(lam_init)
    lamvec = jnp.concatenate([lam_q1[:, None], lam_k1[:, None], lam_q2[:, None], lam_k2[:, None],
                              lam_row, jnp.zeros((depth, 3, DIFF_QK), f32)], axis=1).astype(f32)

    row = lambda v: v.reshape(1, -1).astype(f32)
    xt = x.reshape(batch * seq, d)
    for l in range(depth):
        a, qkv, gates = _in_proj(xt, row(g_mix_pre[l]), w_in_p[l])
        q, k, v = _mla_prep(a, row(g_q[l]), row(g_kv[l]), wq_p[l], wk_p[l], wv_p[l], rope)
        o_a = _mla_attn(q, k, v, batch, seq)
        dil_outs = [_dil_attn(qkv, dil_bias[g], g, DIL_PATTERNS[g][1], batch, seq)
                    for g in range(DIL_GROUPS)]
        o_c = _diff_attn(lamvec[l], qkv, diff_bias, row(g_diff_sub[l]), batch, seq)
        xt = _merge(xt, o_a, o_c, dil_outs, gates, wb[l], wo[l], row(g_mix_post[l]))
        xt = _mlp(xt, row(g_mlp_pre[l]), wup[l], wdn[l], row(g_mlp_post[l]))
    return xt.reshape(batch, seq, d)
```

```python
import functools
import math

import jax
import jax.numpy as jnp
import numpy as np
from jax import lax
from jax.experimental import pallas as pl
from jax.experimental.pallas import tpu as pltpu

f32 = jnp.float32
bf16 = jnp.bfloat16

D_MODEL = 1024
MLA_HEADS = 8
MLA_Q_RANK = 256
MLA_KV_RANK = 128
MLA_NOPE = 64
MLA_ROPE = 32
MLA_V = 64
ROPE_BASE = 10000.0
DIL_PATTERNS = ((128, 1), (512, 4), (2048, 16))
DIL_GROUPS = 3
DIL_HEADS = 4
DIL_QK = 64
DIL_V = 128
DIL_HALF = 64
DIFF_HEADS = 4
DIFF_QK = 64
DIFF_V = 128
REL_BUCKETS = 32
REL_MAX_DIST = 128
DIL_BIAS_COLS = DIL_GROUPS * DIL_HEADS
N_BRANCH = 3
BRANCH_W = 512
D_FF = 4 * D_MODEL
EPS = 1e-6
NEG = -1e30

LANES = 128
A_COLS = 512
QKV_COLS = 3584
GATE_COLS = N_BRANCH * D_MODEL
IN_PAD_COLS = A_COLS + QKV_COLS + GATE_COLS
DQ_BLK, DK_BLK, DV_BLK, FQ_BLK, FK_BLK, FV_BLK = 0, 6, 12, 16, 20, 24

VMEM_LIMIT = 48 * 1024 * 1024


def _rms(x, g):
    return x * lax.rsqrt(jnp.mean(x * x, axis=-1, keepdims=True) + EPS) * g


def _nt_dot(a, b):
    return lax.dot_general(a, b, (((1,), (1,)), ((), ())), preferred_element_type=f32)


IN_TM = 1024
IN_TN = 512
_IN_QKV_J0 = A_COLS // IN_TN
_IN_GATE_J0 = (A_COLS + QKV_COLS) // IN_TN


def _in_proj_kernel(x_ref, g_ref, w_ref, a_ref, qkv_ref, gate_ref, h_ref):
    j = pl.program_id(1)

    @pl.when(j == 0)
    def _():
        h_ref[...] = _rms(x_ref[...], g_ref[...]).astype(bf16)

    z = jnp.dot(h_ref[...], w_ref[...], preferred_element_type=f32)

    @pl.when(j == 0)
    def _():
        a_ref[...] = z

    @pl.when((j >= _IN_QKV_J0) & (j < _IN_GATE_J0))
    def _():
        qkv_ref[...] = z.astype(bf16)

    @pl.when(j >= _IN_GATE_J0)
    def _():
        gate_ref[...] = jax.nn.sigmoid(z)


def _in_proj(x, g, w):
    t = x.shape[0]
    n_j = IN_PAD_COLS // IN_TN
    n_qkv = QKV_COLS // IN_TN
    n_gate = GATE_COLS // IN_TN
    return pl.pallas_call(
        _in_proj_kernel,
        grid=(t // IN_TM, n_j),
        in_specs=[
            pl.BlockSpec((IN_TM, D_MODEL), lambda i, j: (i, 0)),
            pl.BlockSpec((1, D_MODEL), lambda i, j: (0, 0)),
            pl.BlockSpec((D_MODEL, IN_TN), lambda i, j: (0, j)),
        ],
        out_specs=[
            pl.BlockSpec((IN_TM, IN_TN), lambda i, j: (i, 0)),
            pl.BlockSpec((IN_TM, IN_TN), lambda i, j: (i, jnp.clip(j - _IN_QKV_J0, 0, n_qkv - 1))),
            pl.BlockSpec((IN_TM, IN_TN), lambda i, j: (i, jnp.clip(j - _IN_GATE_J0, 0, n_gate - 1))),
        ],
        out_shape=[
            jax.ShapeDtypeStruct((t, A_COLS), f32),
            jax.ShapeDtypeStruct((t, QKV_COLS), bf16),
            jax.ShapeDtypeStruct((t, GATE_COLS), f32),
        ],
        scratch_shapes=[pltpu.VMEM((IN_TM, D_MODEL), bf16)],
        compiler_params=pltpu.CompilerParams(
            dimension_semantics=("arbitrary", "arbitrary"), vmem_limit_bytes=VMEM_LIMIT),
        name="in_proj",
    )(x, g, w)


PREP_TM = 512
MLA_SCALE = (MLA_NOPE + MLA_ROPE) ** -0.5
ROPE_HALF = MLA_ROPE // 2


def _rope_lanes(x, rope):
    c = rope[:, 0:LANES]
    sa = rope[:, LANES:2 * LANES]
    sb = rope[:, 2 * LANES:3 * LANES]
    return (x * c + pltpu.roll(x, ROPE_HALF, 1) * sa + pltpu.roll(x, LANES - ROPE_HALF, 1) * sb)


def _mla_prep_kernel(a_ref, gq_ref, gkv_ref, wq_ref, wk_ref, wv_ref, rope_ref, q_ref, k_ref, v_ref):
    a = a_ref[...]
    rope = rope_ref[...]
    cq = _rms(a[:, :MLA_Q_RANK], gq_ref[...]).astype(bf16)
    ckv = _rms(a[:, MLA_Q_RANK:MLA_Q_RANK + MLA_KV_RANK], gkv_ref[...]).astype(bf16)
    k_rope = _rope_lanes(a[:, A_COLS - LANES:], rope)
    qf = jnp.dot(cq, wq_ref[...], preferred_element_type=f32)
    kf = jnp.dot(ckv, wk_ref[...], preferred_element_type=f32)
    v_ref[...] = jnp.dot(ckv, wv_ref[...], preferred_element_type=f32).astype(bf16)
    for h in range(MLA_HEADS):
        sl = slice(h * LANES, (h + 1) * LANES)
        q_ref[:, sl] = (_rope_lanes(qf[:, sl], rope) * MLA_SCALE).astype(bf16)
        k_ref[:, sl] = (kf[:, sl] + k_rope).astype(bf16)


def _mla_prep(a, gq, gkv, wq, wk, wv, rope):
    t = a.shape[0]
    hq = MLA_HEADS * LANES
    hv = MLA_HEADS * MLA_V
    const = lambda i: (0, 0)
    return pl.pallas_call(
        _mla_prep_kernel,
        grid=(t // PREP_TM,),
        in_specs=[
            pl.BlockSpec((PREP_TM, A_COLS), lambda i: (i, 0)),
            pl.BlockSpec((1, MLA_Q_RANK), const),
            pl.BlockSpec((1, MLA_KV_RANK), const),
            pl.BlockSpec((MLA_Q_RANK, hq), const),
            pl.BlockSpec((MLA_KV_RANK, hq), const),
            pl.BlockSpec((MLA_KV_RANK, hv), const),
            pl.BlockSpec((PREP_TM, 3 * LANES), lambda i: (i, 0)),
        ],
        out_specs=[
            pl.BlockSpec((PREP_TM, hq), lambda i: (i, 0)),
            pl.BlockSpec((PREP_TM, hq), lambda i: (i, 0)),
            pl.BlockSpec((PREP_TM, hv), lambda i: (i, 0)),
        ],
        out_shape=[
            jax.ShapeDtypeStruct((t, hq), bf16),
            jax.ShapeDtypeStruct((t, hq), bf16),
            jax.ShapeDtypeStruct((t, hv), bf16),
        ],
        compiler_params=pltpu.CompilerParams(
            dimension_semantics=("arbitrary",), vmem_limit_bytes=VMEM_LIMIT),
        name="mla_prep",
    )(a, gq, gkv, wq, wk, wv, rope)


ATT_TQ = 256
ATT_TK = 256


def _dense_softmax_pv(q, k, v, s_ref, p_ref, bias_fn):
    tq = q.shape[0]
    n_chunks = k.shape[0] // ATT_TK
    n_sub = ATT_TK // LANES
    s = _nt_dot(q, k)
    m_part = jnp.full((tq, LANES), -jnp.inf, f32)
    for c in range(n_chunks):
        sc = s[:, c * ATT_TK:(c + 1) * ATT_TK]
        if bias_fn is not None:
            sc = sc + bias_fn(c)
        s_ref[:, c * ATT_TK:(c + 1) * ATT_TK] = sc
        for u in range(n_sub):
            m_part = jnp.maximum(m_part, sc[:, u * LANES:(u + 1) * LANES])
    m = jnp.max(m_part, axis=-1, keepdims=True)
    l_part = jnp.zeros((tq, LANES), f32)
    for c in range(n_chunks):
        p = jnp.exp(s_ref[:, c * ATT_TK:(c + 1) * ATT_TK] - m)
        for u in range(n_sub):
            l_part = l_part + p[:, u * LANES:(u + 1) * LANES]
        p_ref[:, c * ATT_TK:(c + 1) * ATT_TK] = p.astype(bf16)
    acc = jnp.dot(p_ref[...], v, preferred_element_type=f32)
    return acc, jnp.sum(l_part, axis=-1, keepdims=True)


def _mla_attn_kernel(q_ref, k_ref, v_ref, o_ref, s_ref, p_ref):
    outs = []
    for hh in range(2):
        q = q_ref[:, hh * LANES:(hh + 1) * LANES]
        acc, l = _dense_softmax_pv(q, k_ref[:, hh * LANES:(hh + 1) * LANES], v_ref[...], s_ref, p_ref, None)
        outs.append(acc / l)
    lane = lax.broadcasted_iota(jnp.int32, outs[0].shape, 1)
    o_ref[...] = jnp.where(lane < MLA_V, outs[0], outs[1]).astype(o_ref.dtype)


def _mla_attn(q, k, v, batch, seq):
    t = q.shape[0]
    nq = seq // ATT_TQ
    n_pairs = MLA_HEADS // 2
    return pl.pallas_call(
        _mla_attn_kernel,
        grid=(batch, n_pairs, nq),
        in_specs=[
            pl.BlockSpec((ATT_TQ, 2 * LANES), lambda b, j, i: (b * nq + i, j)),
            pl.BlockSpec((seq, 2 * LANES), lambda b, j, i: (b, j)),
            pl.BlockSpec((seq, LANES), lambda b, j, i: (b, j)),
        ],
        out_specs=pl.BlockSpec((ATT_TQ, LANES), lambda b, j, i: (b * nq + i, j)),
        out_shape=jax.ShapeDtypeStruct((t, MLA_HEADS * MLA_V), bf16),
        scratch_shapes=[pltpu.VMEM((ATT_TQ, seq), f32), pltpu.VMEM((ATT_TQ, seq), bf16)],
        compiler_params=pltpu.CompilerParams(
            dimension_semantics=("arbitrary", "arbitrary", "arbitrary"), vmem_limit_bytes=VMEM_LIMIT),
        name="mla_attn",
    )(q, k, v)


DIFF_SCALE = DIFF_QK ** -0.5
N_BIAS_TILES = 5


def _diff_attn_kernel(lam_ref, q0_ref, q1_ref, k0_ref, k1_ref, v_ref, bias_ref, gsub_ref, o_ref, s_ref, p_ref):
    qi = pl.program_id(2)
    lv = lam_ref[...]
    lam_init = lv[4:5, 0:1]
    lam = (jnp.exp(jnp.sum(lv[0:1] * lv[1:2], axis=-1, keepdims=True))
           - jnp.exp(jnp.sum(lv[2:3] * lv[3:4], axis=-1, keepdims=True)) + lam_init)
    lane = lax.broadcasted_iota(jnp.int32, (ATT_TQ, LANES), 1)
    q_refs = (q0_ref, q1_ref)
    k_refs = (k0_ref, k1_ref)
    for hh in range(2):
        head_lanes = (lane >= DIFF_QK) == (hh == 1)
        maps = []
        for m in range(2):
            q = q_refs[m][...]
            q = jnp.where(head_lanes, q * DIFF_SCALE, jnp.zeros_like(q))

            def bias_fn(c, m=m, hh=hh):
                return bias_ref[m, hh, jnp.clip(c - qi, -2, 2) + 2]

            acc, l = _dense_softmax_pv(q, k_refs[m][...], v_ref[:, hh * DIFF_V:(hh + 1) * DIFF_V],
                                       s_ref, p_ref, bias_fn)
            maps.append(acc / l)
        o = maps[0] - lam * maps[1]
        o = _rms(o, gsub_ref[...]) * (1.0 - lam_init)
        o_ref[:, hh * DIFF_V:(hh + 1) * DIFF_V] = o.astype(o_ref.dtype)


def _diff_attn(lamvec, qkv, bias_tiles, gsub, batch, seq):
    t = qkv.shape[0]
    nq = seq // ATT_TQ
    n_pairs = DIFF_HEADS // 2
    return pl.pallas_call(
        _diff_attn_kernel,
        grid=(batch, n_pairs, nq),
        in_specs=[
            pl.BlockSpec((8, DIFF_QK), lambda b, j, i: (0, 0)),
            pl.BlockSpec((ATT_TQ, LANES), lambda b, j, i: (b * nq + i, FQ_BLK + j)),
            pl.BlockSpec((ATT_TQ, LANES), lambda b, j, i: (b * nq + i, FQ_BLK + n_pairs + j)),
            pl.BlockSpec((seq, LANES), lambda b, j, i: (b, FK_BLK + j)),
            pl.BlockSpec((seq, LANES), lambda b, j, i: (b, FK_BLK + n_pairs + j)),
            pl.BlockSpec((seq, 2 * DIFF_V), lambda b, j, i: (b, FV_BLK // 2 + j)),
            pl.BlockSpec((2, 2, N_BIAS_TILES, ATT_TQ, ATT_TK), lambda b, j, i: (0, j, 0, 0, 0)),
            pl.BlockSpec((1, DIFF_V), lambda b, j, i: (0, 0)),
        ],
        out_specs=pl.BlockSpec((ATT_TQ, 2 * DIFF_V), lambda b, j, i: (b * nq + i, j)),
        out_shape=jax.ShapeDtypeStruct((t, DIFF_HEADS * DIFF_V), bf16),
        scratch_shapes=[pltpu.VMEM((ATT_TQ, seq), f32), pltpu.VMEM((ATT_TQ, seq), bf16)],
        compiler_params=pltpu.CompilerParams(
            dimension_semantics=("arbitrary", "arbitrary", "arbitrary"), vmem_limit_bytes=VMEM_LIMIT),
        name="diff_attn",
    )(lamvec, qkv, qkv, qkv, qkv, qkv, bias_tiles, gsub)


DIL_TQ = 128
DIL_TW = DIL_TQ + 2 * DIL_HALF
DIL_SCALE = DIL_QK ** -0.5


def _dil_attn_kernel(q_ref, k_ref, v_ref, bm_ref, o_ref, lse_ref, *, sub_len, tl):
    tb = pl.program_id(2)
    lane = lax.broadcasted_iota(jnp.int32, (DIL_TQ, LANES), 1)
    col = lax.broadcasted_iota(jnp.int32, (DIL_TQ, DIL_TW), 1)

    def tile(i, carry):
        r_loc = pl.multiple_of(i * DIL_TQ, DIL_TQ)
        t0 = pl.multiple_of(tb * tl + i * DIL_TQ, DIL_TQ)
        lo = pl.multiple_of(jnp.maximum(t0 - DIL_HALF, 0), DIL_HALF)
        hi = pl.multiple_of(jnp.minimum(t0 + DIL_TQ, sub_len - DIL_HALF), DIL_HALF)
        c_lo = jnp.where(t0 == 0, DIL_HALF, 0)
        c_hi = jnp.where(t0 + DIL_TQ >= sub_len, DIL_HALF + DIL_TQ, DIL_TW)
        valid = (col >= c_lo) & (col < c_hi)
        qt = q_ref[pl.ds(r_loc, DIL_TQ), :]
        for h in range(DIL_HEADS):
            c0 = (h // 2) * LANES
            qh = qt[:, c0:c0 + LANES]
            qh = jnp.where((lane >= DIL_QK) == (h % 2 == 1), qh * DIL_SCALE, jnp.zeros_like(qh))
            ksl = slice(c0, c0 + LANES)
            vsl = slice(h * DIL_V, (h + 1) * DIL_V)
            kw = jnp.concatenate([k_ref[pl.ds(lo, DIL_HALF), ksl], k_ref[pl.ds(t0, DIL_TQ), ksl],
                                  k_ref[pl.ds(hi, DIL_HALF), ksl]], axis=0)
            vw = jnp.concatenate([v_ref[pl.ds(lo, DIL_HALF), vsl], v_ref[pl.ds(t0, DIL_TQ), vsl],
                                  v_ref[pl.ds(hi, DIL_HALF), vsl]], axis=0)
            s = _nt_dot(qh, kw) + bm_ref[h]
            s = jnp.where(valid, s, NEG)
            m = jnp.max(s, axis=-1, keepdims=True)
            p = jnp.exp(s - m)
            l = jnp.sum(p, axis=-1, keepdims=True)
            o = jnp.dot(p.astype(bf16), vw, preferred_element_type=f32) / l
            o_ref[pl.ds(r_loc, DIL_TQ), vsl] = o
            lse_ref[pl.ds(r_loc, DIL_TQ), vsl] = jnp.broadcast_to(m + jnp.log(l), (DIL_TQ, DIL_V))
        return carry

    lax.fori_loop(0, tl // DIL_TQ, tile, 0)


def _dil_attn(qkv, biasmask_g, g, dil, batch, seq):
    sub_len = seq // dil
    tl = min(sub_len, 1024)
    qkv3 = qkv.reshape(batch, sub_len, dil * QKV_COLS)
    qk_w = DIL_HEADS * DIL_QK
    v_w = DIL_HEADS * DIL_V
    q_blocks = QKV_COLS // qk_w
    v_blocks = QKV_COLS // v_w
    dk0 = DK_BLK * LANES // qk_w
    dv0 = DV_BLK * LANES // v_w
    out_shape = jax.ShapeDtypeStruct((batch, sub_len, dil * v_w), f32)
    o, lse = pl.pallas_call(
        functools.partial(_dil_attn_kernel, sub_len=sub_len, tl=tl),
        grid=(batch, dil, sub_len // tl),
        in_specs=[
            pl.BlockSpec((None, tl, qk_w), lambda b, r, i: (b, i, r * q_blocks + g)),
            pl.BlockSpec((None, sub_len, qk_w), lambda b, r, i: (b, 0, r * q_blocks + dk0 + g)),
            pl.BlockSpec((None, sub_len, v_w), lambda b, r, i: (b, 0, r * v_blocks + dv0)),
            pl.BlockSpec((DIL_HEADS, DIL_TQ, DIL_TW), lambda b, r, i: (0, 0, 0)),
        ],
        out_specs=[
            pl.BlockSpec((None, tl, v_w), lambda b, r, i: (b, i, r)),
            pl.BlockSpec((None, tl, v_w), lambda b, r, i: (b, i, r)),
        ],
        out_shape=[out_shape, out_shape],
        compiler_params=pltpu.CompilerParams(
            dimension_semantics=("arbitrary", "arbitrary", "arbitrary"), vmem_limit_bytes=VMEM_LIMIT),
        name=f"dil_attn_g{g}",
    )(qkv3, qkv3, qkv3, biasmask_g)
    return o.reshape(batch * seq, v_w), lse.reshape(batch * seq, v_w)


MERGE_TM = 256


def _merge_kernel(x_ref, oa_ref, oc_ref, o0_ref, o1_ref, o2_ref, l0_ref, l1_ref, l2_ref, gate_ref,
                  wb_ref, wo_ref, g_ref, out_ref):
    l0, l1, l2 = l0_ref[...], l1_ref[...], l2_ref[...]
    m = jnp.maximum(jnp.maximum(l0, l1), l2)
    e0, e1, e2 = jnp.exp(l0 - m), jnp.exp(l1 - m), jnp.exp(l2 - m)
    ob = (e0 * o0_ref[...] + e1 * o1_ref[...] + e2 * o2_ref[...]) / (e0 + e1 + e2)
    branches = (oa_ref[...], ob.astype(bf16), oc_ref[...])
    merged = None
    for n in range(N_BRANCH):
        y = jnp.dot(branches[n], wb_ref[n], preferred_element_type=f32)
        y = gate_ref[:, n * D_MODEL:(n + 1) * D_MODEL] * y
        merged = y if merged is None else merged + y
    y = jnp.dot(merged.astype(bf16), wo_ref[...], preferred_element_type=f32)
    out_ref[...] = x_ref[...] + _rms(y, g_ref[...])


def _merge(x, oa, oc, dil_outs, gates, wb, wo, g):
    t = x.shape[0]
    row = lambda i: (i, 0)
    bw = pl.BlockSpec((MERGE_TM, BRANCH_W), row)
    (o0, l0), (o1, l1), (o2, l2) = dil_outs
    return pl.pallas_call(
        _merge_kernel,
        grid=(t // MERGE_TM,),
        in_specs=[
            pl.BlockSpec((MERGE_TM, D_MODEL), row),
            bw, bw, bw, bw, bw, bw, bw, bw,
            pl.BlockSpec((MERGE_TM, GATE_COLS), row),
            pl.BlockSpec((N_BRANCH, BRANCH_W, D_MODEL), lambda i: (0, 0, 0)),
            pl.BlockSpec((D_MODEL, D_MODEL), lambda i: (0, 0)),
            pl.BlockSpec((1, D_MODEL), lambda i: (0, 0)),
        ],
        out_specs=pl.BlockSpec((MERGE_TM, D_MODEL), row),
        out_shape=jax.ShapeDtypeStruct((t, D_MODEL), f32),
        compiler_params=pltpu.CompilerParams(
            dimension_semantics=("arbitrary",), vmem_limit_bytes=VMEM_LIMIT),
        name="merge",
    )(x, oa, oc, o0, o1, o2, l0, l1, l2, gates, wb, wo, g)


MLP_TM = 1024
MLP_TF = 512


def _mlp_kernel(x_ref, gpre_ref, wup_ref, wdn_ref, gpost_ref, out_ref, h_ref, acc_ref):
    k = pl.program_id(1)

    @pl.when(k == 0)
    def _():
        h_ref[...] = _rms(x_ref[...], gpre_ref[...]).astype(bf16)
        acc_ref[...] = jnp.zeros_like(acc_ref)

    u = jnp.maximum(jnp.dot(h_ref[...], wup_ref[...], preferred_element_type=f32), 0.0)
    acc_ref[...] += jnp.dot((u * u).astype(bf16), wdn_ref[...], preferred_element_type=f32)

    @pl.when(k == pl.num_programs(1) - 1)
    def _():
        out_ref[...] = x_ref[...] + _rms(acc_ref[...], gpost_ref[...])


def _mlp(x, gpre, wup, wdn, gpost):
    t = x.shape[0]
    return pl.pallas_call(
        _mlp_kernel,
        grid=(t // MLP_TM, D_FF // MLP_TF),
        in_specs=[
            pl.BlockSpec((MLP_TM, D_MODEL), lambda i, k: (i, 0)),
            pl.BlockSpec((1, D_MODEL), lambda i, k: (0, 0)),
            pl.BlockSpec((D_MODEL, MLP_TF), lambda i, k: (0, k)),
            pl.BlockSpec((MLP_TF, D_MODEL), lambda i, k: (k, 0)),
            pl.BlockSpec((1, D_MODEL), lambda i, k: (0, 0)),
        ],
        out_specs=pl.BlockSpec((MLP_TM, D_MODEL), lambda i, k: (i, 0)),
        out_shape=jax.ShapeDtypeStruct((t, D_MODEL), f32),
        scratch_shapes=[pltpu.VMEM((MLP_TM, D_MODEL), bf16), pltpu.VMEM((MLP_TM, D_MODEL), f32)],
        compiler_params=pltpu.CompilerParams(
            dimension_semantics=("arbitrary", "arbitrary"), vmem_limit_bytes=VMEM_LIMIT),
        name="mlp",
    )(x, gpre, wup, wdn, gpost)


def _rel_bucket(rel):
    nb = REL_BUCKETS // 2
    max_exact = nb // 2
    ret = jnp.where(rel > 0, nb, 0)
    n = jnp.abs(rel)
    large = max_exact + (jnp.log(jnp.maximum(n, 1).astype(f32) / max_exact)
                         / math.log(REL_MAX_DIST / max_exact) * (nb - max_exact)).astype(jnp.int32)
    large = jnp.minimum(large, nb - 1)
    return ret + jnp.where(n < max_exact, n, large)


def _bias_lookup(tab, bucket):
    one_hot = jax.nn.one_hot(bucket, REL_BUCKETS, dtype=f32)
    return jnp.einsum('...b,bc->...c', one_hot, tab.astype(f32), precision=lax.Precision.HIGHEST)


def _pad_in_weights(w_in):
    sizes = (MLA_Q_RANK, MLA_KV_RANK, MLA_ROPE)
    c_q = w_in[..., :sizes[0]]
    c_kv = w_in[..., sizes[0]:sizes[0] + sizes[1]]
    k_r = w_in[..., sizes[0] + sizes[1]:sum(sizes)]
    rest = w_in[..., sum(sizes):]
    z = lambda n: jnp.zeros(w_in.shape[:-1] + (n,), w_in.dtype)
    return jnp.concatenate([c_q, c_kv, z(MLA_NOPE), k_r, z(LANES - MLA_NOPE - MLA_ROPE), rest],
                           axis=-1).astype(bf16)


def _pad_mla_weights(w_uq, w_ukv):
    depth = w_uq.shape[0]
    wq = jnp.pad(w_uq, ((0, 0), (0, 0), (0, 0), (0, LANES - MLA_NOPE - MLA_ROPE)))
    wk = jnp.pad(w_ukv[..., :MLA_NOPE], ((0, 0), (0, 0), (0, 0), (0, LANES - MLA_NOPE)))
    wv = w_ukv[..., MLA_NOPE:]
    return (wq.reshape(depth, MLA_Q_RANK, MLA_HEADS * LANES).astype(bf16),
            wk.reshape(depth, MLA_KV_RANK, MLA_HEADS * LANES).astype(bf16),
            wv.reshape(depth, MLA_KV_RANK, MLA_HEADS * MLA_V).astype(bf16))


def _rope_tables(positions):
    inv = ROPE_BASE ** (-jnp.arange(ROPE_HALF, dtype=f32) / ROPE_HALF)
    ang = positions.reshape(-1).astype(f32)[:, None] * inv
    cos, sin = jnp.cos(ang), jnp.sin(ang)
    t = ang.shape[0]
    one = jnp.ones((t, MLA_NOPE), f32)
    z = lambda n: jnp.zeros((t, n), f32)
    tail = LANES - MLA_NOPE - MLA_ROPE
    c = jnp.concatenate([one, cos, cos, z(tail)], axis=-1)
    sa = jnp.concatenate([z(MLA_NOPE + ROPE_HALF), sin, z(tail)], axis=-1)
    sb = jnp.concatenate([z(MLA_NOPE), -sin, z(ROPE_HALF + tail)], axis=-1)
    return jnp.concatenate([c, sa, sb], axis=-1)


def _diff_bias_tiles(rel_bias):
    tab = rel_bias[:, DIL_BIAS_COLS:]
    i = jnp.arange(ATT_TQ)[:, None]
    j = jnp.arange(ATT_TK)[None, :]
    offs = jnp.array([-2, -1, 0, 1, 2])[:, None, None] * ATT_TK
    bucket = _rel_bucket(offs + j - i)
    tiles = jnp.transpose(_bias_lookup(tab, bucket), (3, 0, 1, 2))
    return tiles.reshape(2, DIFF_HEADS, N_BIAS_TILES, ATT_TQ, ATT_TK)


def _dil_bias_masks(rel_bias):
    i = jnp.arange(DIL_TQ)[:, None]
    c = jnp.arange(DIL_TW)[None, :]
    rel = c - DIL_HALF - i
    out = []
    for g, (_, dil) in enumerate(DIL_PATTERNS):
        tab = rel_bias[:, g * DIL_HEADS:(g + 1) * DIL_HEADS]
        b = jnp.transpose(_bias_lookup(tab, _rel_bucket(rel * dil)), (2, 0, 1))
        out.append(jnp.where((jnp.abs(rel) <= DIL_HALF)[None], b, NEG))
    return jnp.stack(out, axis=0)


def kernel(x, positions, rel_bias, g_mix_pre, w_in, g_q, w_uq, g_kv, w_ukv, lam_q1, lam_k1, lam_q2, lam_k2,
           g_diff_sub, w_branch, w_out, g_mix_post, g_mlp_pre, w_up, w_down, g_mlp_post):
    batch, seq, d = x.shape
    depth = w_in.shape[0]
    assert d == D_MODEL and seq % 1024 == 0 and seq // DIL_PATTERNS[-1][1] >= 2 * DIL_TQ

    w_in_p = _pad_in_weights(w_in)
    wq_p, wk_p, wv_p = _pad_mla_weights(w_uq, w_ukv)
    wb = w_branch.astype(bf16)
    wo = w_out.astype(bf16)
    wup = w_up.astype(bf16)
    wdn = w_down.astype(bf16)
    rope = _rope_tables(positions)
    diff_bias = _diff_bias_tiles(rel_bias)
    dil_bias = _dil_bias_masks(rel_bias)

    lam_init = jnp.array([0.8 - 0.6 * math.exp(-0.3 * l) for l in range(depth)], f32)
    lam_row = jnp.zeros((depth, 1, DIFF_QK), f32).at[:, 0, 0].set(lam_init)
    lamvec = jnp.concatenate([lam_q1[:, None], lam_k1[:, None], lam_q2[:, None], lam_k2[:, None],
                              lam_row, jnp.zeros((depth, 3, DIFF_QK), f32)], axis=1).astype(f32)

    row = lambda v: v.reshape(1, -1).astype(f32)
    xt = x.reshape(batch * seq, d)
    for l in range(depth):
        a, qkv, gates = _in_proj(xt, row(g_mix_pre[l]), w_in_p[l])
        q, k, v = _mla_prep(a, row(g_q[l]), row(g_kv[l]), wq_p[l], wk_p[l], wv_p[l], rope)
        o_a = _mla_attn(q, k, v, batch, seq)
        dil_outs = [_dil_attn(qkv, dil_bias[g], g, DIL_PATTERNS[g][1], batch, seq)
                    for g in range(DIL_GROUPS)]
        o_c = _diff_attn(lamvec[l], qkv, diff_bias, row(g_diff_sub[l]), batch, seq)
        xt = _merge(xt, o_a, o_c, dil_outs, gates, wb[l], wo[l], row(g_mix_post[l]))
        xt = _mlp(xt, row(g_mlp_pre[l]), wup[l], wdn[l], row(g_mlp_post[l]))
    return xt.reshape(batch, seq, d)
```

```python
import functools
import math

import jax
import jax.numpy as jnp
import numpy as np
from jax import lax
from jax.experimental import pallas as pl
from jax.experimental.pallas import tpu as pltpu

f32 = jnp.float32
bf16 = jnp.bfloat16

D_MODEL = 1024
MLA_HEADS = 8
MLA_Q_RANK = 256
MLA_KV_RANK = 128
MLA_NOPE = 64
MLA_ROPE = 32
MLA_V = 64
ROPE_BASE = 10000.0
DIL_PATTERNS = ((128, 1), (512, 4), (2048, 16))
DIL_GROUPS = 3
DIL_HEADS = 4
DIL_QK = 64
DIL_V = 128
DIL_HALF = 64
DIFF_HEADS = 4
DIFF_QK = 64
DIFF_V = 128
REL_BUCKETS = 32
REL_MAX_DIST = 128
DIL_BIAS_COLS = DIL_GROUPS * DIL_HEADS
N_BRANCH = 3
BRANCH_W = 512
D_FF = 4 * D_MODEL
EPS = 1e-6
NEG = -1e30

LANES = 128
A_COLS = 512
QKV_COLS = 3584
GATE_COLS = N_BRANCH * D_MODEL
IN_PAD_COLS = A_COLS + QKV_COLS + GATE_COLS
DQ_BLK, DK_BLK, DV_BLK, FQ_BLK, FK_BLK, FV_BLK = 0, 6, 12, 16, 20, 24

VMEM_LIMIT = 48 * 1024 * 1024


def _rms(x, g):
    return x * lax.rsqrt(jnp.mean(x * x, axis=-1, keepdims=True) + EPS) * g


def _nt_dot(a, b):
    return lax.dot_general(a, b, (((1,), (1,)), ((), ())), preferred_element_type=f32)


IN_TM = 1024
IN_TN = 512
_IN_QKV_J0 = A_COLS // IN_TN
_IN_GATE_J0 = (A_COLS + QKV_COLS) // IN_TN


def _in_proj_kernel(x_ref, g_ref, w_ref, a_ref, qkv_ref, gate_ref, h_ref):
    j = pl.program_id(1)

    @pl.when(j == 0)
    def _():
        h_ref[...] = _rms(x_ref[...], g_ref[...]).astype(bf16)

    z = jnp.dot(h_ref[...], w_ref[...], preferred_element_type=f32)

    @pl.when(j == 0)
    def _():
        a_ref[...] = z

    @pl.when((j >= _IN_QKV_J0) & (j < _IN_GATE_J0))
    def _():
        qkv_ref[...] = z.astype(bf16)

    @pl.when(j >= _IN_GATE_J0)
    def _():
        gate_ref[...] = jax.nn.sigmoid(z)


def _in_proj(x, g, w):
    t = x.shape[0]
    n_j = IN_PAD_COLS // IN_TN
    n_qkv = QKV_COLS // IN_TN
    n_gate = GATE_COLS // IN_TN
    return pl.pallas_call(
        _in_proj_kernel,
        grid=(t // IN_TM, n_j),
        in_specs=[
            pl.BlockSpec((IN_TM, D_MODEL), lambda i, j: (i, 0)),
            pl.BlockSpec((1, D_MODEL), lambda i, j: (0, 0)),
            pl.BlockSpec((D_MODEL, IN_TN), lambda i, j: (0, j)),
        ],
        out_specs=[
            pl.BlockSpec((IN_TM, IN_TN), lambda i, j: (i, 0)),
            pl.BlockSpec((IN_TM, IN_TN), lambda i, j: (i, jnp.clip(j - _IN_QKV_J0, 0, n_qkv - 1))),
            pl.BlockSpec((IN_TM, IN_TN), lambda i, j: (i, jnp.clip(j - _IN_GATE_J0, 0, n_gate - 1))),
        ],
        out_shape=[
            jax.ShapeDtypeStruct((t, A_COLS), f32),
            jax.ShapeDtypeStruct((t, QKV_COLS), bf16),
            jax.ShapeDtypeStruct((t, GATE_COLS), f32),
        ],
        scratch_shapes=[pltpu.VMEM((IN_TM, D_MODEL), bf16)],
        compiler_params=pltpu.CompilerParams(
            dimension_semantics=("arbitrary", "arbitrary"), vmem_limit_bytes=VMEM_LIMIT),
        name="in_proj",
    )(x, g, w)


PREP_TM = 512
MLA_SCALE = (MLA_NOPE + MLA_ROPE) ** -0.5
ROPE_HALF = MLA_ROPE // 2


def _rope_lanes(x, rope):
    c = rope[:, 0:LANES]
    sa = rope[:, LANES:2 * LANES]
    sb = rope[:, 2 * LANES:3 * LANES]
    return (x * c + pltpu.roll(x, ROPE_HALF, 1) * sa + pltpu.roll(x, LANES - ROPE_HALF, 1) * sb)


def _mla_prep_kernel(a_ref, gq_ref, gkv_ref, wq_ref, wk_ref, wv_ref, rope_ref, q_ref, k_ref, v_ref):
    a = a_ref[...]
    rope = rope_ref[...]
    cq = _rms(a[:, :MLA_Q_RANK], gq_ref[...]).astype(bf16)
    ckv = _rms(a[:, MLA_Q_RANK:MLA_Q_RANK + MLA_KV_RANK], gkv_ref[...]).astype(bf16)
    k_rope = _rope_lanes(a[:, A_COLS - LANES:], rope)
    qf = jnp.dot(cq, wq_ref[...], preferred_element_type=f32)
    kf = jnp.dot(ckv, wk_ref[...], preferred_element_type=f32)
    v_ref[...] = jnp.dot(ckv, wv_ref[...], preferred_element_type=f32).astype(bf16)
    for h in range(MLA_HEADS):
        sl = slice(h * LANES, (h + 1) * LANES)
        q_ref[:, sl] = (_rope_lanes(qf[:, sl], rope) * MLA_SCALE).astype(bf16)
        k_ref[:, sl] = (kf[:, sl] + k_rope).astype(bf16)


def _mla_prep(a, gq, gkv, wq, wk, wv, rope):
    t = a.shape[0]
    hq = MLA_HEADS * LANES
    hv = MLA_HEADS * MLA_V
    const = lambda i: (0, 0)
    return pl.pallas_call(
        _mla_prep_kernel,
        grid=(t // PREP_TM,),
        in_specs=[
            pl.BlockSpec((PREP_TM, A_COLS), lambda i: (i, 0)),
            pl.BlockSpec((1, MLA_Q_RANK), const),
            pl.BlockSpec((1, MLA_KV_RANK), const),
            pl.BlockSpec((MLA_Q_RANK, hq), const),
            pl.BlockSpec((MLA_KV_RANK, hq), const),
            pl.BlockSpec((MLA_KV_RANK, hv), const),
            pl.BlockSpec((PREP_TM, 3 * LANES), lambda i: (i, 0)),
        ],
        out_specs=[
            pl.BlockSpec((PREP_TM, hq), lambda i: (i, 0)),
            pl.BlockSpec((PREP_TM, hq), lambda i: (i, 0)),
            pl.BlockSpec((PREP_TM, hv), lambda i: (i, 0)),
        ],
        out_shape=[
            jax.ShapeDtypeStruct((t, hq), bf16),
            jax.ShapeDtypeStruct((t, hq), bf16),
            jax.ShapeDtypeStruct((t, hv), bf16),
        ],
        compiler_params=pltpu.CompilerParams(
            dimension_semantics=("arbitrary",), vmem_limit_bytes=VMEM_LIMIT),
        name="mla_prep",
    )(a, gq, gkv, wq, wk, wv, rope)


ATT_TQ = 512
ATT_TK = 256


def _softmax_probs(q, k, s_ref, p_ref, p_row0, bias_fn):
    tq = q.shape[0]
    n_chunks = k.shape[0] // ATT_TK
    n_sub = ATT_TK // LANES
    s = _nt_dot(q, k)
    m_part = jnp.full((tq, LANES), -jnp.inf, f32)
    for c in range(n_chunks):
        cols = slice(c * ATT_TK, (c + 1) * ATT_TK)
        sc = s[:, cols]
        if bias_fn is not None:
            sc = jnp.concatenate([sc[r * ATT_TK:(r + 1) * ATT_TK] + bias_fn(r, c)
                                  for r in range(tq // ATT_TK)], axis=0)
        s_ref[:, cols] = sc
        for u in range(n_sub):
            m_part = jnp.maximum(m_part, sc[:, u * LANES:(u + 1) * LANES])
    m = jnp.max(m_part, axis=-1, keepdims=True)
    l_part = jnp.zeros((tq, LANES), f32)
    for c in range(n_chunks):
        cols = slice(c * ATT_TK, (c + 1) * ATT_TK)
        p = jnp.exp(s_ref[:, cols] - m)
        for u in range(n_sub):
            l_part = l_part + p[:, u * LANES:(u + 1) * LANES]
        p_ref[p_row0:p_row0 + tq, cols] = p.astype(bf16)
    return jnp.sum(l_part, axis=-1, keepdims=True)


def _mla_attn_kernel(q_ref, k_ref, v_ref, o_ref, s_ref, p_ref):
    tq = q_ref.shape[0]
    ls = [_softmax_probs(q_ref[:, hh * LANES:(hh + 1) * LANES], k_ref[:, hh * LANES:(hh + 1) * LANES],
                         s_ref, p_ref, hh * tq, None) for hh in range(2)]
    acc = jnp.dot(p_ref[...], v_ref[...], preferred_element_type=f32)
    lane = lax.broadcasted_iota(jnp.int32, (tq, LANES), 1)
    o_ref[...] = jnp.where(lane < MLA_V, acc[:tq] / ls[0], acc[tq:] / ls[1]).astype(o_ref.dtype)


def _mla_attn(q, k, v, batch, seq):
    t = q.shape[0]
    nq = seq // ATT_TQ
    n_pairs = MLA_HEADS // 2
    return pl.pallas_call(
        _mla_attn_kernel,
        grid=(batch, n_pairs, nq),
        in_specs=[
            pl.BlockSpec((ATT_TQ, 2 * LANES), lambda b, j, i: (b * nq + i, j)),
            pl.BlockSpec((seq, 2 * LANES), lambda b, j, i: (b, j)),
            pl.BlockSpec((seq, LANES), lambda b, j, i: (b, j)),
        ],
        out_specs=pl.BlockSpec((ATT_TQ, LANES), lambda b, j, i: (b * nq + i, j)),
        out_shape=jax.ShapeDtypeStruct((t, MLA_HEADS * MLA_V), bf16),
        scratch_shapes=[pltpu.VMEM((ATT_TQ, seq), f32), pltpu.VMEM((2 * ATT_TQ, seq), bf16)],
        compiler_params=pltpu.CompilerParams(
            dimension_semantics=("arbitrary", "arbitrary", "arbitrary"), vmem_limit_bytes=VMEM_LIMIT),
        name="mla_attn",
    )(q, k, v)


DIFF_SCALE = DIFF_QK ** -0.5
N_BIAS_TILES = 5


def _diff_attn_kernel(lam_ref, q0_ref, q1_ref, k0_ref, k1_ref, v_ref, bias_ref, gsub_ref, o_ref, s_ref, p_ref):
    qi = pl.program_id(2)
    tq = q0_ref.shape[0]
    lv = lam_ref[...]
    lam_init = lv[4:5, 0:1]
    lam = (jnp.exp(jnp.sum(lv[0:1] * lv[1:2], axis=-1, keepdims=True))
           - jnp.exp(jnp.sum(lv[2:3] * lv[3:4], axis=-1, keepdims=True)) + lam_init)
    lane = lax.broadcasted_iota(jnp.int32, (tq, LANES), 1)
    q_refs = (q0_ref, q1_ref)
    k_refs = (k0_ref, k1_ref)
    for hh in range(2):
        head_lanes = (lane >= DIFF_QK) == (hh == 1)
        ls = []
        for m in range(2):
            q = q_refs[m][...]
            q = jnp.where(head_lanes, q * DIFF_SCALE, jnp.zeros_like(q))

            def bias_fn(r, c, m=m, hh=hh):
                return bias_ref[m, hh, jnp.clip(c - (qi * (tq // ATT_TK) + r), -2, 2) + 2]

            ls.append(_softmax_probs(q, k_refs[m][...], s_ref, p_ref, m * tq, bias_fn))
        acc = jnp.dot(p_ref[...], v_ref[:, hh * DIFF_V:(hh + 1) * DIFF_V], preferred_element_type=f32)
        o = acc[:tq] / ls[0] - lam * (acc[tq:] / ls[1])
        o = _rms(o, gsub_ref[...]) * (1.0 - lam_init)
        o_ref[:, hh * DIFF_V:(hh + 1) * DIFF_V] = o.astype(o_ref.dtype)


def _diff_attn(lamvec, qkv, bias_tiles, gsub, batch, seq):
    t = qkv.shape[0]
    nq = seq // ATT_TQ
    n_pairs = DIFF_HEADS // 2
    return pl.pallas_call(
        _diff_attn_kernel,
        grid=(batch, n_pairs, nq),
        in_specs=[
            pl.BlockSpec((8, DIFF_QK), lambda b, j, i: (0, 0)),
            pl.BlockSpec((ATT_TQ, LANES), lambda b, j, i: (b * nq + i, FQ_BLK + j)),
            pl.BlockSpec((ATT_TQ, LANES), lambda b, j, i: (b * nq + i, FQ_BLK + n_pairs + j)),
            pl.BlockSpec((seq, LANES), lambda b, j, i: (b, FK_BLK + j)),
            pl.BlockSpec((seq, LANES), lambda b, j, i: (b, FK_BLK + n_pairs + j)),
            pl.BlockSpec((seq, 2 * DIFF_V), lambda b, j, i: (b, FV_BLK // 2 + j)),
            pl.BlockSpec((2, 2, N_BIAS_TILES, ATT_TK, ATT_TK), lambda b, j, i: (0, j, 0, 0, 0)),
            pl.BlockSpec((1, DIFF_V), lambda b, j, i: (0, 0)),
        ],
        out_specs=pl.BlockSpec((ATT_TQ, 2 * DIFF_V), lambda b, j, i: (b * nq + i, j)),
        out_shape=jax.ShapeDtypeStruct((t, DIFF_HEADS * DIFF_V), bf16),
        scratch_shapes=[pltpu.VMEM((ATT_TQ, seq), f32), pltpu.VMEM((2 * ATT_TQ, seq), bf16)],
        compiler_params=pltpu.CompilerParams(
            dimension_semantics=("arbitrary", "arbitrary", "arbitrary"), vmem_limit_bytes=VMEM_LIMIT),
        name="diff_attn",
    )(lamvec, qkv, qkv, qkv, qkv, qkv, bias_tiles, gsub)


DIL_TQ = 128
DIL_TW = DIL_TQ + 2 * DIL_HALF
DIL_SCALE = DIL_QK ** -0.5


def _dil_attn_kernel(q_ref, k_ref, v_ref, bm_ref, o_ref, lse_ref, *, sub_len, tl):
    tb = pl.program_id(2)
    lane = lax.broadcasted_iota(jnp.int32, (DIL_TQ, LANES), 1)
    col = lax.broadcasted_iota(jnp.int32, (DIL_TQ, DIL_TW), 1)

    def tile(i, carry):
        r_loc = pl.multiple_of(i * DIL_TQ, DIL_TQ)
        t0 = pl.multiple_of(tb * tl + i * DIL_TQ, DIL_TQ)
        lo = pl.multiple_of(jnp.maximum(t0 - DIL_HALF, 0), DIL_HALF)
        hi = pl.multiple_of(jnp.minimum(t0 + DIL_TQ, sub_len - DIL_HALF), DIL_HALF)
        c_lo = jnp.where(t0 == 0, DIL_HALF, 0)
        c_hi = jnp.where(t0 + DIL_TQ >= sub_len, DIL_HALF + DIL_TQ, DIL_TW)
        valid = (col >= c_lo) & (col < c_hi)
        qt = q_ref[pl.ds(r_loc, DIL_TQ), :]
        for h in range(DIL_HEADS):
            c0 = (h // 2) * LANES
            qh = qt[:, c0:c0 + LANES]
            qh = jnp.where((lane >= DIL_QK) == (h % 2 == 1), qh * DIL_SCALE, jnp.zeros_like(qh))
            ksl = slice(c0, c0 + LANES)
            vsl = slice(h * DIL_V, (h + 1) * DIL_V)
            kw = jnp.concatenate([k_ref[pl.ds(lo, DIL_HALF), ksl], k_ref[pl.ds(t0, DIL_TQ), ksl],
                                  k_ref[pl.ds(hi, DIL_HALF), ksl]], axis=0)
            vw = jnp.concatenate([v_ref[pl.ds(lo, DIL_HALF), vsl], v_ref[pl.ds(t0, DIL_TQ), vsl],
                                  v_ref[pl.ds(hi, DIL_HALF), vsl]], axis=0)
            s = _nt_dot(qh, kw) + bm_ref[h]
            s = jnp.where(valid, s, NEG)
            m = jnp.max(s, axis=-1, keepdims=True)
            p = jnp.exp(s - m)
            l = jnp.sum(p, axis=-1, keepdims=True)
            o = jnp.dot(p.astype(bf16), vw, preferred_element_type=f32) / l
            o_ref[pl.ds(r_loc, DIL_TQ), vsl] = o
            lse_ref[pl.ds(r_loc, DIL_TQ), vsl] = jnp.broadcast_to(m + jnp.log(l), (DIL_TQ, DIL_V))
        return carry

    lax.fori_loop(0, tl // DIL_TQ, tile, 0)


def _dil_attn(qkv, biasmask_g, g, dil, batch, seq):
    sub_len = seq // dil
    tl = min(sub_len, 1024)
    qkv3 = qkv.reshape(batch, sub_len, dil * QKV_COLS)
    qk_w = DIL_HEADS * DIL_QK
    v_w = DIL_HEADS * DIL_V
    q_blocks = QKV_COLS // qk_w
    v_blocks = QKV_COLS // v_w
    dk0 = DK_BLK * LANES // qk_w
    dv0 = DV_BLK * LANES // v_w
    out_shape = jax.ShapeDtypeStruct((batch, sub_len, dil * v_w), f32)
    o, lse = pl.pallas_call(
        functools.partial(_dil_attn_kernel, sub_len=sub_len, tl=tl),
        grid=(batch, dil, sub_len // tl),
        in_specs=[
            pl.BlockSpec((None, tl, qk_w), lambda b, r, i: (b, i, r * q_blocks + g)),
            pl.BlockSpec((None, sub_len, qk_w), lambda b, r, i: (b, 0, r * q_blocks + dk0 + g)),
            pl.BlockSpec((None, sub_len, v_w), lambda b, r, i: (b, 0, r * v_blocks + dv0)),
            pl.BlockSpec((DIL_HEADS, DIL_TQ, DIL_TW), lambda b, r, i: (0, 0, 0)),
        ],
        out_specs=[
            pl.BlockSpec((None, tl, v_w), lambda b, r, i: (b, i, r)),
            pl.BlockSpec((None, tl, v_w), lambda b, r, i: (b, i, r)),
        ],
        out_shape=[out_shape, out_shape],
        compiler_params=pltpu.CompilerParams(
            dimension_semantics=("arbitrary", "arbitrary", "arbitrary"), vmem_limit_bytes=VMEM_LIMIT),
        name=f"dil_attn_g{g}",
    )(qkv3, qkv3, qkv3, biasmask_g)
    return o.reshape(batch * seq, v_w), lse.reshape(batch * seq, v_w)


MERGE_TM = 256


def _merge_kernel(x_ref, oa_ref, oc_ref, o0_ref, o1_ref, o2_ref, l0_ref, l1_ref, l2_ref, gate_ref,
                  wb_ref, wo_ref, g_ref, out_ref):
    l0, l1, l2 = l0_ref[...], l1_ref[...], l2_ref[...]
    m = jnp.maximum(jnp.maximum(l0, l1), l2)
    e0, e1, e2 = jnp.exp(l0 - m), jnp.exp(l1 - m), jnp.exp(l2 - m)
    ob = (e0 * o0_ref[...] + e1 * o1_ref[...] + e2 * o2_ref[...]) / (e0 + e1 + e2)
    branches = (oa_ref[...], ob.astype(bf16), oc_ref[...])
    merged = None
    for n in range(N_BRANCH):
        y = jnp.dot(branches[n], wb_ref[n], preferred_element_type=f32)
        y = gate_ref[:, n * D_MODEL:(n + 1) * D_MODEL] * y
        merged = y if merged is None else merged + y
    y = jnp.dot(merged.astype(bf16), wo_ref[...], preferred_element_type=f32)
    out_ref[...] = x_ref[...] + _rms(y, g_ref[...])


def _merge(x, oa, oc, dil_outs, gates, wb, wo, g):
    t = x.shape[0]
    row = lambda i: (i, 0)
    bw = pl.BlockSpec((MERGE_TM, BRANCH_W), row)
    (o0, l0), (o1, l1), (o2, l2) = dil_outs
    return pl.pallas_call(
        _merge_kernel,
        grid=(t // MERGE_TM,),
        in_specs=[
            pl.BlockSpec((MERGE_TM, D_MODEL), row),
            bw, bw, bw, bw, bw, bw, bw, bw,
            pl.BlockSpec((MERGE_TM, GATE_COLS), row),
            pl.BlockSpec((N_BRANCH, BRANCH_W, D_MODEL), lambda i: (0, 0, 0)),
            pl.BlockSpec((D_MODEL, D_MODEL), lambda i: (0, 0)),
            pl.BlockSpec((1, D_MODEL), lambda i: (0, 0)),
        ],
        out_specs=pl.BlockSpec((MERGE_TM, D_MODEL), row),
        out_shape=jax.ShapeDtypeStruct((t, D_MODEL), f32),
        compiler_params=pltpu.CompilerParams(
            dimension_semantics=("arbitrary",), vmem_limit_bytes=VMEM_LIMIT),
        name="merge",
    )(x, oa, oc, o0, o1, o2, l0, l1, l2, gates, wb, wo, g)


MLP_TM = 1024
MLP_TF = 512


def _mlp_kernel(x_ref, gpre_ref, wup_ref, wdn_ref, gpost_ref, out_ref, h_ref, acc_ref):
    k = pl.program_id(1)

    @pl.when(k == 0)
    def _():
        h_ref[...] = _rms(x_ref[...], gpre_ref[...]).astype(bf16)
        acc_ref[...] = jnp.zeros_like(acc_ref)

    u = jnp.maximum(jnp.dot(h_ref[...], wup_ref[...], preferred_element_type=f32), 0.0)
    acc_ref[...] += jnp.dot((u * u).astype(bf16), wdn_ref[...], preferred_element_type=f32)

    @pl.when(k == pl.num_programs(1) - 1)
    def _():
        out_ref[...] = x_ref[...] + _rms(acc_ref[...], gpost_ref[...])


def _mlp(x, gpre, wup, wdn, gpost):
    t = x.shape[0]
    return pl.pallas_call(
        _mlp_kernel,
        grid=(t // MLP_TM, D_FF // MLP_TF),
        in_specs=[
            pl.BlockSpec((MLP_TM, D_MODEL), lambda i, k: (i, 0)),
            pl.BlockSpec((1, D_MODEL), lambda i, k: (0, 0)),
            pl.BlockSpec((D_MODEL, MLP_TF), lambda i, k: (0, k)),
            pl.BlockSpec((MLP_TF, D_MODEL), lambda i, k: (k, 0)),
            pl.BlockSpec((1, D_MODEL), lambda i, k: (0, 0)),
        ],
        out_specs=pl.BlockSpec((MLP_TM, D_MODEL), lambda i, k: (i, 0)),
        out_shape=jax.ShapeDtypeStruct((t, D_MODEL), f32),
        scratch_shapes=[pltpu.VMEM((MLP_TM, D_MODEL), bf16), pltpu.VMEM((MLP_TM, D_MODEL), f32)],
        compiler_params=pltpu.CompilerParams(
            dimension_semantics=("arbitrary", "arbitrary"), vmem_limit_bytes=VMEM_LIMIT),
        name="mlp",
    )(x, gpre, wup, wdn, gpost)


def _rel_bucket(rel):
    nb = REL_BUCKETS // 2
    max_exact = nb // 2
    ret = jnp.where(rel > 0, nb, 0)
    n = jnp.abs(rel)
    large = max_exact + (jnp.log(jnp.maximum(n, 1).astype(f32) / max_exact)
                         / math.log(REL_MAX_DIST / max_exact) * (nb - max_exact)).astype(jnp.int32)
    large = jnp.minimum(large, nb - 1)
    return ret + jnp.where(n < max_exact, n, large)


def _bias_lookup(tab, bucket):
    one_hot = jax.nn.one_hot(bucket, REL_BUCKETS, dtype=f32)
    return jnp.einsum('...b,bc->...c', one_hot, tab.astype(f32), precision=lax.Precision.HIGHEST)


def _pad_in_weights(w_in):
    sizes = (MLA_Q_RANK, MLA_KV_RANK, MLA_ROPE)
    c_q = w_in[..., :sizes[0]]
    c_kv = w_in[..., sizes[0]:sizes[0] + sizes[1]]
    k_r = w_in[..., sizes[0] + sizes[1]:sum(sizes)]
    rest = w_in[..., sum(sizes):]
    z = lambda n: jnp.zeros(w_in.shape[:-1] + (n,), w_in.dtype)
    return jnp.concatenate([c_q, c_kv, z(MLA_NOPE), k_r, z(LANES - MLA_NOPE - MLA_ROPE), rest],
                           axis=-1).astype(bf16)


def _pad_mla_weights(w_uq, w_ukv):
    depth = w_uq.shape[0]
    wq = jnp.pad(w_uq, ((0, 0), (0, 0), (0, 0), (0, LANES - MLA_NOPE - MLA_ROPE)))
    wk = jnp.pad(w_ukv[..., :MLA_NOPE], ((0, 0), (0, 0), (0, 0), (0, LANES - MLA_NOPE)))
    wv = w_ukv[..., MLA_NOPE:]
    return (wq.reshape(depth, MLA_Q_RANK, MLA_HEADS * LANES).astype(bf16),
            wk.reshape(depth, MLA_KV_RANK, MLA_HEADS * LANES).astype(bf16),
            wv.reshape(depth, MLA_KV_RANK, MLA_HEADS * MLA_V).astype(bf16))


def _rope_tables(positions):
    inv = ROPE_BASE ** (-jnp.arange(ROPE_HALF, dtype=f32) / ROPE_HALF)
    ang = positions.reshape(-1).astype(f32)[:, None] * inv
    cos, sin = jnp.cos(ang), jnp.sin(ang)
    t = ang.shape[0]
    one = jnp.ones((t, MLA_NOPE), f32)
    z = lambda n: jnp.zeros((t, n), f32)
    tail = LANES - MLA_NOPE - MLA_ROPE
    c = jnp.concatenate([one, cos, cos, z(tail)], axis=-1)
    sa = jnp.concatenate([z(MLA_NOPE + ROPE_HALF), sin, z(tail)], axis=-1)
    sb = jnp.concatenate([z(MLA_NOPE), -sin, z(ROPE_HALF + tail)], axis=-1)
    return jnp.concatenate([c, sa, sb], axis=-1)


def _diff_bias_tiles(rel_bias):
    tab = rel_bias[:, DIL_BIAS_COLS:]
    i = jnp.arange(ATT_TK)[:, None]
    j = jnp.arange(ATT_TK)[None, :]
    offs = jnp.array([-2, -1, 0, 1, 2])[:, None, None] * ATT_TK
    bucket = _rel_bucket(offs + j - i)
    tiles = jnp.transpose(_bias_lookup(tab, bucket), (3, 0, 1, 2))
    return tiles.reshape(2, DIFF_HEADS, N_BIAS_TILES, ATT_TK, ATT_TK)


def _dil_bias_masks(rel_bias):
    i = jnp.arange(DIL_TQ)[:, None]
    c = jnp.arange(DIL_TW)[None, :]
    rel = c - DIL_HALF - i
    out = []
    for g, (_, dil) in enumerate(DIL_PATTERNS):
        tab = rel_bias[:, g * DIL_HEADS:(g + 1) * DIL_HEADS]
        b = jnp.transpose(_bias_lookup(tab, _rel_bucket(rel * dil)), (2, 0, 1))
        out.append(jnp.where((jnp.abs(rel) <= DIL_HALF)[None], b, NEG))
    return jnp.stack(out, axis=0)


def kernel(x, positions, rel_bias, g_mix_pre, w_in, g_q, w_uq, g_kv, w_ukv, lam_q1, lam_k1, lam_q2, lam_k2,
           g_diff_sub, w_branch, w_out, g_mix_post, g_mlp_pre, w_up, w_down, g_mlp_post):
    batch, seq, d = x.shape
    depth = w_in.shape[0]
    assert d == D_MODEL and seq % 1024 == 0 and seq // DIL_PATTERNS[-1][1] >= 2 * DIL_TQ

    w_in_p = _pad_in_weights(w_in)
    wq_p, wk_p, wv_p = _pad_mla_weights(w_uq, w_ukv)
    wb = w_branch.astype(bf16)
    wo = w_out.astype(bf16)
    wup = w_up.astype(bf16)
    wdn = w_down.astype(bf16)
    rope = _rope_tables(positions)
    diff_bias = _diff_bias_tiles(rel_bias)
    dil_bias = _dil_bias_masks(rel_bias)

    lam_init = jnp.array([0.8 - 0.6 * math.exp(-0.3 * l) for l in range(depth)], f32)
    lam_row = jnp.zeros((depth, 1, DIFF_QK), f32).at[:, 0, 0].set(lam_init)
    lamvec = jnp.concatenate([lam_q1[:, None], lam_k1[:, None], lam_q2[:, None], lam_k2[:, None],
                              lam_row, jnp.zeros((depth, 3, DIFF_QK), f32)], axis=1).astype(f32)

    row = lambda v: v.reshape(1, -1).astype(f32)
    xt = x.reshape(batch * seq, d)
    for l in range(depth):
        a, qkv, gates = _in_proj(xt, row(g_mix_pre[l]), w_in_p[l])
        q, k, v = _mla_prep(a, row(g_q[l]), row(g_kv[l]), wq_p[l], wk_p[l], wv_p[l], rope)
        o_a = _mla_attn(q, k, v, batch, seq)
        dil_outs = [_dil_attn(qkv, dil_bias[g], g, DIL_PATTERNS[g][1], batch, seq)
                    for g in range(DIL_GROUPS)]
        o_c = _diff_attn(lamvec[l], qkv, diff_bias, row(g_diff_sub[l]), batch, seq)
        xt = _merge(xt, o_a, o_c, dil_outs, gates, wb[l], wo[l], row(g_mix_post[l]))
        xt = _mlp(xt, row(g_mlp_pre[l]), wup[l], wdn[l], row(g_mlp_post[l]))
    return xt.reshape(batch, seq, d)
```

```python
import functools
import math

import jax
import jax.numpy as jnp
import numpy as np
from jax import lax
from jax.experimental import pallas as pl
from jax.experimental.pallas import tpu as pltpu

f32 = jnp.float32
bf16 = jnp.bfloat16

D_MODEL = 1024
MLA_HEADS = 8
MLA_Q_RANK = 256
MLA_KV_RANK = 128
MLA_NOPE = 64
MLA_ROPE = 32
MLA_V = 64
ROPE_BASE = 10000.0
DIL_PATTERNS = ((128, 1), (512, 4), (2048, 16))
DIL_GROUPS = 3
DIL_HEADS = 4
DIL_QK = 64
DIL_V = 128
DIL_HALF = 64
DIFF_HEADS = 4
DIFF_QK = 64
DIFF_V = 128
REL_BUCKETS = 32
REL_MAX_DIST = 128
DIL_BIAS_COLS = DIL_GROUPS * DIL_HEADS
N_BRANCH = 3
BRANCH_W = 512
D_FF = 4 * D_MODEL
EPS = 1e-6
NEG = -1e30

LANES = 128
A_COLS = 512
QKV_COLS = 3584
GATE_COLS = N_BRANCH * D_MODEL
IN_PAD_COLS = A_COLS + QKV_COLS + GATE_COLS
DQ_BLK, DK_BLK, DV_BLK, FQ_BLK, FK_BLK, FV_BLK = 0, 6, 12, 16, 20, 24

VMEM_LIMIT = 48 * 1024 * 1024
VMEM_LIMIT_DIL = 56 * 1024 * 1024


def _rms(x, g):
    return x * lax.rsqrt(jnp.mean(x * x, axis=-1, keepdims=True) + EPS) * g


def _nt_dot(a, b):
    return lax.dot_general(a, b, (((1,), (1,)), ((), ())), preferred_element_type=f32)


IN_TM = 1024
IN_TN = 512
_IN_QKV_J0 = A_COLS // IN_TN
_IN_GATE_J0 = (A_COLS + QKV_COLS) // IN_TN


def _in_proj_kernel(x_ref, g_ref, w_ref, a_ref, qkv_ref, gate_ref, h_ref):
    j = pl.program_id(1)

    @pl.when(j == 0)
    def _():
        h_ref[...] = _rms(x_ref[...], g_ref[...]).astype(bf16)

    z = jnp.dot(h_ref[...], w_ref[...], preferred_element_type=f32)

    @pl.when(j == 0)
    def _():
        a_ref[...] = z

    @pl.when((j >= _IN_QKV_J0) & (j < _IN_GATE_J0))
    def _():
        qkv_ref[...] = z.astype(bf16)

    @pl.when(j >= _IN_GATE_J0)
    def _():
        gate_ref[...] = jax.nn.sigmoid(z)


def _in_proj(x, g, w):
    t = x.shape[0]
    n_j = IN_PAD_COLS // IN_TN
    n_qkv = QKV_COLS // IN_TN
    n_gate = GATE_COLS // IN_TN
    return pl.pallas_call(
        _in_proj_kernel,
        grid=(t // IN_TM, n_j),
        in_specs=[
            pl.BlockSpec((IN_TM, D_MODEL), lambda i, j: (i, 0)),
            pl.BlockSpec((1, D_MODEL), lambda i, j: (0, 0)),
            pl.BlockSpec((D_MODEL, IN_TN), lambda i, j: (0, j)),
        ],
        out_specs=[
            pl.BlockSpec((IN_TM, IN_TN), lambda i, j: (i, 0)),
            pl.BlockSpec((IN_TM, IN_TN), lambda i, j: (i, jnp.clip(j - _IN_QKV_J0, 0, n_qkv - 1))),
            pl.BlockSpec((IN_TM, IN_TN), lambda i, j: (i, jnp.clip(j - _IN_GATE_J0, 0, n_gate - 1))),
        ],
        out_shape=[
            jax.ShapeDtypeStruct((t, A_COLS), f32),
            jax.ShapeDtypeStruct((t, QKV_COLS), bf16),
            jax.ShapeDtypeStruct((t, GATE_COLS), f32),
        ],
        scratch_shapes=[pltpu.VMEM((IN_TM, D_MODEL), bf16)],
        compiler_params=pltpu.CompilerParams(
            dimension_semantics=("arbitrary", "arbitrary"), vmem_limit_bytes=VMEM_LIMIT),
        name="in_proj",
    )(x, g, w)


PREP_TM = 512
MLA_SCALE = (MLA_NOPE + MLA_ROPE) ** -0.5
ROPE_HALF = MLA_ROPE // 2


def _rope_lanes(x, rope):
    c = rope[:, 0:LANES]
    sa = rope[:, LANES:2 * LANES]
    sb = rope[:, 2 * LANES:3 * LANES]
    return (x * c + pltpu.roll(x, ROPE_HALF, 1) * sa + pltpu.roll(x, LANES - ROPE_HALF, 1) * sb)


def _mla_prep_kernel(a_ref, gq_ref, gkv_ref, wq_ref, wk_ref, wv_ref, rope_ref, q_ref, k_ref, v_ref):
    a = a_ref[...]
    rope = rope_ref[...]
    cq = _rms(a[:, :MLA_Q_RANK], gq_ref[...]).astype(bf16)
    ckv = _rms(a[:, MLA_Q_RANK:MLA_Q_RANK + MLA_KV_RANK], gkv_ref[...]).astype(bf16)
    k_rope = _rope_lanes(a[:, A_COLS - LANES:], rope)
    qf = jnp.dot(cq, wq_ref[...], preferred_element_type=f32)
    kf = jnp.dot(ckv, wk_ref[...], preferred_element_type=f32)
    v_ref[...] = jnp.dot(ckv, wv_ref[...], preferred_element_type=f32).astype(bf16)
    for h in range(MLA_HEADS):
        sl = slice(h * LANES, (h + 1) * LANES)
        q_ref[:, sl] = (_rope_lanes(qf[:, sl], rope) * MLA_SCALE).astype(bf16)
        k_ref[:, sl] = (kf[:, sl] + k_rope).astype(bf16)


def _mla_prep(a, gq, gkv, wq, wk, wv, rope):
    t = a.shape[0]
    hq = MLA_HEADS * LANES
    hv = MLA_HEADS * MLA_V
    const = lambda i: (0, 0)
    return pl.pallas_call(
        _mla_prep_kernel,
        grid=(t // PREP_TM,),
        in_specs=[
            pl.BlockSpec((PREP_TM, A_COLS), lambda i: (i, 0)),
            pl.BlockSpec((1, MLA_Q_RANK), const),
            pl.BlockSpec((1, MLA_KV_RANK), const),
            pl.BlockSpec((MLA_Q_RANK, hq), const),
            pl.BlockSpec((MLA_KV_RANK, hq), const),
            pl.BlockSpec((MLA_KV_RANK, hv), const),
            pl.BlockSpec((PREP_TM, 3 * LANES), lambda i: (i, 0)),
        ],
        out_specs=[
            pl.BlockSpec((PREP_TM, hq), lambda i: (i, 0)),
            pl.BlockSpec((PREP_TM, hq), lambda i: (i, 0)),
            pl.BlockSpec((PREP_TM, hv), lambda i: (i, 0)),
        ],
        out_shape=[
            jax.ShapeDtypeStruct((t, hq), bf16),
            jax.ShapeDtypeStruct((t, hq), bf16),
            jax.ShapeDtypeStruct((t, hv), bf16),
        ],
        compiler_params=pltpu.CompilerParams(
            dimension_semantics=("arbitrary",), vmem_limit_bytes=VMEM_LIMIT),
        name="mla_prep",
    )(a, gq, gkv, wq, wk, wv, rope)


ATT_TQ = 512
ATT_TK = 256


def _softmax_probs(q, k, s_ref, p_ref, p_row0, bias_fn):
    tq = q.shape[0]
    n_chunks = k.shape[0] // ATT_TK
    n_sub = ATT_TK // LANES
    s = _nt_dot(q, k)
    m_part = jnp.full((tq, LANES), -jnp.inf, f32)
    for c in range(n_chunks):
        cols = slice(c * ATT_TK, (c + 1) * ATT_TK)
        sc = s[:, cols]
        if bias_fn is not None:
            sc = jnp.concatenate([sc[r * ATT_TK:(r + 1) * ATT_TK] + bias_fn(r, c)
                                  for r in range(tq // ATT_TK)], axis=0)
        s_ref[:, cols] = sc
        for u in range(n_sub):
            m_part = jnp.maximum(m_part, sc[:, u * LANES:(u + 1) * LANES])
    m = jnp.max(m_part, axis=-1, keepdims=True)
    l_part = jnp.zeros((tq, LANES), f32)
    for c in range(n_chunks):
        cols = slice(c * ATT_TK, (c + 1) * ATT_TK)
        p = jnp.exp(s_ref[:, cols] - m)
        for u in range(n_sub):
            l_part = l_part + p[:, u * LANES:(u + 1) * LANES]
        p_ref[p_row0:p_row0 + tq, cols] = p.astype(bf16)
    return jnp.sum(l_part, axis=-1, keepdims=True)


def _mla_attn_kernel(q_ref, k_ref, v_ref, o_ref, s_ref, p_ref):
    tq = q_ref.shape[0]
    ls = [_softmax_probs(q_ref[:, hh * LANES:(hh + 1) * LANES], k_ref[:, hh * LANES:(hh + 1) * LANES],
                         s_ref, p_ref, hh * tq, None) for hh in range(2)]
    acc = jnp.dot(p_ref[...], v_ref[...], preferred_element_type=f32)
    lane = lax.broadcasted_iota(jnp.int32, (tq, LANES), 1)
    o_ref[...] = jnp.where(lane < MLA_V, acc[:tq] / ls[0], acc[tq:] / ls[1]).astype(o_ref.dtype)


def _mla_attn(q, k, v, batch, seq):
    t = q.shape[0]
    nq = seq // ATT_TQ
    n_pairs = MLA_HEADS // 2
    return pl.pallas_call(
        _mla_attn_kernel,
        grid=(batch, n_pairs, nq),
        in_specs=[
            pl.BlockSpec((ATT_TQ, 2 * LANES), lambda b, j, i: (b * nq + i, j)),
            pl.BlockSpec((seq, 2 * LANES), lambda b, j, i: (b, j)),
            pl.BlockSpec((seq, LANES), lambda b, j, i: (b, j)),
        ],
        out_specs=pl.BlockSpec((ATT_TQ, LANES), lambda b, j, i: (b * nq + i, j)),
        out_shape=jax.ShapeDtypeStruct((t, MLA_HEADS * MLA_V), bf16),
        scratch_shapes=[pltpu.VMEM((ATT_TQ, seq), f32), pltpu.VMEM((2 * ATT_TQ, seq), bf16)],
        compiler_params=pltpu.CompilerParams(
            dimension_semantics=("arbitrary", "arbitrary", "arbitrary"), vmem_limit_bytes=VMEM_LIMIT),
        name="mla_attn",
    )(q, k, v)


DIFF_TQ = 256
DIFF_SCALE = DIFF_QK ** -0.5
N_BIAS_TILES = 5


def _diff_attn_kernel(lam_ref, q0_ref, q1_ref, k0_ref, k1_ref, v_ref, bias_ref, gsub_ref, o_ref, s_ref, p_ref):
    qi = pl.program_id(2)
    tq = q0_ref.shape[0]
    lv = lam_ref[...]
    lam_init = lv[4:5, 0:1]
    lam = (jnp.exp(jnp.sum(lv[0:1] * lv[1:2], axis=-1, keepdims=True))
           - jnp.exp(jnp.sum(lv[2:3] * lv[3:4], axis=-1, keepdims=True)) + lam_init)
    lane = lax.broadcasted_iota(jnp.int32, (tq, LANES), 1)
    q_refs = (q0_ref, q1_ref)
    k_refs = (k0_ref, k1_ref)
    for hh in range(2):
        head_lanes = (lane >= DIFF_QK) == (hh == 1)
        ls = []
        for m in range(2):
            q = q_refs[m][...]
            q = jnp.where(head_lanes, q * DIFF_SCALE, jnp.zeros_like(q))

            def bias_fn(r, c, m=m, hh=hh):
                return bias_ref[m, hh, jnp.clip(c - (qi * (tq // ATT_TK) + r), -2, 2) + 2]

            ls.append(_softmax_probs(q, k_refs[m][...], s_ref, p_ref, m * tq, bias_fn))
        acc = jnp.dot(p_ref[...], v_ref[:, hh * DIFF_V:(hh + 1) * DIFF_V], preferred_element_type=f32)
        o = acc[:tq] / ls[0] - lam * (acc[tq:] / ls[1])
        o = _rms(o, gsub_ref[...]) * (1.0 - lam_init)
        o_ref[:, hh * DIFF_V:(hh + 1) * DIFF_V] = o.astype(o_ref.dtype)


def _diff_attn(lamvec, qkv, bias_tiles, gsub, batch, seq):
    t = qkv.shape[0]
    nq = seq // DIFF_TQ
    n_pairs = DIFF_HEADS // 2
    return pl.pallas_call(
        _diff_attn_kernel,
        grid=(batch, n_pairs, nq),
        in_specs=[
            pl.BlockSpec((8, DIFF_QK), lambda b, j, i: (0, 0)),
            pl.BlockSpec((DIFF_TQ, LANES), lambda b, j, i: (b * nq + i, FQ_BLK + j)),
            pl.BlockSpec((DIFF_TQ, LANES), lambda b, j, i: (b * nq + i, FQ_BLK + n_pairs + j)),
            pl.BlockSpec((seq, LANES), lambda b, j, i: (b, FK_BLK + j)),
            pl.BlockSpec((seq, LANES), lambda b, j, i: (b, FK_BLK + n_pairs + j)),
            pl.BlockSpec((seq, 2 * DIFF_V), lambda b, j, i: (b, FV_BLK // 2 + j)),
            pl.BlockSpec((2, 2, N_BIAS_TILES, ATT_TK, ATT_TK), lambda b, j, i: (0, j, 0, 0, 0)),
            pl.BlockSpec((1, DIFF_V), lambda b, j, i: (0, 0)),
        ],
        out_specs=pl.BlockSpec((DIFF_TQ, 2 * DIFF_V), lambda b, j, i: (b * nq + i, j)),
        out_shape=jax.ShapeDtypeStruct((t, DIFF_HEADS * DIFF_V), bf16),
        scratch_shapes=[pltpu.VMEM((DIFF_TQ, seq), f32), pltpu.VMEM((2 * DIFF_TQ, seq), bf16)],
        compiler_params=pltpu.CompilerParams(
            dimension_semantics=("arbitrary", "arbitrary", "arbitrary"), vmem_limit_bytes=VMEM_LIMIT),
        name="diff_attn",
    )(lamvec, qkv, qkv, qkv, qkv, qkv, bias_tiles, gsub)


DIL_TQ = 128
DIL_TW = DIL_TQ + 2 * DIL_HALF
DIL_SCALE = DIL_QK ** -0.5


def _residue_major(src_ref, dst_ref, stage_ref, dil):
    seq, width = src_ref.shape
    sub_len = seq // dil
    for c in range(width // LANES):
        csl = slice(c * LANES, (c + 1) * LANES)
        stage_ref[c] = src_ref[:, csl].astype(f32)
        for r in range(dil):
            dst_ref[r * sub_len:(r + 1) * sub_len, csl] = (
                stage_ref[c, pl.ds(r, sub_len, stride=dil), :].astype(dst_ref.dtype))


def _dil_attn_kernel(q0_ref, q1_ref, q2_ref, k0_ref, k1_ref, k2_ref, v_ref, bm_ref, o_ref,
                     stage_ref, qp_ref, kp_ref, vp_ref, m_ref, l_ref, acc_ref):
    seq = v_ref.shape[0]
    n_tiles = seq // DIL_TQ
    q_refs = (q0_ref, q1_ref, q2_ref)
    k_refs = (k0_ref, k1_ref, k2_ref)
    lane = lax.broadcasted_iota(jnp.int32, (DIL_TQ, LANES), 1)
    col = lax.broadcasted_iota(jnp.int32, (DIL_TQ, DIL_TW), 1)
    for g, (_, dil) in enumerate(DIL_PATTERNS):
        if dil == 1:
            q_src, k_src, v_src = q_refs[g], k_refs[g], v_ref
        else:
            _residue_major(q_refs[g], qp_ref, stage_ref, dil)
            _residue_major(k_refs[g], kp_ref, stage_ref, dil)
            _residue_major(v_ref, vp_ref, stage_ref, dil)
            q_src, k_src, v_src = qp_ref, kp_ref, vp_ref
        tiles_per_residue = n_tiles // dil

        def tile(n, carry, g=g, dil=dil, q_src=q_src, k_src=k_src, v_src=v_src,
                 tiles_per_residue=tiles_per_residue):
            base = pl.multiple_of(n * DIL_TQ, DIL_TQ)
            t_in = n % tiles_per_residue
            lo = pl.multiple_of(jnp.maximum(base - DIL_HALF, 0), DIL_HALF)
            hi = pl.multiple_of(jnp.minimum(base + DIL_TQ, seq - DIL_HALF), DIL_HALF)
            c_lo = jnp.where(t_in == 0, DIL_HALF, 0)
            c_hi = jnp.where(t_in == tiles_per_residue - 1, DIL_HALF + DIL_TQ, DIL_TW)
            valid = (col >= c_lo) & (col < c_hi)
            if dil == 1:
                rows = pl.ds(base, DIL_TQ)
            else:
                token0 = t_in * (DIL_TQ * dil) + n // tiles_per_residue
                rows = pl.ds(token0, DIL_TQ, stride=dil)
            qt = q_src[pl.ds(base, DIL_TQ), :]
            kw = jnp.concatenate([k_src[pl.ds(lo, DIL_HALF), :], k_src[pl.ds(base, DIL_TQ), :],
                                  k_src[pl.ds(hi, DIL_HALF), :]], axis=0)
            vw = jnp.concatenate([v_src[pl.ds(lo, DIL_HALF), :], v_src[pl.ds(base, DIL_TQ), :],
                                  v_src[pl.ds(hi, DIL_HALF), :]], axis=0)
            for hh in range(2):
                qh = jnp.where((lane >= DIL_QK) == (hh == 1), qt * DIL_SCALE, jnp.zeros_like(qt))
                hsl = slice(hh * DIL_V, (hh + 1) * DIL_V)
                s = _nt_dot(qh, kw) + bm_ref[g, hh]
                s = jnp.where(valid, s, NEG)
                m_t = jnp.max(s, axis=-1, keepdims=True)
                p = jnp.exp(s - m_t)
                l_t = jnp.broadcast_to(jnp.sum(p, axis=-1, keepdims=True), (DIL_TQ, DIL_V))
                u_t = jnp.dot(p.astype(bf16), vw[:, hsl], preferred_element_type=f32)
                m_t = jnp.broadcast_to(m_t, (DIL_TQ, DIL_V))
                if g == 0:
                    m_ref[hh, rows, :] = m_t
                    l_ref[hh, rows, :] = l_t
                    acc_ref[hh, rows, :] = u_t
                else:
                    m_old = m_ref[hh, rows, :]
                    m_new = jnp.maximum(m_old, m_t)
                    a_old = jnp.exp(m_old - m_new)
                    a_t = jnp.exp(m_t - m_new)
                    m_ref[hh, rows, :] = m_new
                    l_ref[hh, rows, :] = a_old * l_ref[hh, rows, :] + a_t * l_t
                    acc_ref[hh, rows, :] = a_old * acc_ref[hh, rows, :] + a_t * u_t
            return carry

        lax.fori_loop(0, n_tiles, tile, 0)
    for hh in range(2):
        o_ref[:, hh * DIL_V:(hh + 1) * DIL_V] = (acc_ref[hh] / l_ref[hh]).astype(o_ref.dtype)


def _dil_attn(qkv, biasmask, batch, seq):
    t = qkv.shape[0]
    n_pairs = DIL_HEADS // 2
    pair_w = 2 * DIL_V

    def qk_spec(blk0, g):
        return pl.BlockSpec((seq, LANES), lambda b, j: (b, blk0 + g * n_pairs + j))

    state = pltpu.VMEM((2, seq, DIL_V), f32)
    return pl.pallas_call(
        _dil_attn_kernel,
        grid=(batch, n_pairs),
        in_specs=[qk_spec(DQ_BLK, g) for g in range(DIL_GROUPS)]
        + [qk_spec(DK_BLK, g) for g in range(DIL_GROUPS)]
        + [pl.BlockSpec((seq, pair_w), lambda b, j: (b, DV_BLK // 2 + j)),
           pl.BlockSpec((DIL_GROUPS, 2, DIL_TQ, DIL_TW), lambda b, j: (0, j, 0, 0))],
        out_specs=pl.BlockSpec((seq, pair_w), lambda b, j: (b, j)),
        out_shape=jax.ShapeDtypeStruct((t, DIL_HEADS * DIL_V), bf16),
        scratch_shapes=[
            pltpu.VMEM((2, seq, LANES), f32),
            pltpu.VMEM((seq, LANES), bf16),
            pltpu.VMEM((seq, LANES), bf16),
            pltpu.VMEM((seq, pair_w), bf16),
            state, state, state,
        ],
        compiler_params=pltpu.CompilerParams(
            dimension_semantics=("arbitrary", "arbitrary"), vmem_limit_bytes=VMEM_LIMIT_DIL),
        name="dil_attn",
    )(qkv, qkv, qkv, qkv, qkv, qkv, qkv, biasmask)


MERGE_TM = 256


def _merge_kernel(x_ref, oa_ref, ob_ref, oc_ref, gate_ref, wb_ref, wo_ref, g_ref, out_ref):
    branches = (oa_ref[...], ob_ref[...], oc_ref[...])
    merged = None
    for n in range(N_BRANCH):
        y = jnp.dot(branches[n], wb_ref[n], preferred_element_type=f32)
        y = gate_ref[:, n * D_MODEL:(n + 1) * D_MODEL] * y
        merged = y if merged is None else merged + y
    y = jnp.dot(merged.astype(bf16), wo_ref[...], preferred_element_type=f32)
    out_ref[...] = x_ref[...] + _rms(y, g_ref[...])


def _merge(x, oa, ob, oc, gates, wb, wo, g):
    t = x.shape[0]
    row = lambda i: (i, 0)
    bw = pl.BlockSpec((MERGE_TM, BRANCH_W), row)
    return pl.pallas_call(
        _merge_kernel,
        grid=(t // MERGE_TM,),
        in_specs=[
            pl.BlockSpec((MERGE_TM, D_MODEL), row),
            bw, bw, bw,
            pl.BlockSpec((MERGE_TM, GATE_COLS), row),
            pl.BlockSpec((N_BRANCH, BRANCH_W, D_MODEL), lambda i: (0, 0, 0)),
            pl.BlockSpec((D_MODEL, D_MODEL), lambda i: (0, 0)),
            pl.BlockSpec((1, D_MODEL), lambda i: (0, 0)),
        ],
        out_specs=pl.BlockSpec((MERGE_TM, D_MODEL), row),
        out_shape=jax.ShapeDtypeStruct((t, D_MODEL), f32),
        compiler_params=pltpu.CompilerParams(
            dimension_semantics=("arbitrary",), vmem_limit_bytes=VMEM_LIMIT),
        name="merge",
    )(x, oa, ob, oc, gates, wb, wo, g)


MLP_TM = 1024
MLP_TF = 512


def _mlp_kernel(x_ref, gpre_ref, wup_ref, wdn_ref, gpost_ref, out_ref, h_ref, acc_ref):
    k = pl.program_id(1)

    @pl.when(k == 0)
    def _():
        h_ref[...] = _rms(x_ref[...], gpre_ref[...]).astype(bf16)
        acc_ref[...] = jnp.zeros_like(acc_ref)

    u = jnp.maximum(jnp.dot(h_ref[...], wup_ref[...], preferred_element_type=f32), 0.0)
    acc_ref[...] += jnp.dot((u * u).astype(bf16), wdn_ref[...], preferred_element_type=f32)

    @pl.when(k == pl.num_programs(1) - 1)
    def _():
        out_ref[...] = x_ref[...] + _rms(acc_ref[...], gpost_ref[...])


def _mlp(x, gpre, wup, wdn, gpost):
    t = x.shape[0]
    return pl.pallas_call(
        _mlp_kernel,
        grid=(t // MLP_TM, D_FF // MLP_TF),
        in_specs=[
            pl.BlockSpec((MLP_TM, D_MODEL), lambda i, k: (i, 0)),
            pl.BlockSpec((1, D_MODEL), lambda i, k: (0, 0)),
            pl.BlockSpec((D_MODEL, MLP_TF), lambda i, k: (0, k)),
            pl.BlockSpec((MLP_TF, D_MODEL), lambda i, k: (k, 0)),
            pl.BlockSpec((1, D_MODEL), lambda i, k: (0, 0)),
        ],
        out_specs=pl.BlockSpec((MLP_TM, D_MODEL), lambda i, k: (i, 0)),
        out_shape=jax.ShapeDtypeStruct((t, D_MODEL), f32),
        scratch_shapes=[pltpu.VMEM((MLP_TM, D_MODEL), bf16), pltpu.VMEM((MLP_TM, D_MODEL), f32)],
        compiler_params=pltpu.CompilerParams(
            dimension_semantics=("arbitrary", "arbitrary"), vmem_limit_bytes=VMEM_LIMIT),
        name="mlp",
    )(x, gpre, wup, wdn, gpost)


def _rel_bucket(rel):
    nb = REL_BUCKETS // 2
    max_exact = nb // 2
    ret = jnp.where(rel > 0, nb, 0)
    n = jnp.abs(rel)
    large = max_exact + (jnp.log(jnp.maximum(n, 1).astype(f32) / max_exact)
                         / math.log(REL_MAX_DIST / max_exact) * (nb - max_exact)).astype(jnp.int32)
    large = jnp.minimum(large, nb - 1)
    return ret + jnp.where(n < max_exact, n, large)


def _bias_lookup(tab, bucket):
    one_hot = jax.nn.one_hot(bucket, REL_BUCKETS, dtype=f32)
    return jnp.einsum('...b,bc->...c', one_hot, tab.astype(f32), precision=lax.Precision.HIGHEST)


def _pad_in_weights(w_in):
    sizes = (MLA_Q_RANK, MLA_KV_RANK, MLA_ROPE)
    c_q = w_in[..., :sizes[0]]
    c_kv = w_in[..., sizes[0]:sizes[0] + sizes[1]]
    k_r = w_in[..., sizes[0] + sizes[1]:sum(sizes)]
    rest = w_in[..., sum(sizes):]
    z = lambda n: jnp.zeros(w_in.shape[:-1] + (n,), w_in.dtype)
    return jnp.concatenate([c_q, c_kv, z(MLA_NOPE), k_r, z(LANES - MLA_NOPE - MLA_ROPE), rest],
                           axis=-1).astype(bf16)


def _pad_mla_weights(w_uq, w_ukv):
    depth = w_uq.shape[0]
    wq = jnp.pad(w_uq, ((0, 0), (0, 0), (0, 0), (0, LANES - MLA_NOPE - MLA_ROPE)))
    wk = jnp.pad(w_ukv[..., :MLA_NOPE], ((0, 0), (0, 0), (0, 0), (0, LANES - MLA_NOPE)))
    wv = w_ukv[..., MLA_NOPE:]
    return (wq.reshape(depth, MLA_Q_RANK, MLA_HEADS * LANES).astype(bf16),
            wk.reshape(depth, MLA_KV_RANK, MLA_HEADS * LANES).astype(bf16),
            wv.reshape(depth, MLA_KV_RANK, MLA_HEADS * MLA_V).astype(bf16))


def _rope_tables(positions):
    inv = ROPE_BASE ** (-jnp.arange(ROPE_HALF, dtype=f32) / ROPE_HALF)
    ang = positions.reshape(-1).astype(f32)[:, None] * inv
    cos, sin = jnp.cos(ang), jnp.sin(ang)
    t = ang.shape[0]
    one = jnp.ones((t, MLA_NOPE), f32)
    z = lambda n: jnp.zeros((t, n), f32)
    tail = LANES - MLA_NOPE - MLA_ROPE
    c = jnp.concatenate([one, cos, cos, z(tail)], axis=-1)
    sa = jnp.concatenate([z(MLA_NOPE + ROPE_HALF), sin, z(tail)], axis=-1)
    sb = jnp.concatenate([z(MLA_NOPE), -sin, z(ROPE_HALF + tail)], axis=-1)
    return jnp.concatenate([c, sa, sb], axis=-1)


def _diff_bias_tiles(rel_bias):
    tab = rel_bias[:, DIL_BIAS_COLS:]
    i = jnp.arange(ATT_TK)[:, None]
    j = jnp.arange(ATT_TK)[None, :]
    offs = jnp.array([-2, -1, 0, 1, 2])[:, None, None] * ATT_TK
    bucket = _rel_bucket(offs + j - i)
    tiles = jnp.transpose(_bias_lookup(tab, bucket), (3, 0, 1, 2))
    return tiles.reshape(2, DIFF_HEADS, N_BIAS_TILES, ATT_TK, ATT_TK)


def _dil_bias_masks(rel_bias):
    i = jnp.arange(DIL_TQ)[:, None]
    c = jnp.arange(DIL_TW)[None, :]
    rel = c - DIL_HALF - i
    out = []
    for g, (_, dil) in enumerate(DIL_PATTERNS):
        tab = rel_bias[:, g * DIL_HEADS:(g + 1) * DIL_HEADS]
        b = jnp.transpose(_bias_lookup(tab, _rel_bucket(rel * dil)), (2, 0, 1))
        out.append(jnp.where((jnp.abs(rel) <= DIL_HALF)[None], b, NEG))
    return jnp.stack(out, axis=0)


def kernel(x, positions, rel_bias, g_mix_pre, w_in, g_q, w_uq, g_kv, w_ukv, lam_q1, lam_k1, lam_q2, lam_k2,
           g_diff_sub, w_branch, w_out, g_mix_post, g_mlp_pre, w_up, w_down, g_mlp_post):
    batch, seq, d = x.shape
    depth = w_in.shape[0]
    assert d == D_MODEL and seq % 1024 == 0 and seq // DIL_PATTERNS[-1][1] >= 2 * DIL_TQ

    w_in_p = _pad_in_weights(w_in)
    wq_p, wk_p, wv_p = _pad_mla_weights(w_uq, w_ukv)
    wb = w_branch.astype(bf16)
    wo = w_out.astype(bf16)
    wup = w_up.astype(bf16)
    wdn = w_down.astype(bf16)
    rope = _rope_tables(positions)
    diff_bias = _diff_bias_tiles(rel_bias)
    dil_bias = _dil_bias_masks(rel_bias)

    lam_init = jnp.array([0.8 - 0.6 * math.exp(-0.3 * l) for l in range(depth)], f32)
    lam_row = jnp.zeros((depth, 1, DIFF_QK), f32).at[:, 0, 0].set(lam_init)
    lamvec = jnp.concatenate([lam_q1[:, None], lam_k1[:, None], lam_q2[:, None], lam_k2[:, None],
                              lam_row, jnp.zeros((depth, 3, DIFF_QK), f32)], axis=1).astype(f32)

    row = lambda v: v.reshape(1, -1).astype(f32)
    xt = x.reshape(batch * seq, d)
    for l in range(depth):
        a, qkv, gates = _in_proj(xt, row(g_mix_pre[l]), w_in_p[l])
        q, k, v = _mla_prep(a, row(g_q[l]), row(g_kv[l]), wq_p[l], wk_p[l], wv_p[l], rope)
        o_a = _mla_attn(q, k, v, batch, seq)
        o_b = _dil_attn(qkv, dil_bias, batch, seq)
        o_c = _diff_attn(lamvec[l], qkv, diff_bias, row(g_diff_sub[l]), batch, seq)
        xt = _merge(xt, o_a, o_b, o_c, gates, wb[l], wo[l], row(g_mix_post[l]))
        xt = _mlp(xt, row(g_mlp_pre[l]), wup[l], wdn[l], row(g_mlp_post[l]))
    return xt.reshape(batch, seq, d)
```

```python
import functools
import math

import jax
import jax.numpy as jnp
import numpy as np
from jax import lax
from jax.experimental import pallas as pl
from jax.experimental.pallas import tpu as pltpu

f32 = jnp.float32
bf16 = jnp.bfloat16

D_MODEL = 1024
MLA_HEADS = 8
MLA_Q_RANK = 256
MLA_KV_RANK = 128
MLA_NOPE = 64
MLA_ROPE = 32
MLA_V = 64
ROPE_BASE = 10000.0
DIL_PATTERNS = ((128, 1), (512, 4), (2048, 16))
DIL_GROUPS = 3
DIL_HEADS = 4
DIL_QK = 64
DIL_V = 128
DIL_HALF = 64
DIFF_HEADS = 4
DIFF_QK = 64
DIFF_V = 128
REL_BUCKETS = 32
REL_MAX_DIST = 128
DIL_BIAS_COLS = DIL_GROUPS * DIL_HEADS
N_BRANCH = 3
BRANCH_W = 512
D_FF = 4 * D_MODEL
EPS = 1e-6
NEG = -1e30

LANES = 128
A_COLS = 512
QKV_COLS = 3584
GATE_COLS = N_BRANCH * D_MODEL
IN_PAD_COLS = A_COLS + QKV_COLS + GATE_COLS
DQ_BLK, DK_BLK, DV_BLK, FQ_BLK, FK_BLK, FV_BLK = 0, 6, 12, 16, 20, 24

VMEM_LIMIT = 48 * 1024 * 1024
VMEM_LIMIT_DIL = 56 * 1024 * 1024


def _rms(x, g):
    return x * lax.rsqrt(jnp.mean(x * x, axis=-1, keepdims=True) + EPS) * g


def _nt_dot(a, b):
    return lax.dot_general(a, b, (((1,), (1,)), ((), ())), preferred_element_type=f32)


IN_TM = 1024
IN_TN = 512
_IN_QKV_J0 = A_COLS // IN_TN
_IN_GATE_J0 = (A_COLS + QKV_COLS) // IN_TN


def _in_proj_kernel(x_ref, g_ref, w_ref, a_ref, qkv_ref, gate_ref, h_ref):
    j = pl.program_id(1)

    @pl.when(j == 0)
    def _():
        h_ref[...] = _rms(x_ref[...], g_ref[...]).astype(bf16)

    z = jnp.dot(h_ref[...], w_ref[...], preferred_element_type=f32)

    @pl.when(j == 0)
    def _():
        a_ref[...] = z

    @pl.when((j >= _IN_QKV_J0) & (j < _IN_GATE_J0))
    def _():
        qkv_ref[...] = z.astype(bf16)

    @pl.when(j >= _IN_GATE_J0)
    def _():
        gate_ref[...] = jax.nn.sigmoid(z)


def _in_proj(x, g, w):
    t = x.shape[0]
    n_j = IN_PAD_COLS // IN_TN
    n_qkv = QKV_COLS // IN_TN
    n_gate = GATE_COLS // IN_TN
    return pl.pallas_call(
        _in_proj_kernel,
        grid=(t // IN_TM, n_j),
        in_specs=[
            pl.BlockSpec((IN_TM, D_MODEL), lambda i, j: (i, 0)),
            pl.BlockSpec((1, D_MODEL), lambda i, j: (0, 0)),
            pl.BlockSpec((D_MODEL, IN_TN), lambda i, j: (0, j)),
        ],
        out_specs=[
            pl.BlockSpec((IN_TM, IN_TN), lambda i, j: (i, 0)),
            pl.BlockSpec((IN_TM, IN_TN), lambda i, j: (i, jnp.clip(j - _IN_QKV_J0, 0, n_qkv - 1))),
            pl.BlockSpec((IN_TM, IN_TN), lambda i, j: (i, jnp.clip(j - _IN_GATE_J0, 0, n_gate - 1))),
        ],
        out_shape=[
            jax.ShapeDtypeStruct((t, A_COLS), f32),
            jax.ShapeDtypeStruct((t, QKV_COLS), bf16),
            jax.ShapeDtypeStruct((t, GATE_COLS), f32),
        ],
        scratch_shapes=[pltpu.VMEM((IN_TM, D_MODEL), bf16)],
        compiler_params=pltpu.CompilerParams(
            dimension_semantics=("arbitrary", "arbitrary"), vmem_limit_bytes=VMEM_LIMIT),
        name="in_proj",
    )(x, g, w)


PREP_TM = 512
MLA_SCALE = (MLA_NOPE + MLA_ROPE) ** -0.5
LOG2E = math.log2(math.e)
ROPE_HALF = MLA_ROPE // 2


def _rope_lanes(x, rope):
    c = rope[:, 0:LANES]
    sa = rope[:, LANES:2 * LANES]
    sb = rope[:, 2 * LANES:3 * LANES]
    return (x * c + pltpu.roll(x, ROPE_HALF, 1) * sa + pltpu.roll(x, LANES - ROPE_HALF, 1) * sb)


def _mla_prep_kernel(a_ref, gq_ref, gkv_ref, wq_ref, wk_ref, wv_ref, rope_ref, q_ref, k_ref, v_ref):
    a = a_ref[...]
    rope = rope_ref[...]
    cq = _rms(a[:, :MLA_Q_RANK], gq_ref[...]).astype(bf16)
    ckv = _rms(a[:, MLA_Q_RANK:MLA_Q_RANK + MLA_KV_RANK], gkv_ref[...]).astype(bf16)
    k_rope = _rope_lanes(a[:, A_COLS - LANES:], rope)
    qf = jnp.dot(cq, wq_ref[...], preferred_element_type=f32)
    kf = jnp.dot(ckv, wk_ref[...], preferred_element_type=f32)
    vf = jnp.dot(ckv, wv_ref[...], preferred_element_type=f32).astype(bf16)
    ones = jnp.ones((vf.shape[0], LANES), bf16)
    for j in range(MLA_HEADS // 2):
        v_ref[:, 2 * j * LANES:(2 * j + 1) * LANES] = vf[:, j * LANES:(j + 1) * LANES]
        v_ref[:, (2 * j + 1) * LANES:(2 * j + 2) * LANES] = ones
    for h in range(MLA_HEADS):
        sl = slice(h * LANES, (h + 1) * LANES)
        q_ref[:, sl] = (_rope_lanes(qf[:, sl], rope) * (MLA_SCALE * LOG2E)).astype(bf16)
        k_ref[:, sl] = (kf[:, sl] + k_rope).astype(bf16)


def _mla_prep(a, gq, gkv, wq, wk, wv, rope):
    t = a.shape[0]
    hq = MLA_HEADS * LANES
    hv = MLA_HEADS * MLA_V
    const = lambda i: (0, 0)
    return pl.pallas_call(
        _mla_prep_kernel,
        grid=(t // PREP_TM,),
        in_specs=[
            pl.BlockSpec((PREP_TM, A_COLS), lambda i: (i, 0)),
            pl.BlockSpec((1, MLA_Q_RANK), const),
            pl.BlockSpec((1, MLA_KV_RANK), const),
            pl.BlockSpec((MLA_Q_RANK, hq), const),
            pl.BlockSpec((MLA_KV_RANK, hq), const),
            pl.BlockSpec((MLA_KV_RANK, hv), const),
            pl.BlockSpec((PREP_TM, 3 * LANES), lambda i: (i, 0)),
        ],
        out_specs=[
            pl.BlockSpec((PREP_TM, hq), lambda i: (i, 0)),
            pl.BlockSpec((PREP_TM, hq), lambda i: (i, 0)),
            pl.BlockSpec((PREP_TM, hq), lambda i: (i, 0)),
        ],
        out_shape=[
            jax.ShapeDtypeStruct((t, hq), bf16),
            jax.ShapeDtypeStruct((t, hq), bf16),
            jax.ShapeDtypeStruct((t, hq), bf16),
        ],
        compiler_params=pltpu.CompilerParams(
            dimension_semantics=("arbitrary",), vmem_limit_bytes=VMEM_LIMIT),
        name="mla_prep",
    )(a, gq, gkv, wq, wk, wv, rope)


ATT_TQ = 512
ATT_TK = 256


FLASH_TK = 1024


def _lane_fold(x, op):
    out = x[:, :LANES]
    for u in range(1, x.shape[1] // LANES):
        out = op(out, x[:, u * LANES:(u + 1) * LANES])
    return out


def _flash_streams(streams, seq):
    n_chunks = seq // FLASH_TK
    state = [None] * len(streams)
    for c in range(n_chunks):
        rows = slice(c * FLASH_TK, (c + 1) * FLASH_TK)
        for i, (q, k_ref, k_cols, v_ref, v_cols, bias_fn) in enumerate(streams):
            s = _nt_dot(q, k_ref[rows, k_cols])
            if bias_fn is not None:
                s = s + bias_fn(c)
            m_c = jnp.max(_lane_fold(s, jnp.maximum), axis=-1, keepdims=True)
            if c == 0:
                m_new = m_c
            else:
                m_old, acc_old = state[i]
                m_new = jnp.maximum(m_old, m_c)
            p = jnp.exp2(s - m_new).astype(bf16)
            pv = jnp.dot(p, v_ref[rows, v_cols], preferred_element_type=f32)
            state[i] = (m_new, pv if c == 0 else jnp.exp2(m_old - m_new) * acc_old + pv)
    return [acc[:, :LANES] / acc[:, LANES:LANES + 1] for _, acc in state]


def _mla_attn_kernel(q_ref, k_ref, v_ref, o_ref):
    tq = q_ref.shape[0]
    all_cols = slice(0, 2 * LANES)
    streams = [(q_ref[:, hh * LANES:(hh + 1) * LANES], k_ref, slice(hh * LANES, (hh + 1) * LANES),
                v_ref, all_cols, None) for hh in range(2)]
    o0, o1 = _flash_streams(streams, k_ref.shape[0])
    lane = lax.broadcasted_iota(jnp.int32, (tq, LANES), 1)
    o_ref[...] = jnp.where(lane < MLA_V, o0, o1).astype(o_ref.dtype)


def _mla_attn(q, k, v, batch, seq):
    t = q.shape[0]
    nq = seq // ATT_TQ
    n_pairs = MLA_HEADS // 2
    return pl.pallas_call(
        _mla_attn_kernel,
        grid=(batch, n_pairs, nq),
        in_specs=[
            pl.BlockSpec((ATT_TQ, 2 * LANES), lambda b, j, i: (b * nq + i, j)),
            pl.BlockSpec((seq, 2 * LANES), lambda b, j, i: (b, j)),
            pl.BlockSpec((seq, 2 * LANES), lambda b, j, i: (b, j)),
        ],
        out_specs=pl.BlockSpec((ATT_TQ, LANES), lambda b, j, i: (b * nq + i, j)),
        out_shape=jax.ShapeDtypeStruct((t, MLA_HEADS * MLA_V), bf16),
        compiler_params=pltpu.CompilerParams(
            dimension_semantics=("arbitrary", "arbitrary", "arbitrary"), vmem_limit_bytes=VMEM_LIMIT),
        name="mla_attn",
    )(q, k, v)


DIFF_TQ = 512
DIFF_SCALE = DIFF_QK ** -0.5
N_BIAS_TILES = 5


def _diff_attn_kernel(lam_ref, q0_ref, q1_ref, k0_ref, k1_ref, v_ref, bias_ref, gsub_ref, o_ref, vaug_ref):
    qi = pl.program_id(2)
    tq = q0_ref.shape[0]
    seq = v_ref.shape[0]

    @pl.when(qi == 0)
    def _():
        for hh in range(2):
            vaug_ref[hh, :, :DIFF_V] = v_ref[:, hh * DIFF_V:(hh + 1) * DIFF_V]
            vaug_ref[hh, :, DIFF_V:] = jnp.ones((seq, DIFF_V), bf16)

    lv = lam_ref[...]
    lam_init = lv[4:5, 0:1]
    lam = (jnp.exp(jnp.sum(lv[0:1] * lv[1:2], axis=-1, keepdims=True))
           - jnp.exp(jnp.sum(lv[2:3] * lv[3:4], axis=-1, keepdims=True)) + lam_init)
    lane = lax.broadcasted_iota(jnp.int32, (tq, LANES), 1)
    q_refs = (q0_ref, q1_ref)
    k_refs = (k0_ref, k1_ref)
    all_cols = slice(0, LANES)
    tiles_per_chunk = FLASH_TK // ATT_TK
    streams = []
    for hh in range(2):
        head_lanes = (lane >= DIFF_QK) == (hh == 1)
        for m in range(2):
            q = q_refs[m][...]
            q = jnp.where(head_lanes, q, jnp.zeros_like(q))

            def bias_fn(c, m=m, hh=hh):
                def tile(r, cc):
                    offset = (c * tiles_per_chunk + cc) - (qi * (tq // ATT_TK) + r)
                    return bias_ref[m, hh, jnp.clip(offset, -2, 2) + 2]
                return jnp.concatenate(
                    [jnp.concatenate([tile(r, cc) for cc in range(tiles_per_chunk)], axis=1)
                     for r in range(tq // ATT_TK)], axis=0)

            streams.append((q, k_refs[m], all_cols, vaug_ref.at[hh], slice(0, 2 * DIFF_V), bias_fn))
    outs = _flash_streams(streams, seq)
    for hh in range(2):
        o = outs[2 * hh] - lam * outs[2 * hh + 1]
        o = _rms(o, gsub_ref[...]) * (1.0 - lam_init)
        o_ref[:, hh * DIFF_V:(hh + 1) * DIFF_V] = o.astype(o_ref.dtype)


def _diff_attn(lamvec, qkv, bias_tiles, gsub, batch, seq):
    t = qkv.shape[0]
    nq = seq // DIFF_TQ
    n_pairs = DIFF_HEADS // 2
    return pl.pallas_call(
        _diff_attn_kernel,
        grid=(batch, n_pairs, nq),
        in_specs=[
            pl.BlockSpec((8, DIFF_QK), lambda b, j, i: (0, 0)),
            pl.BlockSpec((DIFF_TQ, LANES), lambda b, j, i: (b * nq + i, FQ_BLK + j)),
            pl.BlockSpec((DIFF_TQ, LANES), lambda b, j, i: (b * nq + i, FQ_BLK + n_pairs + j)),
            pl.BlockSpec((seq, LANES), lambda b, j, i: (b, FK_BLK + j)),
            pl.BlockSpec((seq, LANES), lambda b, j, i: (b, FK_BLK + n_pairs + j)),
            pl.BlockSpec((seq, 2 * DIFF_V), lambda b, j, i: (b, FV_BLK // 2 + j)),
            pl.BlockSpec((2, 2, N_BIAS_TILES, ATT_TK, ATT_TK), lambda b, j, i: (0, j, 0, 0, 0)),
            pl.BlockSpec((1, DIFF_V), lambda b, j, i: (0, 0)),
        ],
        out_specs=pl.BlockSpec((DIFF_TQ, 2 * DIFF_V), lambda b, j, i: (b * nq + i, j)),
        out_shape=jax.ShapeDtypeStruct((t, DIFF_HEADS * DIFF_V), bf16),
        scratch_shapes=[pltpu.VMEM((2, seq, 2 * DIFF_V), bf16)],
        compiler_params=pltpu.CompilerParams(
            dimension_semantics=("arbitrary", "arbitrary", "arbitrary"), vmem_limit_bytes=VMEM_LIMIT),
        name="diff_attn",
    )(lamvec, qkv, qkv, qkv, qkv, qkv, bias_tiles, gsub)


DIL_TQ = 128
DIL_TW = DIL_TQ + 2 * DIL_HALF
DIL_SCALE = DIL_QK ** -0.5


def _residue_major(src_ref, dst_ref, stage_ref, dil):
    seq, width = src_ref.shape
    sub_len = seq // dil
    for c in range(width // LANES):
        csl = slice(c * LANES, (c + 1) * LANES)
        stage_ref[c] = src_ref[:, csl].astype(f32)
        for r in range(dil):
            dst_ref[r * sub_len:(r + 1) * sub_len, csl] = (
                stage_ref[c, pl.ds(r, sub_len, stride=dil), :].astype(dst_ref.dtype))


def _dil_attn_kernel(q0_ref, q1_ref, q2_ref, k0_ref, k1_ref, k2_ref, v_ref, bm_ref, o_ref,
                     stage_ref, qp_ref, kp_ref, vp_ref, m_ref, l_ref, acc_ref):
    seq = v_ref.shape[0]
    n_tiles = seq // DIL_TQ
    q_refs = (q0_ref, q1_ref, q2_ref)
    k_refs = (k0_ref, k1_ref, k2_ref)
    lane = lax.broadcasted_iota(jnp.int32, (DIL_TQ, LANES), 1)
    col = lax.broadcasted_iota(jnp.int32, (DIL_TQ, DIL_TW), 1)
    for g, (_, dil) in enumerate(DIL_PATTERNS):
        if dil == 1:
            q_src, k_src, v_src = q_refs[g], k_refs[g], v_ref
        else:
            _residue_major(q_refs[g], qp_ref, stage_ref, dil)
            _residue_major(k_refs[g], kp_ref, stage_ref, dil)
            _residue_major(v_ref, vp_ref, stage_ref, dil)
            q_src, k_src, v_src = qp_ref, kp_ref, vp_ref
        tiles_per_residue = n_tiles // dil

        def tile(n, carry, g=g, dil=dil, q_src=q_src, k_src=k_src, v_src=v_src,
                 tiles_per_residue=tiles_per_residue):
            base = pl.multiple_of(n * DIL_TQ, DIL_TQ)
            t_in = n % tiles_per_residue
            lo = pl.multiple_of(jnp.maximum(base - DIL_HALF, 0), DIL_HALF)
            hi = pl.multiple_of(jnp.minimum(base + DIL_TQ, seq - DIL_HALF), DIL_HALF)
            c_lo = jnp.where(t_in == 0, DIL_HALF, 0)
            c_hi = jnp.where(t_in == tiles_per_residue - 1, DIL_HALF + DIL_TQ, DIL_TW)
            valid = (col >= c_lo) & (col < c_hi)
            if dil == 1:
                rows = pl.ds(base, DIL_TQ)
            else:
                token0 = t_in * (DIL_TQ * dil) + n // tiles_per_residue
                rows = pl.ds(token0, DIL_TQ, stride=dil)
            qt = q_src[pl.ds(base, DIL_TQ), :]
            kw = jnp.concatenate([k_src[pl.ds(lo, DIL_HALF), :], k_src[pl.ds(base, DIL_TQ), :],
                                  k_src[pl.ds(hi, DIL_HALF), :]], axis=0)
            vw = jnp.concatenate([v_src[pl.ds(lo, DIL_HALF), :], v_src[pl.ds(base, DIL_TQ), :],
                                  v_src[pl.ds(hi, DIL_HALF), :]], axis=0)
            for hh in range(2):
                qh = jnp.where((lane >= DIL_QK) == (hh == 1), qt * DIL_SCALE, jnp.zeros_like(qt))
                hsl = slice(hh * DIL_V, (hh + 1) * DIL_V)
                s = _nt_dot(qh, kw) + bm_ref[g, hh]
                s = jnp.where(valid, s, NEG)
                m_t = jnp.max(s, axis=-1, keepdims=True)
                p = jnp.exp(s - m_t)
                l_t = jnp.broadcast_to(jnp.sum(p, axis=-1, keepdims=True), (DIL_TQ, DIL_V))
                u_t = jnp.dot(p.astype(bf16), vw[:, hsl], preferred_element_type=f32)
                m_t = jnp.broadcast_to(m_t, (DIL_TQ, DIL_V))
                if g == 0:
                    m_ref[hh, rows, :] = m_t
                    l_ref[hh, rows, :] = l_t
                    acc_ref[hh, rows, :] = u_t
                else:
                    m_old = m_ref[hh, rows, :]
                    m_new = jnp.maximum(m_old, m_t)
                    a_old = jnp.exp(m_old - m_new)
                    a_t = jnp.exp(m_t - m_new)
                    m_ref[hh, rows, :] = m_new
                    l_ref[hh, rows, :] = a_old * l_ref[hh, rows, :] + a_t * l_t
                    acc_ref[hh, rows, :] = a_old * acc_ref[hh, rows, :] + a_t * u_t
            return carry

        lax.fori_loop(0, n_tiles, tile, 0)
    for hh in range(2):
        o_ref[:, hh * DIL_V:(hh + 1) * DIL_V] = (acc_ref[hh] / l_ref[hh]).astype(o_ref.dtype)


def _dil_attn(qkv, biasmask, batch, seq):
    t = qkv.shape[0]
    n_pairs = DIL_HEADS // 2
    pair_w = 2 * DIL_V

    def qk_spec(blk0, g):
        return pl.BlockSpec((seq, LANES), lambda b, j: (b, blk0 + g * n_pairs + j))

    state = pltpu.VMEM((2, seq, DIL_V), f32)
    return pl.pallas_call(
        _dil_attn_kernel,
        grid=(batch, n_pairs),
        in_specs=[qk_spec(DQ_BLK, g) for g in range(DIL_GROUPS)]
        + [qk_spec(DK_BLK, g) for g in range(DIL_GROUPS)]
        + [pl.BlockSpec((seq, pair_w), lambda b, j: (b, DV_BLK // 2 + j)),
           pl.BlockSpec((DIL_GROUPS, 2, DIL_TQ, DIL_TW), lambda b, j: (0, j, 0, 0))],
        out_specs=pl.BlockSpec((seq, pair_w), lambda b, j: (b, j)),
        out_shape=jax.ShapeDtypeStruct((t, DIL_HEADS * DIL_V), bf16),
        scratch_shapes=[
            pltpu.VMEM((2, seq, LANES), f32),
            pltpu.VMEM((seq, LANES), bf16),
            pltpu.VMEM((seq, LANES), bf16),
            pltpu.VMEM((seq, pair_w), bf16),
            state, state, state,
        ],
        compiler_params=pltpu.CompilerParams(
            dimension_semantics=("arbitrary", "arbitrary"), vmem_limit_bytes=VMEM_LIMIT_DIL),
        name="dil_attn",
    )(qkv, qkv, qkv, qkv, qkv, qkv, qkv, biasmask)


MERGE_TM = 256


def _merge_kernel(x_ref, oa_ref, ob_ref, oc_ref, gate_ref, wb_ref, wo_ref, g_ref, out_ref):
    branches = (oa_ref[...], ob_ref[...], oc_ref[...])
    merged = None
    for n in range(N_BRANCH):
        y = jnp.dot(branches[n], wb_ref[n], preferred_element_type=f32)
        y = gate_ref[:, n * D_MODEL:(n + 1) * D_MODEL] * y
        merged = y if merged is None else merged + y
    y = jnp.dot(merged.astype(bf16), wo_ref[...], preferred_element_type=f32)
    out_ref[...] = x_ref[...] + _rms(y, g_ref[...])


def _merge(x, oa, ob, oc, gates, wb, wo, g):
    t = x.shape[0]
    row = lambda i: (i, 0)
    bw = pl.BlockSpec((MERGE_TM, BRANCH_W), row)
    return pl.pallas_call(
        _merge_kernel,
        grid=(t // MERGE_TM,),
        in_specs=[
            pl.BlockSpec((MERGE_TM, D_MODEL), row),
            bw, bw, bw,
            pl.BlockSpec((MERGE_TM, GATE_COLS), row),
            pl.BlockSpec((N_BRANCH, BRANCH_W, D_MODEL), lambda i: (0, 0, 0)),
            pl.BlockSpec((D_MODEL, D_MODEL), lambda i: (0, 0)),
            pl.BlockSpec((1, D_MODEL), lambda i: (0, 0)),
        ],
        out_specs=pl.BlockSpec((MERGE_TM, D_MODEL), row),
        out_shape=jax.ShapeDtypeStruct((t, D_MODEL), f32),
        compiler_params=pltpu.CompilerParams(
            dimension_semantics=("arbitrary",), vmem_limit_bytes=VMEM_LIMIT),
        name="merge",
    )(x, oa, ob, oc, gates, wb, wo, g)


MLP_TM = 1024
MLP_TF = 512


def _mlp_kernel(x_ref, gpre_ref, wup_ref, wdn_ref, gpost_ref, out_ref, h_ref, acc_ref):
    k = pl.program_id(1)

    @pl.when(k == 0)
    def _():
        h_ref[...] = _rms(x_ref[...], gpre_ref[...]).astype(bf16)
        acc_ref[...] = jnp.zeros_like(acc_ref)

    u = jnp.maximum(jnp.dot(h_ref[...], wup_ref[...], preferred_element_type=f32), 0.0)
    acc_ref[...] += jnp.dot((u * u).astype(bf16), wdn_ref[...], preferred_element_type=f32)

    @pl.when(k == pl.num_programs(1) - 1)
    def _():
        out_ref[...] = x_ref[...] + _rms(acc_ref[...], gpost_ref[...])


def _mlp(x, gpre, wup, wdn, gpost):
    t = x.shape[0]
    return pl.pallas_call(
        _mlp_kernel,
        grid=(t // MLP_TM, D_FF // MLP_TF),
        in_specs=[
            pl.BlockSpec((MLP_TM, D_MODEL), lambda i, k: (i, 0)),
            pl.BlockSpec((1, D_MODEL), lambda i, k: (0, 0)),
            pl.BlockSpec((D_MODEL, MLP_TF), lambda i, k: (0, k)),
            pl.BlockSpec((MLP_TF, D_MODEL), lambda i, k: (k, 0)),
            pl.BlockSpec((1, D_MODEL), lambda i, k: (0, 0)),
        ],
        out_specs=pl.BlockSpec((MLP_TM, D_MODEL), lambda i, k: (i, 0)),
        out_shape=jax.ShapeDtypeStruct((t, D_MODEL), f32),
        scratch_shapes=[pltpu.VMEM((MLP_TM, D_MODEL), bf16), pltpu.VMEM((MLP_TM, D_MODEL), f32)],
        compiler_params=pltpu.CompilerParams(
            dimension_semantics=("arbitrary", "arbitrary"), vmem_limit_bytes=VMEM_LIMIT),
        name="mlp",
    )(x, gpre, wup, wdn, gpost)


def _rel_bucket(rel):
    nb = REL_BUCKETS // 2
    max_exact = nb // 2
    ret = jnp.where(rel > 0, nb, 0)
    n = jnp.abs(rel)
    large = max_exact + (jnp.log(jnp.maximum(n, 1).astype(f32) / max_exact)
                         / math.log(REL_MAX_DIST / max_exact) * (nb - max_exact)).astype(jnp.int32)
    large = jnp.minimum(large, nb - 1)
    return ret + jnp.where(n < max_exact, n, large)


def _bias_lookup(tab, bucket):
    one_hot = jax.nn.one_hot(bucket, REL_BUCKETS, dtype=f32)
    return jnp.einsum('...b,bc->...c', one_hot, tab.astype(f32), precision=lax.Precision.HIGHEST)


def _pad_in_weights(w_in):
    sizes = (MLA_Q_RANK, MLA_KV_RANK, MLA_ROPE)
    c_q = w_in[..., :sizes[0]]
    c_kv = w_in[..., sizes[0]:sizes[0] + sizes[1]]
    k_r = w_in[..., sizes[0] + sizes[1]:sum(sizes)]
    rest = w_in[..., sum(sizes):]
    fq0 = (FQ_BLK - DQ_BLK) * LANES
    fq1 = (FK_BLK - DQ_BLK) * LANES
    rest = jnp.concatenate([rest[..., :fq0], rest[..., fq0:fq1] * (DIFF_SCALE * LOG2E), rest[..., fq1:]], axis=-1)
    z = lambda n: jnp.zeros(w_in.shape[:-1] + (n,), w_in.dtype)
    return jnp.concatenate([c_q, c_kv, z(MLA_NOPE), k_r, z(LANES - MLA_NOPE - MLA_ROPE), rest],
                           axis=-1).astype(bf16)


def _pad_mla_weights(w_uq, w_ukv):
    depth = w_uq.shape[0]
    wq = jnp.pad(w_uq, ((0, 0), (0, 0), (0, 0), (0, LANES - MLA_NOPE - MLA_ROPE)))
    wk = jnp.pad(w_ukv[..., :MLA_NOPE], ((0, 0), (0, 0), (0, 0), (0, LANES - MLA_NOPE)))
    wv = w_ukv[..., MLA_NOPE:]
    return (wq.reshape(depth, MLA_Q_RANK, MLA_HEADS * LANES).astype(bf16),
            wk.reshape(depth, MLA_KV_RANK, MLA_HEADS * LANES).astype(bf16),
            wv.reshape(depth, MLA_KV_RANK, MLA_HEADS * MLA_V).astype(bf16))


def _rope_tables(positions):
    inv = ROPE_BASE ** (-jnp.arange(ROPE_HALF, dtype=f32) / ROPE_HALF)
    ang = positions.reshape(-1).astype(f32)[:, None] * inv
    cos, sin = jnp.cos(ang), jnp.sin(ang)
    t = ang.shape[0]
    one = jnp.ones((t, MLA_NOPE), f32)
    z = lambda n: jnp.zeros((t, n), f32)
    tail = LANES - MLA_NOPE - MLA_ROPE
    c = jnp.concatenate([one, cos, cos, z(tail)], axis=-1)
    sa = jnp.concatenate([z(MLA_NOPE + ROPE_HALF), sin, z(tail)], axis=-1)
    sb = jnp.concatenate([z(MLA_NOPE), -sin, z(ROPE_HALF + tail)], axis=-1)
    return jnp.concatenate([c, sa, sb], axis=-1)


def _diff_bias_tiles(rel_bias):
    tab = rel_bias[:, DIL_BIAS_COLS:]
    i = jnp.arange(ATT_TK)[:, None]
    j = jnp.arange(ATT_TK)[None, :]
    offs = jnp.array([-2, -1, 0, 1, 2])[:, None, None] * ATT_TK
    bucket = _rel_bucket(offs + j - i)
    tiles = jnp.transpose(_bias_lookup(tab, bucket), (3, 0, 1, 2)) * LOG2E
    return tiles.reshape(2, DIFF_HEADS, N_BIAS_TILES, ATT_TK, ATT_TK)


def _dil_bias_masks(rel_bias):
    i = jnp.arange(DIL_TQ)[:, None]
    c = jnp.arange(DIL_TW)[None, :]
    rel = c - DIL_HALF - i
    out = []
    for g, (_, dil) in enumerate(DIL_PATTERNS):
        tab = rel_bias[:, g * DIL_HEADS:(g + 1) * DIL_HEADS]
        b = jnp.transpose(_bias_lookup(tab, _rel_bucket(rel * dil)), (2, 0, 1))
        out.append(jnp.where((jnp.abs(rel) <= DIL_HALF)[None], b, NEG))
    return jnp.stack(out, axis=0)


def kernel(x, positions, rel_bias, g_mix_pre, w_in, g_q, w_uq, g_kv, w_ukv, lam_q1, lam_k1, lam_q2, lam_k2,
           g_diff_sub, w_branch, w_out, g_mix_post, g_mlp_pre, w_up, w_down, g_mlp_post):
    batch, seq, d = x.shape
    depth = w_in.shape[0]
    assert d == D_MODEL and seq % 1024 == 0 and seq // DIL_PATTERNS[-1][1] >= 2 * DIL_TQ

    w_in_p = _pad_in_weights(w_in)
    wq_p, wk_p, wv_p = _pad_mla_weights(w_uq, w_ukv)
    wb = w_branch.astype(bf16)
    wo = w_out.astype(bf16)
    wup = w_up.astype(bf16)
    wdn = w_down.astype(bf16)
    rope = _rope_tables(positions)
    diff_bias = _diff_bias_tiles(rel_bias)
    dil_bias = _dil_bias_masks(rel_bias)

    lam_init = jnp.array([0.8 - 0.6 * math.exp(-0.3 * l) for l in range(depth)], f32)
    lam_row = jnp.zeros((depth, 1, DIFF_QK), f32).at[:, 0, 0].set(lam_init)
    lamvec = jnp.concatenate([lam_q1[:, None], lam_k1[:, None], lam_q2[:, None], lam_k2[:, None],
                              lam_row, jnp.zeros((depth, 3, DIFF_QK), f32)], axis=1).astype(f32)

    row = lambda v: v.reshape(1, -1).astype(f32)
    xt = x.reshape(batch * seq, d)
    for l in range(depth):
        a, qkv, gates = _in_proj(xt, row(g_mix_pre[l]), w_in_p[l])
        q, k, v = _mla_prep(a, row(g_q[l]), row(g_kv[l]), wq_p[l], wk_p[l], wv_p[l], rope)
        o_a = _mla_attn(q, k, v, batch, seq)
        o_b = _dil_attn(qkv, dil_bias, batch, seq)
        o_c = _diff_attn(lamvec[l], qkv, diff_bias, row(g_diff_sub[l]), batch, seq)
        xt = _merge(xt, o_a, o_b, o_c, gates, wb[l], wo[l], row(g_mix_post[l]))
        xt = _mlp(xt, row(g_mlp_pre[l]), wup[l], wdn[l], row(g_mlp_post[l]))
    return xt.reshape(batch, seq, d)
```

```python
import functools
import math

import jax
import jax.numpy as jnp
import numpy as np
from jax import lax
from jax.experimental import pallas as pl
from jax.experimental.pallas import tpu as pltpu

f32 = jnp.float32
bf16 = jnp.bfloat16

D_MODEL = 1024
MLA_HEADS = 8
MLA_Q_RANK = 256
MLA_KV_RANK = 128
MLA_NOPE = 64
MLA_ROPE = 32
MLA_V = 64
ROPE_BASE = 10000.0
DIL_PATTERNS = ((128, 1), (512, 4), (2048, 16))
DIL_GROUPS = 3
DIL_HEADS = 4
DIL_QK = 64
DIL_V = 128
DIL_HALF = 64
DIFF_HEADS = 4
DIFF_QK = 64
DIFF_V = 128
REL_BUCKETS = 32
REL_MAX_DIST = 128
DIL_BIAS_COLS = DIL_GROUPS * DIL_HEADS
N_BRANCH = 3
BRANCH_W = 512
D_FF = 4 * D_MODEL
EPS = 1e-6
NEG = -1e30

LANES = 128
A_COLS = 512
QKV_COLS = 3584
GATE_COLS = N_BRANCH * D_MODEL
IN_PAD_COLS = A_COLS + QKV_COLS + GATE_COLS
DQ_BLK, DK_BLK, DV_BLK, FQ_BLK, FK_BLK, FV_BLK = (A_COLS // LANES + o for o in (0, 6, 12, 16, 20, 24))
GATE_BLK = (A_COLS + QKV_COLS) // D_MODEL

VMEM_LIMIT = 48 * 1024 * 1024
VMEM_LIMIT_DIL = 56 * 1024 * 1024


def _rms(x, g):
    return x * lax.rsqrt(jnp.mean(x * x, axis=-1, keepdims=True) + EPS) * g


def _nt_dot(a, b):
    return lax.dot_general(a, b, (((1,), (1,)), ((), ())), preferred_element_type=f32)


IN_TM = 1024
IN_TN = 1024


def _in_proj_kernel(x_ref, g_ref, w_ref, z_ref, h_ref):
    @pl.when(pl.program_id(1) == 0)
    def _():
        h_ref[...] = _rms(x_ref[...], g_ref[...]).astype(bf16)

    z_ref[...] = jnp.dot(h_ref[...], w_ref[...], preferred_element_type=f32).astype(z_ref.dtype)


def _in_proj(x, g, w):
    t = x.shape[0]
    return pl.pallas_call(
        _in_proj_kernel,
        grid=(t // IN_TM, IN_PAD_COLS // IN_TN),
        in_specs=[
            pl.BlockSpec((IN_TM, D_MODEL), lambda i, j: (i, 0)),
            pl.BlockSpec((1, D_MODEL), lambda i, j: (0, 0)),
            pl.BlockSpec((D_MODEL, IN_TN), lambda i, j: (0, j)),
        ],
        out_specs=pl.BlockSpec((IN_TM, IN_TN), lambda i, j: (i, j)),
        out_shape=jax.ShapeDtypeStruct((t, IN_PAD_COLS), bf16),
        scratch_shapes=[pltpu.VMEM((IN_TM, D_MODEL), bf16)],
        compiler_params=pltpu.CompilerParams(
            dimension_semantics=("arbitrary", "arbitrary"), vmem_limit_bytes=VMEM_LIMIT),
        name="in_proj",
    )(x, g, w)


PREP_TM = 512
MLA_SCALE = (MLA_NOPE + MLA_ROPE) ** -0.5
LOG2E = math.log2(math.e)
ROPE_HALF = MLA_ROPE // 2


def _rope_lanes(x, rope):
    c = rope[:, 0:LANES]
    sa = rope[:, LANES:2 * LANES]
    sb = rope[:, 2 * LANES:3 * LANES]
    return (x * c + pltpu.roll(x, ROPE_HALF, 1) * sa + pltpu.roll(x, LANES - ROPE_HALF, 1) * sb)


def _mla_prep_kernel(a_ref, gq_ref, gkv_ref, wq_ref, wk_ref, wv_ref, rope_ref, q_ref, k_ref, v_ref):
    a = a_ref[...].astype(f32)
    rope = rope_ref[...]
    cq = _rms(a[:, :MLA_Q_RANK], gq_ref[...]).astype(bf16)
    ckv = _rms(a[:, MLA_Q_RANK:MLA_Q_RANK + MLA_KV_RANK], gkv_ref[...]).astype(bf16)
    k_rope = _rope_lanes(a[:, A_COLS - LANES:], rope)
    qf = jnp.dot(cq, wq_ref[...], preferred_element_type=f32)
    kf = jnp.dot(ckv, wk_ref[...], preferred_element_type=f32)
    vf = jnp.dot(ckv, wv_ref[...], preferred_element_type=f32).astype(bf16)
    ones = jnp.ones((vf.shape[0], LANES), bf16)
    for j in range(MLA_HEADS // 2):
        v_ref[:, 2 * j * LANES:(2 * j + 1) * LANES] = vf[:, j * LANES:(j + 1) * LANES]
        v_ref[:, (2 * j + 1) * LANES:(2 * j + 2) * LANES] = ones
    for h in range(MLA_HEADS):
        sl = slice(h * LANES, (h + 1) * LANES)
        q_ref[:, sl] = (_rope_lanes(qf[:, sl], rope) * (MLA_SCALE * LOG2E)).astype(bf16)
        k_ref[:, sl] = (kf[:, sl] + k_rope).astype(bf16)


def _mla_prep(a, gq, gkv, wq, wk, wv, rope):
    t = a.shape[0]
    hq = MLA_HEADS * LANES
    hv = MLA_HEADS * MLA_V
    const = lambda i: (0, 0)
    return pl.pallas_call(
        _mla_prep_kernel,
        grid=(t // PREP_TM,),
        in_specs=[
            pl.BlockSpec((PREP_TM, A_COLS), lambda i: (i, 0)),
            pl.BlockSpec((1, MLA_Q_RANK), const),
            pl.BlockSpec((1, MLA_KV_RANK), const),
            pl.BlockSpec((MLA_Q_RANK, hq), const),
            pl.BlockSpec((MLA_KV_RANK, hq), const),
            pl.BlockSpec((MLA_KV_RANK, hv), const),
            pl.BlockSpec((PREP_TM, 3 * LANES), lambda i: (i, 0)),
        ],
        out_specs=[
            pl.BlockSpec((PREP_TM, hq), lambda i: (i, 0)),
            pl.BlockSpec((PREP_TM, hq), lambda i: (i, 0)),
            pl.BlockSpec((PREP_TM, hq), lambda i: (i, 0)),
        ],
        out_shape=[
            jax.ShapeDtypeStruct((t, hq), bf16),
            jax.ShapeDtypeStruct((t, hq), bf16),
            jax.ShapeDtypeStruct((t, hq), bf16),
        ],
        compiler_params=pltpu.CompilerParams(
            dimension_semantics=("arbitrary",), vmem_limit_bytes=VMEM_LIMIT),
        name="mla_prep",
    )(a, gq, gkv, wq, wk, wv, rope)


ATT_TQ = 512
ATT_TK = 256


FLASH_TK = 1024


def _lane_fold(x, op):
    out = x[:, :LANES]
    for u in range(1, x.shape[1] // LANES):
        out = op(out, x[:, u * LANES:(u + 1) * LANES])
    return out


def _flash_streams(streams, seq):
    n_chunks = seq // FLASH_TK
    state = [None] * len(streams)
    for c in range(n_chunks):
        rows = slice(c * FLASH_TK, (c + 1) * FLASH_TK)
        for i, (q, k_ref, k_cols, v_ref, v_cols, bias_fn) in enumerate(streams):
            s = _nt_dot(q, k_ref[rows, k_cols])
            if bias_fn is not None:
                s = s + bias_fn(c)
            m_c = jnp.max(_lane_fold(s, jnp.maximum), axis=-1, keepdims=True)
            if c == 0:
                m_new = m_c
            else:
                m_old, acc_old = state[i]
                m_new = jnp.maximum(m_old, m_c)
            p = jnp.exp2(s - m_new).astype(bf16)
            pv = jnp.dot(p, v_ref[rows, v_cols], preferred_element_type=f32)
            state[i] = (m_new, pv if c == 0 else jnp.exp2(m_old - m_new) * acc_old + pv)
    return [acc[:, :LANES] / acc[:, LANES:LANES + 1] for _, acc in state]


def _mla_attn_kernel(q_ref, k_ref, v_ref, o_ref):
    tq = q_ref.shape[0]
    all_cols = slice(0, 2 * LANES)
    streams = [(q_ref[:, hh * LANES:(hh + 1) * LANES], k_ref, slice(hh * LANES, (hh + 1) * LANES),
                v_ref, all_cols, None) for hh in range(2)]
    o0, o1 = _flash_streams(streams, k_ref.shape[0])
    lane = lax.broadcasted_iota(jnp.int32, (tq, LANES), 1)
    o_ref[...] = jnp.where(lane < MLA_V, o0, o1).astype(o_ref.dtype)


def _mla_attn(q, k, v, batch, seq):
    t = q.shape[0]
    nq = seq // ATT_TQ
    n_pairs = MLA_HEADS // 2
    return pl.pallas_call(
        _mla_attn_kernel,
        grid=(batch, n_pairs, nq),
        in_specs=[
            pl.BlockSpec((ATT_TQ, 2 * LANES), lambda b, j, i: (b * nq + i, j)),
            pl.BlockSpec((seq, 2 * LANES), lambda b, j, i: (b, j)),
            pl.BlockSpec((seq, 2 * LANES), lambda b, j, i: (b, j)),
        ],
        out_specs=pl.BlockSpec((ATT_TQ, LANES), lambda b, j, i: (b * nq + i, j)),
        out_shape=jax.ShapeDtypeStruct((t, MLA_HEADS * MLA_V), bf16),
        compiler_params=pltpu.CompilerParams(
            dimension_semantics=("arbitrary", "arbitrary", "arbitrary"), vmem_limit_bytes=VMEM_LIMIT),
        name="mla_attn",
    )(q, k, v)


DIFF_TQ = 512
DIFF_SCALE = DIFF_QK ** -0.5
N_BIAS_TILES = 5


def _diff_attn_kernel(lam_ref, q0_ref, q1_ref, k0_ref, k1_ref, v_ref, bias_ref, gsub_ref, o_ref, vaug_ref):
    qi = pl.program_id(2)
    tq = q0_ref.shape[0]
    seq = v_ref.shape[0]

    @pl.when(qi == 0)
    def _():
        for hh in range(2):
            vaug_ref[hh, :, :DIFF_V] = v_ref[:, hh * DIFF_V:(hh + 1) * DIFF_V]
            vaug_ref[hh, :, DIFF_V:] = jnp.ones((seq, DIFF_V), bf16)

    lv = lam_ref[...]
    lam_init = lv[4:5, 0:1]
    lam = (jnp.exp(jnp.sum(lv[0:1] * lv[1:2], axis=-1, keepdims=True))
           - jnp.exp(jnp.sum(lv[2:3] * lv[3:4], axis=-1, keepdims=True)) + lam_init)
    lane = lax.broadcasted_iota(jnp.int32, (tq, LANES), 1)
    q_refs = (q0_ref, q1_ref)
    k_refs = (k0_ref, k1_ref)
    all_cols = slice(0, LANES)
    tiles_per_chunk = FLASH_TK // ATT_TK
    streams = []
    for hh in range(2):
        head_lanes = (lane >= DIFF_QK) == (hh == 1)
        for m in range(2):
            q = q_refs[m][...]
            q = jnp.where(head_lanes, q, jnp.zeros_like(q))

            def bias_fn(c, m=m, hh=hh):
                def tile(r, cc):
                    offset = (c * tiles_per_chunk + cc) - (qi * (tq // ATT_TK) + r)
                    return bias_ref[m, hh, jnp.clip(offset, -2, 2) + 2]
                return jnp.concatenate(
                    [jnp.concatenate([tile(r, cc) for cc in range(tiles_per_chunk)], axis=1)
                     for r in range(tq // ATT_TK)], axis=0)

            streams.append((q, k_refs[m], all_cols, vaug_ref.at[hh], slice(0, 2 * DIFF_V), bias_fn))
    outs = _flash_streams(streams, seq)
    for hh in range(2):
        o = outs[2 * hh] - lam * outs[2 * hh + 1]
        o = _rms(o, gsub_ref[...]) * (1.0 - lam_init)
        o_ref[:, hh * DIFF_V:(hh + 1) * DIFF_V] = o.astype(o_ref.dtype)


def _diff_attn(lamvec, qkv, bias_tiles, gsub, batch, seq):
    t = qkv.shape[0]
    nq = seq // DIFF_TQ
    n_pairs = DIFF_HEADS // 2
    return pl.pallas_call(
        _diff_attn_kernel,
        grid=(batch, n_pairs, nq),
        in_specs=[
            pl.BlockSpec((8, DIFF_QK), lambda b, j, i: (0, 0)),
            pl.BlockSpec((DIFF_TQ, LANES), lambda b, j, i: (b * nq + i, FQ_BLK + j)),
            pl.BlockSpec((DIFF_TQ, LANES), lambda b, j, i: (b * nq + i, FQ_BLK + n_pairs + j)),
            pl.BlockSpec((seq, LANES), lambda b, j, i: (b, FK_BLK + j)),
            pl.BlockSpec((seq, LANES), lambda b, j, i: (b, FK_BLK + n_pairs + j)),
            pl.BlockSpec((seq, 2 * DIFF_V), lambda b, j, i: (b, FV_BLK // 2 + j)),
            pl.BlockSpec((2, 2, N_BIAS_TILES, ATT_TK, ATT_TK), lambda b, j, i: (0, j, 0, 0, 0)),
            pl.BlockSpec((1, DIFF_V), lambda b, j, i: (0, 0)),
        ],
        out_specs=pl.BlockSpec((DIFF_TQ, 2 * DIFF_V), lambda b, j, i: (b * nq + i, j)),
        out_shape=jax.ShapeDtypeStruct((t, DIFF_HEADS * DIFF_V), bf16),
        scratch_shapes=[pltpu.VMEM((2, seq, 2 * DIFF_V), bf16)],
        compiler_params=pltpu.CompilerParams(
            dimension_semantics=("arbitrary", "arbitrary", "arbitrary"), vmem_limit_bytes=VMEM_LIMIT),
        name="diff_attn",
    )(lamvec, qkv, qkv, qkv, qkv, qkv, bias_tiles, gsub)


DIL_TQ = 128
DIL_TW = DIL_TQ + 2 * DIL_HALF
DIL_SCALE = DIL_QK ** -0.5
DIL_UNROLL = 4


def _residue_major(src_ref, dst_ref, stage_ref, dil):
    seq, width = src_ref.shape
    sub_len = seq // dil
    for c in range(width // LANES):
        csl = slice(c * LANES, (c + 1) * LANES)
        stage_ref[c] = src_ref[:, csl].astype(f32)
        for r in range(dil):
            dst_ref[r * sub_len:(r + 1) * sub_len, csl] = (
                stage_ref[c, pl.ds(r, sub_len, stride=dil), :].astype(dst_ref.dtype))


def _dil_attn_kernel(q0_ref, q1_ref, q2_ref, k0_ref, k1_ref, k2_ref, v_ref, bm_ref, o_ref,
                     stage_ref, qp_ref, kp_ref, vp_ref, m_ref, l_ref, acc_ref):
    seq = v_ref.shape[0]
    n_tiles = seq // DIL_TQ
    q_refs = (q0_ref, q1_ref, q2_ref)
    k_refs = (k0_ref, k1_ref, k2_ref)
    lane = lax.broadcasted_iota(jnp.int32, (DIL_TQ, LANES), 1)
    col = lax.broadcasted_iota(jnp.int32, (DIL_TQ, DIL_TW), 1)
    for g, (_, dil) in enumerate(DIL_PATTERNS):
        if dil == 1:
            q_src, k_src, v_src = q_refs[g], k_refs[g], v_ref
        else:
            _residue_major(q_refs[g], qp_ref, stage_ref, dil)
            _residue_major(k_refs[g], kp_ref, stage_ref, dil)
            _residue_major(v_ref, vp_ref, stage_ref, dil)
            q_src, k_src, v_src = qp_ref, kp_ref, vp_ref
        tiles_per_residue = n_tiles // dil

        def tile(n, carry, g=g, dil=dil, q_src=q_src, k_src=k_src, v_src=v_src,
                 tiles_per_residue=tiles_per_residue):
            base = pl.multiple_of(n * DIL_TQ, DIL_TQ)
            t_in = n % tiles_per_residue
            lo = pl.multiple_of(jnp.maximum(base - DIL_HALF, 0), DIL_HALF)
            hi = pl.multiple_of(jnp.minimum(base + DIL_TQ, seq - DIL_HALF), DIL_HALF)
            c_lo = jnp.where(t_in == 0, DIL_HALF, 0)
            c_hi = jnp.where(t_in == tiles_per_residue - 1, DIL_HALF + DIL_TQ, DIL_TW)
            valid = (col >= c_lo) & (col < c_hi)
            if dil == 1:
                rows = pl.ds(base, DIL_TQ)
            else:
                token0 = t_in * (DIL_TQ * dil) + n // tiles_per_residue
                rows = pl.ds(token0, DIL_TQ, stride=dil)
            qt = q_src[pl.ds(base, DIL_TQ), :]
            kw = jnp.concatenate([k_src[pl.ds(lo, DIL_HALF), :], k_src[pl.ds(base, DIL_TQ), :],
                                  k_src[pl.ds(hi, DIL_HALF), :]], axis=0)
            vw = jnp.concatenate([v_src[pl.ds(lo, DIL_HALF), :], v_src[pl.ds(base, DIL_TQ), :],
                                  v_src[pl.ds(hi, DIL_HALF), :]], axis=0)
            for hh in range(2):
                qh = jnp.where((lane >= DIL_QK) == (hh == 1), qt * DIL_SCALE, jnp.zeros_like(qt))
                hsl = slice(hh * DIL_V, (hh + 1) * DIL_V)
                s = _nt_dot(qh, kw) + bm_ref[g, hh]
                s = jnp.where(valid, s, NEG)
                m_t = jnp.max(s, axis=-1, keepdims=True)
                p = jnp.exp(s - m_t)
                l_t = jnp.broadcast_to(jnp.sum(p, axis=-1, keepdims=True), (DIL_TQ, DIL_V))
                u_t = jnp.dot(p.astype(bf16), vw[:, hsl], preferred_element_type=f32)
                m_t = jnp.broadcast_to(m_t, (DIL_TQ, DIL_V))
                if g == 0:
                    m_ref[hh, rows, :] = m_t
                    l_ref[hh, rows, :] = l_t
                    acc_ref[hh, rows, :] = u_t
                else:
                    m_old = m_ref[hh, rows, :]
                    m_new = jnp.maximum(m_old, m_t)
                    a_old = jnp.exp(m_old - m_new)
                    a_t = jnp.exp(m_t - m_new)
                    m_ref[hh, rows, :] = m_new
                    l_ref[hh, rows, :] = a_old * l_ref[hh, rows, :] + a_t * l_t
                    acc_ref[hh, rows, :] = a_old * acc_ref[hh, rows, :] + a_t * u_t
            return carry

        def tile_group(i, carry, tile=tile):
            for u in range(DIL_UNROLL):
                tile(i * DIL_UNROLL + u, carry)
            return carry

        lax.fori_loop(0, n_tiles // DIL_UNROLL, tile_group, 0)
    for hh in range(2):
        o_ref[:, hh * DIL_V:(hh + 1) * DIL_V] = (acc_ref[hh] / l_ref[hh]).astype(o_ref.dtype)


def _dil_attn(qkv, biasmask, batch, seq):
    t = qkv.shape[0]
    n_pairs = DIL_HEADS // 2
    pair_w = 2 * DIL_V

    def qk_spec(blk0, g):
        return pl.BlockSpec((seq, LANES), lambda b, j: (b, blk0 + g * n_pairs + j))

    state = pltpu.VMEM((2, seq, DIL_V), f32)
    return pl.pallas_call(
        _dil_attn_kernel,
        grid=(batch, n_pairs),
        in_specs=[qk_spec(DQ_BLK, g) for g in range(DIL_GROUPS)]
        + [qk_spec(DK_BLK, g) for g in range(DIL_GROUPS)]
        + [pl.BlockSpec((seq, pair_w), lambda b, j: (b, DV_BLK // 2 + j)),
           pl.BlockSpec((DIL_GROUPS, 2, DIL_TQ, DIL_TW), lambda b, j: (0, j, 0, 0))],
        out_specs=pl.BlockSpec((seq, pair_w), lambda b, j: (b, j)),
        out_shape=jax.ShapeDtypeStruct((t, DIL_HEADS * DIL_V), bf16),
        scratch_shapes=[
            pltpu.VMEM((2, seq, LANES), f32),
            pltpu.VMEM((seq, LANES), bf16),
            pltpu.VMEM((seq, LANES), bf16),
            pltpu.VMEM((seq, pair_w), bf16),
            state, state, state,
        ],
        compiler_params=pltpu.CompilerParams(
            dimension_semantics=("arbitrary", "arbitrary"), vmem_limit_bytes=VMEM_LIMIT_DIL),
        name="dil_attn",
    )(qkv, qkv, qkv, qkv, qkv, qkv, qkv, biasmask)


MERGE_TM = 512


def _merge_kernel(x_ref, oa_ref, ob_ref, oc_ref, gz0_ref, gz1_ref, gz2_ref, wb_ref, wo_ref, g_ref, out_ref):
    branches = (oa_ref[...], ob_ref[...], oc_ref[...])
    gate_refs = (gz0_ref, gz1_ref, gz2_ref)
    merged = None
    for n in range(N_BRANCH):
        y = jnp.dot(branches[n], wb_ref[n], preferred_element_type=f32)
        y = jax.nn.sigmoid(gate_refs[n][...].astype(f32)) * y
        merged = y if merged is None else merged + y
    y = jnp.dot(merged.astype(bf16), wo_ref[...], preferred_element_type=f32)
    out_ref[...] = x_ref[...] + _rms(y, g_ref[...])


def _merge(x, oa, ob, oc, z, wb, wo, g):
    t = x.shape[0]
    row = lambda i: (i, 0)
    bw = pl.BlockSpec((MERGE_TM, BRANCH_W), row)
    gate_specs = [pl.BlockSpec((MERGE_TM, D_MODEL), lambda i, n=n: (i, GATE_BLK + n)) for n in range(N_BRANCH)]
    return pl.pallas_call(
        _merge_kernel,
        grid=(t // MERGE_TM,),
        in_specs=[
            pl.BlockSpec((MERGE_TM, D_MODEL), row),
            bw, bw, bw,
            *gate_specs,
            pl.BlockSpec((N_BRANCH, BRANCH_W, D_MODEL), lambda i: (0, 0, 0)),
            pl.BlockSpec((D_MODEL, D_MODEL), lambda i: (0, 0)),
            pl.BlockSpec((1, D_MODEL), lambda i: (0, 0)),
        ],
        out_specs=pl.BlockSpec((MERGE_TM, D_MODEL), row),
        out_shape=jax.ShapeDtypeStruct((t, D_MODEL), f32),
        compiler_params=pltpu.CompilerParams(
            dimension_semantics=("arbitrary",), vmem_limit_bytes=VMEM_LIMIT),
        name="merge",
    )(x, oa, ob, oc, z, z, z, wb, wo, g)


MLP_TM = 1024
MLP_TF = 512


def _mlp_kernel(x_ref, gpre_ref, wup_ref, wdn_ref, gpost_ref, out_ref, h_ref, acc_ref):
    k = pl.program_id(1)

    @pl.when(k == 0)
    def _():
        h_ref[...] = _rms(x_ref[...], gpre_ref[...]).astype(bf16)
        acc_ref[...] = jnp.zeros_like(acc_ref)

    u = jnp.maximum(jnp.dot(h_ref[...], wup_ref[...], preferred_element_type=f32), 0.0)
    acc_ref[...] += jnp.dot((u * u).astype(bf16), wdn_ref[...], preferred_element_type=f32)

    @pl.when(k == pl.num_programs(1) - 1)
    def _():
        out_ref[...] = x_ref[...] + _rms(acc_ref[...], gpost_ref[...])


def _mlp(x, gpre, wup, wdn, gpost):
    t = x.shape[0]
    return pl.pallas_call(
        _mlp_kernel,
        grid=(t // MLP_TM, D_FF // MLP_TF),
        in_specs=[
            pl.BlockSpec((MLP_TM, D_MODEL), lambda i, k: (i, 0)),
            pl.BlockSpec((1, D_MODEL), lambda i, k: (0, 0)),
            pl.BlockSpec((D_MODEL, MLP_TF), lambda i, k: (0, k)),
            pl.BlockSpec((MLP_TF, D_MODEL), lambda i, k: (k, 0)),
            pl.BlockSpec((1, D_MODEL), lambda i, k: (0, 0)),
        ],
        out_specs=pl.BlockSpec((MLP_TM, D_MODEL), lambda i, k: (i, 0)),
        out_shape=jax.ShapeDtypeStruct((t, D_MODEL), f32),
        scratch_shapes=[pltpu.VMEM((MLP_TM, D_MODEL), bf16), pltpu.VMEM((MLP_TM, D_MODEL), f32)],
        compiler_params=pltpu.CompilerParams(
            dimension_semantics=("arbitrary", "arbitrary"), vmem_limit_bytes=VMEM_LIMIT),
        name="mlp",
    )(x, gpre, wup, wdn, gpost)


def _rel_bucket(rel):
    nb = REL_BUCKETS // 2
    max_exact = nb // 2
    ret = jnp.where(rel > 0, nb, 0)
    n = jnp.abs(rel)
    large = max_exact + (jnp.log(jnp.maximum(n, 1).astype(f32) / max_exact)
                         / math.log(REL_MAX_DIST / max_exact) * (nb - max_exact)).astype(jnp.int32)
    large = jnp.minimum(large, nb - 1)
    return ret + jnp.where(n < max_exact, n, large)


def _bias_lookup(tab, bucket):
    one_hot = jax.nn.one_hot(bucket, REL_BUCKETS, dtype=f32)
    return jnp.einsum('...b,bc->...c', one_hot, tab.astype(f32), precision=lax.Precision.HIGHEST)


def _pad_in_weights(w_in):
    sizes = (MLA_Q_RANK, MLA_KV_RANK, MLA_ROPE)
    c_q = w_in[..., :sizes[0]]
    c_kv = w_in[..., sizes[0]:sizes[0] + sizes[1]]
    k_r = w_in[..., sizes[0] + sizes[1]:sum(sizes)]
    rest = w_in[..., sum(sizes):]
    fq0 = (FQ_BLK - DQ_BLK) * LANES
    fq1 = (FK_BLK - DQ_BLK) * LANES
    rest = jnp.concatenate([rest[..., :fq0], rest[..., fq0:fq1] * (DIFF_SCALE * LOG2E), rest[..., fq1:]], axis=-1)
    z = lambda n: jnp.zeros(w_in.shape[:-1] + (n,), w_in.dtype)
    return jnp.concatenate([c_q, c_kv, z(MLA_NOPE), k_r, z(LANES - MLA_NOPE - MLA_ROPE), rest],
                           axis=-1).astype(bf16)


def _pad_mla_weights(w_uq, w_ukv):
    depth = w_uq.shape[0]
    wq = jnp.pad(w_uq, ((0, 0), (0, 0), (0, 0), (0, LANES - MLA_NOPE - MLA_ROPE)))
    wk = jnp.pad(w_ukv[..., :MLA_NOPE], ((0, 0), (0, 0), (0, 0), (0, LANES - MLA_NOPE)))
    wv = w_ukv[..., MLA_NOPE:]
    return (wq.reshape(depth, MLA_Q_RANK, MLA_HEADS * LANES).astype(bf16),
            wk.reshape(depth, MLA_KV_RANK, MLA_HEADS * LANES).astype(bf16),
            wv.reshape(depth, MLA_KV_RANK, MLA_HEADS * MLA_V).astype(bf16))


def _rope_tables(positions):
    inv = ROPE_BASE ** (-jnp.arange(ROPE_HALF, dtype=f32) / ROPE_HALF)
    ang = positions.reshape(-1).astype(f32)[:, None] * inv
    cos, sin = jnp.cos(ang), jnp.sin(ang)
    t = ang.shape[0]
    one = jnp.ones((t, MLA_NOPE), f32)
    z = lambda n: jnp.zeros((t, n), f32)
    tail = LANES - MLA_NOPE - MLA_ROPE
    c = jnp.concatenate([one, cos, cos, z(tail)], axis=-1)
    sa = jnp.concatenate([z(MLA_NOPE + ROPE_HALF), sin, z(tail)], axis=-1)
    sb = jnp.concatenate([z(MLA_NOPE), -sin, z(ROPE_HALF + tail)], axis=-1)
    return jnp.concatenate([c, sa, sb], axis=-1)


def _diff_bias_tiles(rel_bias):
    tab = rel_bias[:, DIL_BIAS_COLS:]
    i = jnp.arange(ATT_TK)[:, None]
    j = jnp.arange(ATT_TK)[None, :]
    offs = jnp.array([-2, -1, 0, 1, 2])[:, None, None] * ATT_TK
    bucket = _rel_bucket(offs + j - i)
    tiles = jnp.transpose(_bias_lookup(tab, bucket), (3, 0, 1, 2)) * LOG2E
    return tiles.reshape(2, DIFF_HEADS, N_BIAS_TILES, ATT_TK, ATT_TK)


def _dil_bias_masks(rel_bias):
    i = jnp.arange(DIL_TQ)[:, None]
    c = jnp.arange(DIL_TW)[None, :]
    rel = c - DIL_HALF - i
    out = []
    for g, (_, dil) in enumerate(DIL_PATTERNS):
        tab = rel_bias[:, g * DIL_HEADS:(g + 1) * DIL_HEADS]
        b = jnp.transpose(_bias_lookup(tab, _rel_bucket(rel * dil)), (2, 0, 1))
        out.append(jnp.where((jnp.abs(rel) <= DIL_HALF)[None], b, NEG))
    return jnp.stack(out, axis=0)


def kernel(x, positions, rel_bias, g_mix_pre, w_in, g_q, w_uq, g_kv, w_ukv, lam_q1, lam_k1, lam_q2, lam_k2,
           g_diff_sub, w_branch, w_out, g_mix_post, g_mlp_pre, w_up, w_down, g_mlp_post):
    batch, seq, d = x.shape
    depth = w_in.shape[0]
    assert d == D_MODEL and seq % 1024 == 0 and seq // DIL_PATTERNS[-1][1] >= 2 * DIL_TQ

    w_in_p = _pad_in_weights(w_in)
    wq_p, wk_p, wv_p = _pad_mla_weights(w_uq, w_ukv)
    wb = w_branch.astype(bf16)
    wo = w_out.astype(bf16)
    wup = w_up.astype(bf16)
    wdn = w_down.astype(bf16)
    rope = _rope_tables(positions)
    diff_bias = _diff_bias_tiles(rel_bias)
    dil_bias = _dil_bias_masks(rel_bias)

    lam_init = jnp.array([0.8 - 0.6 * math.exp(-0.3 * l) for l in range(depth)], f32)
    lam_row = jnp.zeros((depth, 1, DIFF_QK), f32).at[:, 0, 0].set(lam_init)
    lamvec = jnp.concatenate([lam_q1[:, None], lam_k1[:, None], lam_q2[:, None], lam_k2[:, None],
                              lam_row, jnp.zeros((depth, 3, DIFF_QK), f32)], axis=1).astype(f32)

    row = lambda v: v.reshape(1, -1).astype(f32)
    xt = x.reshape(batch * seq, d)
    for l in range(depth):
        z = _in_proj(xt, row(g_mix_pre[l]), w_in_p[l])
        q, k, v = _mla_prep(z, row(g_q[l]), row(g_kv[l]), wq_p[l], wk_p[l], wv_p[l], rope)
        o_a = _mla_attn(q, k, v, batch, seq)
        o_b = _dil_attn(z, dil_bias, batch, seq)
        o_c = _diff_attn(lamvec[l], z, diff_bias, row(g_diff_sub[l]), batch, seq)
        xt = _merge(xt, o_a, o_b, o_c, z, wb[l], wo[l], row(g_mix_post[l]))
        xt = _mlp(xt, row(g_mlp_pre[l]), wup[l], wdn[l], row(g_mlp_post[l]))
    return xt.reshape(batch, seq, d)
```

```python
import functools
import math

import jax
import jax.numpy as jnp
import numpy as np
from jax import lax
from jax.experimental import pallas as pl
from jax.experimental.pallas import tpu as pltpu

f32 = jnp.float32
bf16 = jnp.bfloat16

D_MODEL = 1024
MLA_HEADS = 8
MLA_Q_RANK = 256
MLA_KV_RANK = 128
MLA_NOPE = 64
MLA_ROPE = 32
MLA_V = 64
ROPE_BASE = 10000.0
DIL_PATTERNS = ((128, 1), (512, 4), (2048, 16))
DIL_GROUPS = 3
DIL_HEADS = 4
DIL_QK = 64
DIL_V = 128
DIL_HALF = 64
DIFF_HEADS = 4
DIFF_QK = 64
DIFF_V = 128
REL_BUCKETS = 32
REL_MAX_DIST = 128
DIL_BIAS_COLS = DIL_GROUPS * DIL_HEADS
N_BRANCH = 3
BRANCH_W = 512
D_FF = 4 * D_MODEL
EPS = 1e-6
NEG = -1e30

LANES = 128
A_COLS = 512
QKV_COLS = 3584
GATE_COLS = N_BRANCH * D_MODEL
IN_PAD_COLS = A_COLS + QKV_COLS + GATE_COLS
DQ_BLK, DK_BLK, DV_BLK, FQ_BLK, FK_BLK, FV_BLK = (A_COLS // LANES + o for o in (0, 6, 12, 16, 20, 24))
GATE_BLK = (A_COLS + QKV_COLS) // D_MODEL

VMEM_LIMIT = 48 * 1024 * 1024
VMEM_LIMIT_DIL = 56 * 1024 * 1024


def _rms(x, g):
    return x * lax.rsqrt(jnp.mean(x * x, axis=-1, keepdims=True) + EPS) * g


def _nt_dot(a, b):
    return lax.dot_general(a, b, (((1,), (1,)), ((), ())), preferred_element_type=f32)


IN_TM = 1024
IN_TN = 1024


def _in_proj_kernel(x_ref, g_ref, w_ref, z_ref, h_ref):
    @pl.when(pl.program_id(1) == 0)
    def _():
        h_ref[...] = _rms(x_ref[...], g_ref[...]).astype(bf16)

    z_ref[...] = jnp.dot(h_ref[...], w_ref[...], preferred_element_type=f32).astype(z_ref.dtype)


def _in_proj(x, g, w):
    t = x.shape[0]
    return pl.pallas_call(
        _in_proj_kernel,
        grid=(t // IN_TM, IN_PAD_COLS // IN_TN),
        in_specs=[
            pl.BlockSpec((IN_TM, D_MODEL), lambda i, j: (i, 0)),
            pl.BlockSpec((1, D_MODEL), lambda i, j: (0, 0)),
            pl.BlockSpec((D_MODEL, IN_TN), lambda i, j: (0, j)),
        ],
        out_specs=pl.BlockSpec((IN_TM, IN_TN), lambda i, j: (i, j)),
        out_shape=jax.ShapeDtypeStruct((t, IN_PAD_COLS), bf16),
        scratch_shapes=[pltpu.VMEM((IN_TM, D_MODEL), bf16)],
        compiler_params=pltpu.CompilerParams(
            dimension_semantics=("arbitrary", "arbitrary"), vmem_limit_bytes=VMEM_LIMIT),
        name="in_proj",
    )(x, g, w)


PREP_TM = 512
MLA_SCALE = (MLA_NOPE + MLA_ROPE) ** -0.5
LOG2E = math.log2(math.e)
ROPE_HALF = MLA_ROPE // 2


def _rope_lanes(x, rope):
    c = rope[:, 0:LANES]
    sa = rope[:, LANES:2 * LANES]
    sb = rope[:, 2 * LANES:3 * LANES]
    return (x * c + pltpu.roll(x, ROPE_HALF, 1) * sa + pltpu.roll(x, LANES - ROPE_HALF, 1) * sb)


def _mla_prep_kernel(a_ref, gq_ref, gkv_ref, wq_ref, wk_ref, wv_ref, rope_ref, q_ref, k_ref, v_ref):
    a = a_ref[...].astype(f32)
    rope = rope_ref[...]
    cq = _rms(a[:, :MLA_Q_RANK], gq_ref[...]).astype(bf16)
    ckv = _rms(a[:, MLA_Q_RANK:MLA_Q_RANK + MLA_KV_RANK], gkv_ref[...]).astype(bf16)
    k_rope = _rope_lanes(a[:, A_COLS - LANES:], rope)
    qf = jnp.dot(cq, wq_ref[...], preferred_element_type=f32)
    kf = jnp.dot(ckv, wk_ref[...], preferred_element_type=f32)
    vf = jnp.dot(ckv, wv_ref[...], preferred_element_type=f32).astype(bf16)
    ones = jnp.ones((vf.shape[0], LANES), bf16)
    for j in range(MLA_HEADS // 2):
        v_ref[:, 2 * j * LANES:(2 * j + 1) * LANES] = vf[:, j * LANES:(j + 1) * LANES]
        v_ref[:, (2 * j + 1) * LANES:(2 * j + 2) * LANES] = ones
    for h in range(MLA_HEADS):
        sl = slice(h * LANES, (h + 1) * LANES)
        q_ref[:, sl] = (_rope_lanes(qf[:, sl], rope) * (MLA_SCALE * LOG2E)).astype(bf16)
        k_ref[:, sl] = (kf[:, sl] + k_rope).astype(bf16)


def _mla_prep(a, gq, gkv, wq, wk, wv, rope):
    t = a.shape[0]
    hq = MLA_HEADS * LANES
    hv = MLA_HEADS * MLA_V
    const = lambda i: (0, 0)
    return pl.pallas_call(
        _mla_prep_kernel,
        grid=(t // PREP_TM,),
        in_specs=[
            pl.BlockSpec((PREP_TM, A_COLS), lambda i: (i, 0)),
            pl.BlockSpec((1, MLA_Q_RANK), const),
            pl.BlockSpec((1, MLA_KV_RANK), const),
            pl.BlockSpec((MLA_Q_RANK, hq), const),
            pl.BlockSpec((MLA_KV_RANK, hq), const),
            pl.BlockSpec((MLA_KV_RANK, hv), const),
            pl.BlockSpec((PREP_TM, 3 * LANES), lambda i: (i, 0)),
        ],
        out_specs=[
            pl.BlockSpec((PREP_TM, hq), lambda i: (i, 0)),
            pl.BlockSpec((PREP_TM, hq), lambda i: (i, 0)),
            pl.BlockSpec((PREP_TM, hq), lambda i: (i, 0)),
        ],
        out_shape=[
            jax.ShapeDtypeStruct((t, hq), bf16),
            jax.ShapeDtypeStruct((t, hq), bf16),
            jax.ShapeDtypeStruct((t, hq), bf16),
        ],
        compiler_params=pltpu.CompilerParams(
            dimension_semantics=("arbitrary",), vmem_limit_bytes=VMEM_LIMIT),
        name="mla_prep",
    )(a, gq, gkv, wq, wk, wv, rope)


ATT_TQ = 512
ATT_TK = 256


MLA_TK = 256
DIFF_TK = 512


def _lane_fold(x, op):
    out = x[:, :LANES]
    for u in range(1, x.shape[1] // LANES):
        out = op(out, x[:, u * LANES:(u + 1) * LANES])
    return out


def _flash_streams(streams, seq, tk):
    n_chunks = seq // tk
    state = [None] * len(streams)
    for c in range(n_chunks):
        rows = slice(c * tk, (c + 1) * tk)
        for i, (q, k_ref, k_cols, v_ref, v_cols, bias_fn) in enumerate(streams):
            s = _nt_dot(q, k_ref[rows, k_cols])
            if bias_fn is not None:
                s = s + bias_fn(c)
            m_c = jnp.max(_lane_fold(s, jnp.maximum), axis=-1, keepdims=True)
            if c == 0:
                m_new = m_c
            else:
                m_old, acc_old = state[i]
                m_new = jnp.maximum(m_old, m_c)
            p = jnp.exp2(s - m_new).astype(bf16)
            pv = jnp.dot(p, v_ref[rows, v_cols], preferred_element_type=f32)
            state[i] = (m_new, pv if c == 0 else jnp.exp2(m_old - m_new) * acc_old + pv)
    return [acc[:, :LANES] / acc[:, LANES:LANES + 1] for _, acc in state]


def _mla_attn_kernel(q_ref, k_ref, v_ref, o_ref):
    tq = q_ref.shape[0]
    all_cols = slice(0, 2 * LANES)
    streams = [(q_ref[:, hh * LANES:(hh + 1) * LANES], k_ref, slice(hh * LANES, (hh + 1) * LANES),
                v_ref, all_cols, None) for hh in range(2)]
    o0, o1 = _flash_streams(streams, k_ref.shape[0], MLA_TK)
    lane = lax.broadcasted_iota(jnp.int32, (tq, LANES), 1)
    o_ref[...] = jnp.where(lane < MLA_V, o0, o1).astype(o_ref.dtype)


def _mla_attn(q, k, v, batch, seq):
    t = q.shape[0]
    nq = seq // ATT_TQ
    n_pairs = MLA_HEADS // 2
    return pl.pallas_call(
        _mla_attn_kernel,
        grid=(batch, n_pairs, nq),
        in_specs=[
            pl.BlockSpec((ATT_TQ, 2 * LANES), lambda b, j, i: (b * nq + i, j)),
            pl.BlockSpec((seq, 2 * LANES), lambda b, j, i: (b, j)),
            pl.BlockSpec((seq, 2 * LANES), lambda b, j, i: (b, j)),
        ],
        out_specs=pl.BlockSpec((ATT_TQ, LANES), lambda b, j, i: (b * nq + i, j)),
        out_shape=jax.ShapeDtypeStruct((t, MLA_HEADS * MLA_V), bf16),
        compiler_params=pltpu.CompilerParams(
            dimension_semantics=("arbitrary", "arbitrary", "arbitrary"), vmem_limit_bytes=VMEM_LIMIT),
        name="mla_attn",
    )(q, k, v)


DIFF_TQ = 512
DIFF_SCALE = DIFF_QK ** -0.5
N_BIAS_TILES = 5


def _diff_attn_kernel(lam_ref, q0_ref, q1_ref, k0_ref, k1_ref, v_ref, bias_ref, gsub_ref, o_ref, vaug_ref):
    qi = pl.program_id(2)
    tq = q0_ref.shape[0]
    seq = v_ref.shape[0]

    @pl.when(qi == 0)
    def _():
        for hh in range(2):
            vaug_ref[hh, :, :DIFF_V] = v_ref[:, hh * DIFF_V:(hh + 1) * DIFF_V]
            vaug_ref[hh, :, DIFF_V:] = jnp.ones((seq, DIFF_V), bf16)

    lv = lam_ref[...]
    lam_init = lv[4:5, 0:1]
    lam = (jnp.exp(jnp.sum(lv[0:1] * lv[1:2], axis=-1, keepdims=True))
           - jnp.exp(jnp.sum(lv[2:3] * lv[3:4], axis=-1, keepdims=True)) + lam_init)
    lane = lax.broadcasted_iota(jnp.int32, (tq, LANES), 1)
    q_refs = (q0_ref, q1_ref)
    k_refs = (k0_ref, k1_ref)
    all_cols = slice(0, LANES)
    tiles_per_chunk = DIFF_TK // ATT_TK
    streams = []
    for hh in range(2):
        head_lanes = (lane >= DIFF_QK) == (hh == 1)
        for m in range(2):
            q = q_refs[m][...]
            q = jnp.where(head_lanes, q, jnp.zeros_like(q))

            def bias_fn(c, m=m, hh=hh):
                def tile(r, cc):
                    offset = (c * tiles_per_chunk + cc) - (qi * (tq // ATT_TK) + r)
                    return bias_ref[m, hh, jnp.clip(offset, -2, 2) + 2]
                return jnp.concatenate(
                    [jnp.concatenate([tile(r, cc) for cc in range(tiles_per_chunk)], axis=1)
                     for r in range(tq // ATT_TK)], axis=0)

            streams.append((q, k_refs[m], all_cols, vaug_ref.at[hh], slice(0, 2 * DIFF_V), bias_fn))
    outs = _flash_streams(streams, seq, DIFF_TK)
    for hh in range(2):
        o = outs[2 * hh] - lam * outs[2 * hh + 1]
        o = _rms(o, gsub_ref[...]) * (1.0 - lam_init)
        o_ref[:, hh * DIFF_V:(hh + 1) * DIFF_V] = o.astype(o_ref.dtype)


def _diff_attn(lamvec, qkv, bias_tiles, gsub, batch, seq):
    t = qkv.shape[0]
    nq = seq // DIFF_TQ
    n_pairs = DIFF_HEADS // 2
    return pl.pallas_call(
        _diff_attn_kernel,
        grid=(batch, n_pairs, nq),
        in_specs=[
            pl.BlockSpec((8, DIFF_QK), lambda b, j, i: (0, 0)),
            pl.BlockSpec((DIFF_TQ, LANES), lambda b, j, i: (b * nq + i, FQ_BLK + j)),
            pl.BlockSpec((DIFF_TQ, LANES), lambda b, j, i: (b * nq + i, FQ_BLK + n_pairs + j)),
            pl.BlockSpec((seq, LANES), lambda b, j, i: (b, FK_BLK + j)),
            pl.BlockSpec((seq, LANES), lambda b, j, i: (b, FK_BLK + n_pairs + j)),
            pl.BlockSpec((seq, 2 * DIFF_V), lambda b, j, i: (b, FV_BLK // 2 + j)),
            pl.BlockSpec((2, 2, N_BIAS_TILES, ATT_TK, ATT_TK), lambda b, j, i: (0, j, 0, 0, 0)),
            pl.BlockSpec((1, DIFF_V), lambda b, j, i: (0, 0)),
        ],
        out_specs=pl.BlockSpec((DIFF_TQ, 2 * DIFF_V), lambda b, j, i: (b * nq + i, j)),
        out_shape=jax.ShapeDtypeStruct((t, DIFF_HEADS * DIFF_V), bf16),
        scratch_shapes=[pltpu.VMEM((2, seq, 2 * DIFF_V), bf16)],
        compiler_params=pltpu.CompilerParams(
            dimension_semantics=("arbitrary", "arbitrary", "arbitrary"), vmem_limit_bytes=VMEM_LIMIT),
        name="diff_attn",
    )(lamvec, qkv, qkv, qkv, qkv, qkv, bias_tiles, gsub)


DIL_TQ = 128
DIL_TW = DIL_TQ + 2 * DIL_HALF
DIL_SCALE = DIL_QK ** -0.5
DIL_UNROLL = 8


def _residue_major(src_ref, dst_ref, stage_ref, dil):
    seq, width = src_ref.shape
    sub_len = seq // dil
    for c in range(width // LANES):
        csl = slice(c * LANES, (c + 1) * LANES)
        stage_ref[c] = src_ref[:, csl].astype(f32)
        for r in range(dil):
            dst_ref[r * sub_len:(r + 1) * sub_len, csl] = (
                stage_ref[c, pl.ds(r, sub_len, stride=dil), :].astype(dst_ref.dtype))


def _dil_attn_kernel(q0_ref, q1_ref, q2_ref, k0_ref, k1_ref, k2_ref, v_ref, bm_ref, o_ref,
                     stage_ref, qp_ref, kp_ref, vp_ref, m_ref, l_ref, acc_ref):
    seq = v_ref.shape[0]
    n_tiles = seq // DIL_TQ
    q_refs = (q0_ref, q1_ref, q2_ref)
    k_refs = (k0_ref, k1_ref, k2_ref)
    lane = lax.broadcasted_iota(jnp.int32, (DIL_TQ, LANES), 1)
    col = lax.broadcasted_iota(jnp.int32, (DIL_TQ, DIL_TW), 1)
    for g, (_, dil) in enumerate(DIL_PATTERNS):
        if dil == 1:
            q_src, k_src, v_src = q_refs[g], k_refs[g], v_ref
        else:
            _residue_major(q_refs[g], qp_ref, stage_ref, dil)
            _residue_major(k_refs[g], kp_ref, stage_ref, dil)
            _residue_major(v_ref, vp_ref, stage_ref, dil)
            q_src, k_src, v_src = qp_ref, kp_ref, vp_ref
        tiles_per_residue = n_tiles // dil

        def tile(n, carry, g=g, dil=dil, q_src=q_src, k_src=k_src, v_src=v_src,
                 tiles_per_residue=tiles_per_residue):
            base = pl.multiple_of(n * DIL_TQ, DIL_TQ)
            t_in = n % tiles_per_residue
            lo = pl.multiple_of(jnp.maximum(base - DIL_HALF, 0), DIL_HALF)
            hi = pl.multiple_of(jnp.minimum(base + DIL_TQ, seq - DIL_HALF), DIL_HALF)
            c_lo = jnp.where(t_in == 0, DIL_HALF, 0)
            c_hi = jnp.where(t_in == tiles_per_residue - 1, DIL_HALF + DIL_TQ, DIL_TW)
            valid = (col >= c_lo) & (col < c_hi)
            if dil == 1:
                rows = pl.ds(base, DIL_TQ)
            else:
                token0 = t_in * (DIL_TQ * dil) + n // tiles_per_residue
                rows = pl.ds(token0, DIL_TQ, stride=dil)
            qt = q_src[pl.ds(base, DIL_TQ), :]
            kw = jnp.concatenate([k_src[pl.ds(lo, DIL_HALF), :], k_src[pl.ds(base, DIL_TQ), :],
                                  k_src[pl.ds(hi, DIL_HALF), :]], axis=0)
            vw = jnp.concatenate([v_src[pl.ds(lo, DIL_HALF), :], v_src[pl.ds(base, DIL_TQ), :],
                                  v_src[pl.ds(hi, DIL_HALF), :]], axis=0)
            for hh in range(2):
                qh = jnp.where((lane >= DIL_QK) == (hh == 1), qt * DIL_SCALE, jnp.zeros_like(qt))
                hsl = slice(hh * DIL_V, (hh + 1) * DIL_V)
                s = _nt_dot(qh, kw) + bm_ref[g, hh]
                s = jnp.where(valid, s, NEG)
                m_t = jnp.max(s, axis=-1, keepdims=True)
                p = jnp.exp(s - m_t)
                l_t = jnp.broadcast_to(jnp.sum(p, axis=-1, keepdims=True), (DIL_TQ, DIL_V))
                u_t = jnp.dot(p.astype(bf16), vw[:, hsl], preferred_element_type=f32)
                m_t = jnp.broadcast_to(m_t, (DIL_TQ, DIL_V))
                if g == 0:
                    m_ref[hh, rows, :] = m_t
                    l_ref[hh, rows, :] = l_t
                    acc_ref[hh, rows, :] = u_t
                else:
                    m_old = m_ref[hh, rows, :]
                    m_new = jnp.maximum(m_old, m_t)
                    a_old = jnp.exp(m_old - m_new)
                    a_t = jnp.exp(m_t - m_new)
                    m_ref[hh, rows, :] = m_new
                    l_ref[hh, rows, :] = a_old * l_ref[hh, rows, :] + a_t * l_t
                    acc_ref[hh, rows, :] = a_old * acc_ref[hh, rows, :] + a_t * u_t
            return carry

        def tile_group(i, carry, tile=tile):
            for u in range(DIL_UNROLL):
                tile(i * DIL_UNROLL + u, carry)
            return carry

        lax.fori_loop(0, n_tiles // DIL_UNROLL, tile_group, 0)
    for hh in range(2):
        o_ref[:, hh * DIL_V:(hh + 1) * DIL_V] = (acc_ref[hh] / l_ref[hh]).astype(o_ref.dtype)


def _dil_attn(qkv, biasmask, batch, seq):
    t = qkv.shape[0]
    n_pairs = DIL_HEADS // 2
    pair_w = 2 * DIL_V

    def qk_spec(blk0, g):
        return pl.BlockSpec((seq, LANES), lambda b, j: (b, blk0 + g * n_pairs + j))

    state = pltpu.VMEM((2, seq, DIL_V), f32)
    return pl.pallas_call(
        _dil_attn_kernel,
        grid=(batch, n_pairs),
        in_specs=[qk_spec(DQ_BLK, g) for g in range(DIL_GROUPS)]
        + [qk_spec(DK_BLK, g) for g in range(DIL_GROUPS)]
        + [pl.BlockSpec((seq, pair_w), lambda b, j: (b, DV_BLK // 2 + j)),
           pl.BlockSpec((DIL_GROUPS, 2, DIL_TQ, DIL_TW), lambda b, j: (0, j, 0, 0))],
        out_specs=pl.BlockSpec((seq, pair_w), lambda b, j: (b, j)),
        out_shape=jax.ShapeDtypeStruct((t, DIL_HEADS * DIL_V), bf16),
        scratch_shapes=[
            pltpu.VMEM((2, seq, LANES), f32),
            pltpu.VMEM((seq, LANES), bf16),
            pltpu.VMEM((seq, LANES), bf16),
            pltpu.VMEM((seq, pair_w), bf16),
            state, state, state,
        ],
        compiler_params=pltpu.CompilerParams(
            dimension_semantics=("arbitrary", "arbitrary"), vmem_limit_bytes=VMEM_LIMIT_DIL),
        name="dil_attn",
    )(qkv, qkv, qkv, qkv, qkv, qkv, qkv, biasmask)


MERGE_TM = 512


def _merge_kernel(x_ref, oa_ref, ob_ref, oc_ref, gz0_ref, gz1_ref, gz2_ref, wb_ref, wo_ref, g_ref, out_ref):
    branches = (oa_ref[...], ob_ref[...], oc_ref[...])
    gate_refs = (gz0_ref, gz1_ref, gz2_ref)
    merged = None
    for n in range(N_BRANCH):
        y = jnp.dot(branches[n], wb_ref[n], preferred_element_type=f32)
        y = jax.nn.sigmoid(gate_refs[n][...].astype(f32)) * y
        merged = y if merged is None else merged + y
    y = jnp.dot(merged.astype(bf16), wo_ref[...], preferred_element_type=f32)
    out_ref[...] = x_ref[...] + _rms(y, g_ref[...])


def _merge(x, oa, ob, oc, z, wb, wo, g):
    t = x.shape[0]
    row = lambda i: (i, 0)
    bw = pl.BlockSpec((MERGE_TM, BRANCH_W), row)
    gate_specs = [pl.BlockSpec((MERGE_TM, D_MODEL), lambda i, n=n: (i, GATE_BLK + n)) for n in range(N_BRANCH)]
    return pl.pallas_call(
        _merge_kernel,
        grid=(t // MERGE_TM,),
        in_specs=[
            pl.BlockSpec((MERGE_TM, D_MODEL), row),
            bw, bw, bw,
            *gate_specs,
            pl.BlockSpec((N_BRANCH, BRANCH_W, D_MODEL), lambda i: (0, 0, 0)),
            pl.BlockSpec((D_MODEL, D_MODEL), lambda i: (0, 0)),
            pl.BlockSpec((1, D_MODEL), lambda i: (0, 0)),
        ],
        out_specs=pl.BlockSpec((MERGE_TM, D_MODEL), row),
        out_shape=jax.ShapeDtypeStruct((t, D_MODEL), f32),
        compiler_params=pltpu.CompilerParams(
            dimension_semantics=("arbitrary",), vmem_limit_bytes=VMEM_LIMIT),
        name="merge",
    )(x, oa, ob, oc, z, z, z, wb, wo, g)


MLP_TM = 1024
MLP_TF = 1024


def _mlp_kernel(x_ref, gpre_ref, wup_ref, wdn_ref, gpost_ref, out_ref, h_ref, acc_ref):
    k = pl.program_id(1)

    @pl.when(k == 0)
    def _():
        h_ref[...] = _rms(x_ref[...], gpre_ref[...]).astype(bf16)
        acc_ref[...] = jnp.zeros_like(acc_ref)

    u = jnp.maximum(jnp.dot(h_ref[...], wup_ref[...], preferred_element_type=f32), 0.0)
    acc_ref[...] += jnp.dot((u * u).astype(bf16), wdn_ref[...], preferred_element_type=f32)

    @pl.when(k == pl.num_programs(1) - 1)
    def _():
        out_ref[...] = x_ref[...] + _rms(acc_ref[...], gpost_ref[...])


def _mlp(x, gpre, wup, wdn, gpost):
    t = x.shape[0]
    return pl.pallas_call(
        _mlp_kernel,
        grid=(t // MLP_TM, D_FF // MLP_TF),
        in_specs=[
            pl.BlockSpec((MLP_TM, D_MODEL), lambda i, k: (i, 0)),
            pl.BlockSpec((1, D_MODEL), lambda i, k: (0, 0)),
            pl.BlockSpec((D_MODEL, MLP_TF), lambda i, k: (0, k)),
            pl.BlockSpec((MLP_TF, D_MODEL), lambda i, k: (k, 0)),
            pl.BlockSpec((1, D_MODEL), lambda i, k: (0, 0)),
        ],
        out_specs=pl.BlockSpec((MLP_TM, D_MODEL), lambda i, k: (i, 0)),
        out_shape=jax.ShapeDtypeStruct((t, D_MODEL), f32),
        scratch_shapes=[pltpu.VMEM((MLP_TM, D_MODEL), bf16), pltpu.VMEM((MLP_TM, D_MODEL), f32)],
        compiler_params=pltpu.CompilerParams(
            dimension_semantics=("arbitrary", "arbitrary"), vmem_limit_bytes=VMEM_LIMIT),
        name="mlp",
    )(x, gpre, wup, wdn, gpost)


def _rel_bucket(rel):
    nb = REL_BUCKETS // 2
    max_exact = nb // 2
    ret = jnp.where(rel > 0, nb, 0)
    n = jnp.abs(rel)
    large = max_exact + (jnp.log(jnp.maximum(n, 1).astype(f32) / max_exact)
                         / math.log(REL_MAX_DIST / max_exact) * (nb - max_exact)).astype(jnp.int32)
    large = jnp.minimum(large, nb - 1)
    return ret + jnp.where(n < max_exact, n, large)


def _bias_lookup(tab, bucket):
    one_hot = jax.nn.one_hot(bucket, REL_BUCKETS, dtype=f32)
    return jnp.einsum('...b,bc->...c', one_hot, tab.astype(f32), precision=lax.Precision.HIGHEST)


def _pad_in_weights(w_in):
    sizes = (MLA_Q_RANK, MLA_KV_RANK, MLA_ROPE)
    c_q = w_in[..., :sizes[0]]
    c_kv = w_in[..., sizes[0]:sizes[0] + sizes[1]]
    k_r = w_in[..., sizes[0] + sizes[1]:sum(sizes)]
    rest = w_in[..., sum(sizes):]
    fq0 = (FQ_BLK - DQ_BLK) * LANES
    fq1 = (FK_BLK - DQ_BLK) * LANES
    z = lambda n: jnp.zeros(w_in.shape[:-1] + (n,), bf16)
    pieces = [c_q, c_kv, z(MLA_NOPE), k_r, z(LANES - MLA_NOPE - MLA_ROPE),
              rest[..., :fq0], rest[..., fq0:fq1] * (DIFF_SCALE * LOG2E), rest[..., fq1:]]
    return jnp.concatenate([p.astype(bf16) for p in pieces], axis=-1)


def _pad_mla_weights(w_uq, w_ukv):
    depth = w_uq.shape[0]
    wq = jnp.pad(w_uq, ((0, 0), (0, 0), (0, 0), (0, LANES - MLA_NOPE - MLA_ROPE)))
    wk = jnp.pad(w_ukv[..., :MLA_NOPE], ((0, 0), (0, 0), (0, 0), (0, LANES - MLA_NOPE)))
    wv = w_ukv[..., MLA_NOPE:]
    return (wq.reshape(depth, MLA_Q_RANK, MLA_HEADS * LANES).astype(bf16),
            wk.reshape(depth, MLA_KV_RANK, MLA_HEADS * LANES).astype(bf16),
            wv.reshape(depth, MLA_KV_RANK, MLA_HEADS * MLA_V).astype(bf16))


def _rope_tables(positions):
    inv = ROPE_BASE ** (-jnp.arange(ROPE_HALF, dtype=f32) / ROPE_HALF)
    ang = positions.reshape(-1).astype(f32)[:, None] * inv
    cos, sin = jnp.cos(ang), jnp.sin(ang)
    t = ang.shape[0]
    one = jnp.ones((t, MLA_NOPE), f32)
    z = lambda n: jnp.zeros((t, n), f32)
    tail = LANES - MLA_NOPE - MLA_ROPE
    c = jnp.concatenate([one, cos, cos, z(tail)], axis=-1)
    sa = jnp.concatenate([z(MLA_NOPE + ROPE_HALF), sin, z(tail)], axis=-1)
    sb = jnp.concatenate([z(MLA_NOPE), -sin, z(ROPE_HALF + tail)], axis=-1)
    return jnp.concatenate([c, sa, sb], axis=-1)


def _diff_bias_tiles(rel_bias):
    tab = rel_bias[:, DIL_BIAS_COLS:]
    i = jnp.arange(ATT_TK)[:, None]
    j = jnp.arange(ATT_TK)[None, :]
    offs = jnp.array([-2, -1, 0, 1, 2])[:, None, None] * ATT_TK
    bucket = _rel_bucket(offs + j - i)
    tiles = jnp.transpose(_bias_lookup(tab, bucket), (3, 0, 1, 2)) * LOG2E
    return tiles.reshape(2, DIFF_HEADS, N_BIAS_TILES, ATT_TK, ATT_TK)


def _dil_bias_masks(rel_bias):
    i = jnp.arange(DIL_TQ)[:, None]
    c = jnp.arange(DIL_TW)[None, :]
    rel = c - DIL_HALF - i
    out = []
    for g, (_, dil) in enumerate(DIL_PATTERNS):
        tab = rel_bias[:, g * DIL_HEADS:(g + 1) * DIL_HEADS]
        b = jnp.transpose(_bias_lookup(tab, _rel_bucket(rel * dil)), (2, 0, 1))
        out.append(jnp.where((jnp.abs(rel) <= DIL_HALF)[None], b, NEG))
    return jnp.stack(out, axis=0)


def kernel(x, positions, rel_bias, g_mix_pre, w_in, g_q, w_uq, g_kv, w_ukv, lam_q1, lam_k1, lam_q2, lam_k2,
           g_diff_sub, w_branch, w_out, g_mix_post, g_mlp_pre, w_up, w_down, g_mlp_post):
    batch, seq, d = x.shape
    depth = w_in.shape[0]
    assert d == D_MODEL and seq % 1024 == 0 and seq // DIL_PATTERNS[-1][1] >= 2 * DIL_TQ

    w_in_p = _pad_in_weights(w_in)
    wq_p, wk_p, wv_p = _pad_mla_weights(w_uq, w_ukv)
    wb = w_branch.astype(bf16)
    wo = w_out.astype(bf16)
    wup = w_up.astype(bf16)
    wdn = w_down.astype(bf16)
    rope = _rope_tables(positions)
    diff_bias = _diff_bias_tiles(rel_bias)
    dil_bias = _dil_bias_masks(rel_bias)

    lam_init = jnp.array([0.8 - 0.6 * math.exp(-0.3 * l) for l in range(depth)], f32)
    lam_row = jnp.zeros((depth, 1, DIFF_QK), f32).at[:, 0, 0].set(lam_init)
    lamvec = jnp.concatenate([lam_q1[:, None], lam_k1[:, None], lam_q2[:, None], lam_k2[:, None],
                              lam_row, jnp.zeros((depth, 3, DIFF_QK), f32)], axis=1).astype(f32)

    row = lambda v: v.reshape(1, -1).astype(f32)
    xt = x.reshape(batch * seq, d)
    for l in range(depth):
        z = _in_proj(xt, row(g_mix_pre[l]), w_in_p[l])
        q, k, v = _mla_prep(z, row(g_q[l]), row(g_kv[l]), wq_p[l], wk_p[l], wv_p[l], rope)
        o_a = _mla_attn(q, k, v, batch, seq)
        o_b = _dil_attn(z, dil_bias, batch, seq)
        o_c = _diff_attn(lamvec[l], z, diff_bias, row(g_diff_sub[l]), batch, seq)
        xt = _merge(xt, o_a, o_b, o_c, z, wb[l], wo[l], row(g_mix_post[l]))
        xt = _mlp(xt, row(g_mlp_pre[l]), wup[l], wdn[l], row(g_mlp_post[l]))
    return xt.reshape(batch, seq, d)
```

```python
import functools
import math

import jax
import jax.numpy as jnp
import numpy as np
from jax import lax
from jax.experimental import pallas as pl
from jax.experimental.pallas import tpu as pltpu

f32 = jnp.float32
bf16 = jnp.bfloat16

D_MODEL = 1024
MLA_HEADS = 8
MLA_Q_RANK = 256
MLA_KV_RANK = 128
MLA_NOPE = 64
MLA_ROPE = 32
MLA_V = 64
ROPE_BASE = 10000.0
DIL_PATTERNS = ((128, 1), (512, 4), (2048, 16))
DIL_GROUPS = 3
DIL_HEADS = 4
DIL_QK = 64
DIL_V = 128
DIL_HALF = 64
DIFF_HEADS = 4
DIFF_QK = 64
DIFF_V = 128
REL_BUCKETS = 32
REL_MAX_DIST = 128
DIL_BIAS_COLS = DIL_GROUPS * DIL_HEADS
N_BRANCH = 3
BRANCH_W = 512
D_FF = 4 * D_MODEL
EPS = 1e-6
NEG = -1e30

LANES = 128
A_COLS = 512
QKV_COLS = 3584
GATE_COLS = N_BRANCH * D_MODEL
IN_PAD_COLS = A_COLS + QKV_COLS + GATE_COLS
DQ_BLK, DK_BLK, DV_BLK, FQ_BLK, FK_BLK, FV_BLK = (A_COLS // LANES + o for o in (0, 6, 12, 16, 20, 24))
GATE_BLK = (A_COLS + QKV_COLS) // D_MODEL

VMEM_LIMIT = 48 * 1024 * 1024
VMEM_LIMIT_DIL = 56 * 1024 * 1024


def _rms(x, g):
    return x * lax.rsqrt(jnp.mean(x * x, axis=-1, keepdims=True) + EPS) * g


def _nt_dot(a, b):
    return lax.dot_general(a, b, (((1,), (1,)), ((), ())), preferred_element_type=f32)


IN_TM = 1024
IN_TN = 1792


def _in_proj_kernel(x_ref, g_ref, w_ref, z_ref, h_ref):
    @pl.when(pl.program_id(1) == 0)
    def _():
        h_ref[...] = _rms(x_ref[...], g_ref[...]).astype(bf16)

    z_ref[...] = jnp.dot(h_ref[...], w_ref[...], preferred_element_type=f32).astype(z_ref.dtype)


def _in_proj(x, g, w):
    t = x.shape[0]
    return pl.pallas_call(
        _in_proj_kernel,
        grid=(t // IN_TM, IN_PAD_COLS // IN_TN),
        in_specs=[
            pl.BlockSpec((IN_TM, D_MODEL), lambda i, j: (i, 0)),
            pl.BlockSpec((1, D_MODEL), lambda i, j: (0, 0)),
            pl.BlockSpec((D_MODEL, IN_TN), lambda i, j: (0, j)),
        ],
        out_specs=pl.BlockSpec((IN_TM, IN_TN), lambda i, j: (i, j)),
        out_shape=jax.ShapeDtypeStruct((t, IN_PAD_COLS), bf16),
        scratch_shapes=[pltpu.VMEM((IN_TM, D_MODEL), bf16)],
        compiler_params=pltpu.CompilerParams(
            dimension_semantics=("arbitrary", "arbitrary"), vmem_limit_bytes=VMEM_LIMIT),
        name="in_proj",
    )(x, g, w)


PREP_TM = 512
MLA_SCALE = (MLA_NOPE + MLA_ROPE) ** -0.5
LOG2E = math.log2(math.e)
ROPE_HALF = MLA_ROPE // 2


def _rope_lanes(x, rope):
    c = rope[:, 0:LANES]
    sa = rope[:, LANES:2 * LANES]
    sb = rope[:, 2 * LANES:3 * LANES]
    return (x * c + pltpu.roll(x, ROPE_HALF, 1) * sa + pltpu.roll(x, LANES - ROPE_HALF, 1) * sb)


def _mla_prep_kernel(a_ref, gq_ref, gkv_ref, wq_ref, wk_ref, wv_ref, rope_ref, q_ref, k_ref, v_ref):
    a = a_ref[...].astype(f32)
    rope = rope_ref[...]
    cq = _rms(a[:, :MLA_Q_RANK], gq_ref[...]).astype(bf16)
    ckv = _rms(a[:, MLA_Q_RANK:MLA_Q_RANK + MLA_KV_RANK], gkv_ref[...]).astype(bf16)
    k_rope = _rope_lanes(a[:, A_COLS - LANES:], rope)
    qf = jnp.dot(cq, wq_ref[...], preferred_element_type=f32)
    kf = jnp.dot(ckv, wk_ref[...], preferred_element_type=f32)
    vf = jnp.dot(ckv, wv_ref[...], preferred_element_type=f32).astype(bf16)
    ones = jnp.ones((vf.shape[0], LANES), bf16)
    lane = lax.broadcasted_iota(jnp.int32, ones.shape, 1)
    for j in range(MLA_HEADS // 2):
        pair = vf[:, j * LANES:(j + 1) * LANES]
        v_ref[:, 2 * j * LANES:(2 * j + 1) * LANES] = jnp.where(lane < MLA_V, pair, ones)
        v_ref[:, (2 * j + 1) * LANES:(2 * j + 2) * LANES] = jnp.where(lane >= MLA_V, pair, ones)
    for h in range(MLA_HEADS):
        sl = slice(h * LANES, (h + 1) * LANES)
        q_ref[:, sl] = (_rope_lanes(qf[:, sl], rope) * (MLA_SCALE * LOG2E)).astype(bf16)
        k_ref[:, sl] = (kf[:, sl] + k_rope).astype(bf16)


def _mla_prep(a, gq, gkv, wq, wk, wv, rope):
    t = a.shape[0]
    hq = MLA_HEADS * LANES
    hv = MLA_HEADS * MLA_V
    const = lambda i: (0, 0)
    return pl.pallas_call(
        _mla_prep_kernel,
        grid=(t // PREP_TM,),
        in_specs=[
            pl.BlockSpec((PREP_TM, A_COLS), lambda i: (i, 0)),
            pl.BlockSpec((1, MLA_Q_RANK), const),
            pl.BlockSpec((1, MLA_KV_RANK), const),
            pl.BlockSpec((MLA_Q_RANK, hq), const),
            pl.BlockSpec((MLA_KV_RANK, hq), const),
            pl.BlockSpec((MLA_KV_RANK, hv), const),
            pl.BlockSpec((PREP_TM, 3 * LANES), lambda i: (i, 0)),
        ],
        out_specs=[
            pl.BlockSpec((PREP_TM, hq), lambda i: (i, 0)),
            pl.BlockSpec((PREP_TM, hq), lambda i: (i, 0)),
            pl.BlockSpec((PREP_TM, hq), lambda i: (i, 0)),
        ],
        out_shape=[
            jax.ShapeDtypeStruct((t, hq), bf16),
            jax.ShapeDtypeStruct((t, hq), bf16),
            jax.ShapeDtypeStruct((t, hq), bf16),
        ],
        compiler_params=pltpu.CompilerParams(
            dimension_semantics=("arbitrary",), vmem_limit_bytes=VMEM_LIMIT),
        name="mla_prep",
    )(a, gq, gkv, wq, wk, wv, rope)


ATT_TQ = 512
ATT_TK = 256


MLA_TK = 256
DIFF_TK = 512


def _lane_fold(x, op):
    out = x[:, :LANES]
    for u in range(1, x.shape[1] // LANES):
        out = op(out, x[:, u * LANES:(u + 1) * LANES])
    return out


def _flash_streams(streams, seq, tk):
    n_chunks = seq // tk
    state = [None] * len(streams)
    for c in range(n_chunks):
        rows = slice(c * tk, (c + 1) * tk)
        for i, (q, k_ref, k_cols, v_ref, v_cols, bias_fn) in enumerate(streams):
            s = _nt_dot(q, k_ref[rows, k_cols])
            if bias_fn is not None:
                s = s + bias_fn(c)
            m_c = jnp.max(_lane_fold(s, jnp.maximum), axis=-1, keepdims=True)
            if c == 0:
                m_new = m_c
            else:
                m_old, acc_old = state[i]
                m_new = jnp.maximum(m_old, m_c)
            p = jnp.exp2(s - m_new).astype(bf16)
            pv = jnp.dot(p, v_ref[rows, v_cols], preferred_element_type=f32)
            state[i] = (m_new, pv if c == 0 else jnp.exp2(m_old - m_new) * acc_old + pv)
    return [acc for _, acc in state]


def _mla_attn_kernel(q_ref, k_ref, v_ref, o_ref):
    tq = q_ref.shape[0]
    streams = [(q_ref[:, hh * LANES:(hh + 1) * LANES], k_ref, slice(hh * LANES, (hh + 1) * LANES),
                v_ref, slice(hh * LANES, (hh + 1) * LANES), None) for hh in range(2)]
    acc_a, acc_b = _flash_streams(streams, k_ref.shape[0], MLA_TK)
    lane = lax.broadcasted_iota(jnp.int32, (tq, LANES), 1)
    o_ref[...] = jnp.where(lane < MLA_V, acc_a / acc_a[:, MLA_V:MLA_V + 1],
                           acc_b / acc_b[:, 0:1]).astype(o_ref.dtype)


def _mla_attn(q, k, v, batch, seq):
    t = q.shape[0]
    nq = seq // ATT_TQ
    n_pairs = MLA_HEADS // 2
    return pl.pallas_call(
        _mla_attn_kernel,
        grid=(batch, n_pairs, nq),
        in_specs=[
            pl.BlockSpec((ATT_TQ, 2 * LANES), lambda b, j, i: (b * nq + i, j)),
            pl.BlockSpec((seq, 2 * LANES), lambda b, j, i: (b, j)),
            pl.BlockSpec((seq, 2 * LANES), lambda b, j, i: (b, j)),
        ],
        out_specs=pl.BlockSpec((ATT_TQ, LANES), lambda b, j, i: (b * nq + i, j)),
        out_shape=jax.ShapeDtypeStruct((t, MLA_HEADS * MLA_V), bf16),
        compiler_params=pltpu.CompilerParams(
            dimension_semantics=("arbitrary", "arbitrary", "arbitrary"), vmem_limit_bytes=VMEM_LIMIT),
        name="mla_attn",
    )(q, k, v)


DIFF_TQ = 512
DIFF_SCALE = DIFF_QK ** -0.5
N_BIAS_TILES = 5


def _diff_attn_kernel(lam_ref, q0_ref, q1_ref, k0_ref, k1_ref, v_ref, bias_ref, gsub_ref, o_ref, vaug_ref):
    qi = pl.program_id(2)
    tq = q0_ref.shape[0]
    seq = v_ref.shape[0]

    @pl.when(qi == 0)
    def _():
        for hh in range(2):
            vaug_ref[hh, :, :DIFF_V] = v_ref[:, hh * DIFF_V:(hh + 1) * DIFF_V]
            vaug_ref[hh, :, DIFF_V:] = jnp.ones((seq, DIFF_V), bf16)

    lv = lam_ref[...]
    lam_init = lv[4:5, 0:1]
    lam = (jnp.exp(jnp.sum(lv[0:1] * lv[1:2], axis=-1, keepdims=True))
           - jnp.exp(jnp.sum(lv[2:3] * lv[3:4], axis=-1, keepdims=True)) + lam_init)
    lane = lax.broadcasted_iota(jnp.int32, (tq, LANES), 1)
    q_refs = (q0_ref, q1_ref)
    k_refs = (k0_ref, k1_ref)
    all_cols = slice(0, LANES)
    tiles_per_chunk = DIFF_TK // ATT_TK
    streams = []
    for hh in range(2):
        head_lanes = (lane >= DIFF_QK) == (hh == 1)
        for m in range(2):
            q = q_refs[m][...]
            q = jnp.where(head_lanes, q, jnp.zeros_like(q))

            def bias_fn(c, m=m, hh=hh):
                def tile(r, cc):
                    offset = (c * tiles_per_chunk + cc) - (qi * (tq // ATT_TK) + r)
                    return bias_ref[m, hh, jnp.clip(offset, -2, 2) + 2]
                return jnp.concatenate(
                    [jnp.concatenate([tile(r, cc) for cc in range(tiles_per_chunk)], axis=1)
                     for r in range(tq // ATT_TK)], axis=0)

            streams.append((q, k_refs[m], all_cols, vaug_ref.at[hh], slice(0, 2 * DIFF_V), bias_fn))
    outs = [acc[:, :DIFF_V] / acc[:, DIFF_V:DIFF_V + 1] for acc in _flash_streams(streams, seq, DIFF_TK)]
    for hh in range(2):
        o = outs[2 * hh] - lam * outs[2 * hh + 1]
        o = _rms(o, gsub_ref[...]) * (1.0 - lam_init)
        o_ref[:, hh * DIFF_V:(hh + 1) * DIFF_V] = o.astype(o_ref.dtype)


def _diff_attn(lamvec, qkv, bias_tiles, gsub, batch, seq):
    t = qkv.shape[0]
    nq = seq // DIFF_TQ
    n_pairs = DIFF_HEADS // 2
    return pl.pallas_call(
        _diff_attn_kernel,
        grid=(batch, n_pairs, nq),
        in_specs=[
            pl.BlockSpec((8, DIFF_QK), lambda b, j, i: (0, 0)),
            pl.BlockSpec((DIFF_TQ, LANES), lambda b, j, i: (b * nq + i, FQ_BLK + j)),
            pl.BlockSpec((DIFF_TQ, LANES), lambda b, j, i: (b * nq + i, FQ_BLK + n_pairs + j)),
            pl.BlockSpec((seq, LANES), lambda b, j, i: (b, FK_BLK + j)),
            pl.BlockSpec((seq, LANES), lambda b, j, i: (b, FK_BLK + n_pairs + j)),
            pl.BlockSpec((seq, 2 * DIFF_V), lambda b, j, i: (b, FV_BLK // 2 + j)),
            pl.BlockSpec((2, 2, N_BIAS_TILES, ATT_TK, ATT_TK), lambda b, j, i: (0, j, 0, 0, 0)),
            pl.BlockSpec((1, DIFF_V), lambda b, j, i: (0, 0)),
        ],
        out_specs=pl.BlockSpec((DIFF_TQ, 2 * DIFF_V), lambda b, j, i: (b * nq + i, j)),
        out_shape=jax.ShapeDtypeStruct((t, DIFF_HEADS * DIFF_V), bf16),
        scratch_shapes=[pltpu.VMEM((2, seq, 2 * DIFF_V), bf16)],
        compiler_params=pltpu.CompilerParams(
            dimension_semantics=("arbitrary", "arbitrary", "arbitrary"), vmem_limit_bytes=VMEM_LIMIT),
        name="diff_attn",
    )(lamvec, qkv, qkv, qkv, qkv, qkv, bias_tiles, gsub)


DIL_TQ = 128
DIL_TW = DIL_TQ + 2 * DIL_HALF
DIL_SCALE = DIL_QK ** -0.5
DIL_UNROLL = 8


def _residue_major(src_ref, dst_ref, stage_ref, dil):
    seq, width = src_ref.shape
    sub_len = seq // dil
    for c in range(width // LANES):
        csl = slice(c * LANES, (c + 1) * LANES)
        stage_ref[c] = src_ref[:, csl].astype(f32)
        for r in range(dil):
            dst_ref[r * sub_len:(r + 1) * sub_len, csl] = (
                stage_ref[c, pl.ds(r, sub_len, stride=dil), :].astype(dst_ref.dtype))


def _dil_attn_kernel(q0_ref, q1_ref, q2_ref, k0_ref, k1_ref, k2_ref, v_ref, bm_ref, o_ref,
                     stage_ref, qp_ref, kp_ref, vp_ref, m_ref, l_ref, acc_ref):
    seq = v_ref.shape[0]
    n_tiles = seq // DIL_TQ
    q_refs = (q0_ref, q1_ref, q2_ref)
    k_refs = (k0_ref, k1_ref, k2_ref)
    lane = lax.broadcasted_iota(jnp.int32, (DIL_TQ, LANES), 1)
    col = lax.broadcasted_iota(jnp.int32, (DIL_TQ, DIL_TW), 1)
    for g, (_, dil) in enumerate(DIL_PATTERNS):
        if dil == 1:
            q_src, k_src, v_src = q_refs[g], k_refs[g], v_ref
        else:
            _residue_major(q_refs[g], qp_ref, stage_ref, dil)
            _residue_major(k_refs[g], kp_ref, stage_ref, dil)
            _residue_major(v_ref, vp_ref, stage_ref, dil)
            q_src, k_src, v_src = qp_ref, kp_ref, vp_ref
        tiles_per_residue = n_tiles // dil

        def tile(n, carry, g=g, dil=dil, q_src=q_src, k_src=k_src, v_src=v_src,
                 tiles_per_residue=tiles_per_residue):
            base = pl.multiple_of(n * DIL_TQ, DIL_TQ)
            t_in = n % tiles_per_residue
            lo = pl.multiple_of(jnp.maximum(base - DIL_HALF, 0), DIL_HALF)
            hi = pl.multiple_of(jnp.minimum(base + DIL_TQ, seq - DIL_HALF), DIL_HALF)
            c_lo = jnp.where(t_in == 0, DIL_HALF, 0)
            c_hi = jnp.where(t_in == tiles_per_residue - 1, DIL_HALF + DIL_TQ, DIL_TW)
            valid = (col >= c_lo) & (col < c_hi)
            if dil == 1:
                rows = pl.ds(base, DIL_TQ)
            else:
                token0 = t_in * (DIL_TQ * dil) + n // tiles_per_residue
                rows = pl.ds(token0, DIL_TQ, stride=dil)
            qt = q_src[pl.ds(base, DIL_TQ), :]
            kw = jnp.concatenate([k_src[pl.ds(lo, DIL_HALF), :], k_src[pl.ds(base, DIL_TQ), :],
                                  k_src[pl.ds(hi, DIL_HALF), :]], axis=0)
            vw = jnp.concatenate([v_src[pl.ds(lo, DIL_HALF), :], v_src[pl.ds(base, DIL_TQ), :],
                                  v_src[pl.ds(hi, DIL_HALF), :]], axis=0)
            for hh in range(2):
                qh = jnp.where((lane >= DIL_QK) == (hh == 1), qt * DIL_SCALE, jnp.zeros_like(qt))
                hsl = slice(hh * DIL_V, (hh + 1) * DIL_V)
                s = _nt_dot(qh, kw) + bm_ref[g, hh]
                s = jnp.where(valid, s, NEG)
                m_t = jnp.max(s, axis=-1, keepdims=True)
                p = jnp.exp(s - m_t)
                l_t = jnp.broadcast_to(jnp.sum(p, axis=-1, keepdims=True), (DIL_TQ, DIL_V))
                u_t = jnp.dot(p.astype(bf16), vw[:, hsl], preferred_element_type=f32)
                m_t = jnp.broadcast_to(m_t, (DIL_TQ, DIL_V))
                if g == 0:
                    m_ref[hh, rows, :] = m_t
                    l_ref[hh, rows, :] = l_t
                    acc_ref[hh, rows, :] = u_t
                else:
                    m_old = m_ref[hh, rows, :]
                    m_new = jnp.maximum(m_old, m_t)
                    a_old = jnp.exp(m_old - m_new)
                    a_t = jnp.exp(m_t - m_new)
                    m_ref[hh, rows, :] = m_new
                    l_ref[hh, rows, :] = a_old * l_ref[hh, rows, :] + a_t * l_t
                    acc_ref[hh, rows, :] = a_old * acc_ref[hh, rows, :] + a_t * u_t
            return carry

        def tile_group(i, carry, tile=tile):
            for u in range(DIL_UNROLL):
                tile(i * DIL_UNROLL + u, carry)
            return carry

        lax.fori_loop(0, n_tiles // DIL_UNROLL, tile_group, 0)
    for hh in range(2):
        o_ref[:, hh * DIL_V:(hh + 1) * DIL_V] = (acc_ref[hh] / l_ref[hh]).astype(o_ref.dtype)


def _dil_attn(qkv, biasmask, batch, seq):
    t = qkv.shape[0]
    n_pairs = DIL_HEADS // 2
    pair_w = 2 * DIL_V

    def qk_spec(blk0, g):
        return pl.BlockSpec((seq, LANES), lambda b, j: (b, blk0 + g * n_pairs + j))

    state = pltpu.VMEM((2, seq, DIL_V), f32)
    return pl.pallas_call(
        _dil_attn_kernel,
        grid=(batch, n_pairs),
        in_specs=[qk_spec(DQ_BLK, g) for g in range(DIL_GROUPS)]
        + [qk_spec(DK_BLK, g) for g in range(DIL_GROUPS)]
        + [pl.BlockSpec((seq, pair_w), lambda b, j: (b, DV_BLK // 2 + j)),
           pl.BlockSpec((DIL_GROUPS, 2, DIL_TQ, DIL_TW), lambda b, j: (0, j, 0, 0))],
        out_specs=pl.BlockSpec((seq, pair_w), lambda b, j: (b, j)),
        out_shape=jax.ShapeDtypeStruct((t, DIL_HEADS * DIL_V), bf16),
        scratch_shapes=[
            pltpu.VMEM((2, seq, LANES), f32),
            pltpu.VMEM((seq, LANES), bf16),
            pltpu.VMEM((seq, LANES), bf16),
            pltpu.VMEM((seq, pair_w), bf16),
            state, state, state,
        ],
        compiler_params=pltpu.CompilerParams(
            dimension_semantics=("arbitrary", "arbitrary"), vmem_limit_bytes=VMEM_LIMIT_DIL),
        name="dil_attn",
    )(qkv, qkv, qkv, qkv, qkv, qkv, qkv, biasmask)


MERGE_TM = 512


def _merge_kernel(x_ref, oa_ref, ob_ref, oc_ref, gz0_ref, gz1_ref, gz2_ref, wb_ref, wo_ref, g_ref, out_ref):
    branches = (oa_ref[...], ob_ref[...], oc_ref[...])
    gate_refs = (gz0_ref, gz1_ref, gz2_ref)
    merged = None
    for n in range(N_BRANCH):
        y = jnp.dot(branches[n], wb_ref[n], preferred_element_type=f32)
        y = jax.nn.sigmoid(gate_refs[n][...].astype(f32)) * y
        merged = y if merged is None else merged + y
    y = jnp.dot(merged.astype(bf16), wo_ref[...], preferred_element_type=f32)
    out_ref[...] = x_ref[...] + _rms(y, g_ref[...])


def _merge(x, oa, ob, oc, z, wb, wo, g):
    t = x.shape[0]
    row = lambda i: (i, 0)
    bw = pl.BlockSpec((MERGE_TM, BRANCH_W), row)
    gate_specs = [pl.BlockSpec((MERGE_TM, D_MODEL), lambda i, n=n: (i, GATE_BLK + n)) for n in range(N_BRANCH)]
    return pl.pallas_call(
        _merge_kernel,
        grid=(t // MERGE_TM,),
        in_specs=[
            pl.BlockSpec((MERGE_TM, D_MODEL), row),
            bw, bw, bw,
            *gate_specs,
            pl.BlockSpec((N_BRANCH, BRANCH_W, D_MODEL), lambda i: (0, 0, 0)),
            pl.BlockSpec((D_MODEL, D_MODEL), lambda i: (0, 0)),
            pl.BlockSpec((1, D_MODEL), lambda i: (0, 0)),
        ],
        out_specs=pl.BlockSpec((MERGE_TM, D_MODEL), row),
        out_shape=jax.ShapeDtypeStruct((t, D_MODEL), f32),
        compiler_params=pltpu.CompilerParams(
            dimension_semantics=("arbitrary",), vmem_limit_bytes=VMEM_LIMIT),
        name="merge",
    )(x, oa, ob, oc, z, z, z, wb, wo, g)


MLP_TM = 1024
MLP_TF = 1024


def _mlp_kernel(x_ref, gpre_ref, wup_ref, wdn_ref, gpost_ref, out_ref, h_ref, acc_ref):
    k = pl.program_id(1)

    @pl.when(k == 0)
    def _():
        h_ref[...] = _rms(x_ref[...], gpre_ref[...]).astype(bf16)
        acc_ref[...] = jnp.zeros_like(acc_ref)

    u = jnp.maximum(jnp.dot(h_ref[...], wup_ref[...], preferred_element_type=f32), 0.0)
    acc_ref[...] += jnp.dot((u * u).astype(bf16), wdn_ref[...], preferred_element_type=f32)

    @pl.when(k == pl.num_programs(1) - 1)
    def _():
        out_ref[...] = x_ref[...] + _rms(acc_ref[...], gpost_ref[...])


def _mlp(x, gpre, wup, wdn, gpost):
    t = x.shape[0]
    return pl.pallas_call(
        _mlp_kernel,
        grid=(t // MLP_TM, D_FF // MLP_TF),
        in_specs=[
            pl.BlockSpec((MLP_TM, D_MODEL), lambda i, k: (i, 0)),
            pl.BlockSpec((1, D_MODEL), lambda i, k: (0, 0)),
            pl.BlockSpec((D_MODEL, MLP_TF), lambda i, k: (0, k)),
            pl.BlockSpec((MLP_TF, D_MODEL), lambda i, k: (k, 0)),
            pl.BlockSpec((1, D_MODEL), lambda i, k: (0, 0)),
        ],
        out_specs=pl.BlockSpec((MLP_TM, D_MODEL), lambda i, k: (i, 0)),
        out_shape=jax.ShapeDtypeStruct((t, D_MODEL), f32),
        scratch_shapes=[pltpu.VMEM((MLP_TM, D_MODEL), bf16), pltpu.VMEM((MLP_TM, D_MODEL), f32)],
        compiler_params=pltpu.CompilerParams(
            dimension_semantics=("arbitrary", "arbitrary"), vmem_limit_bytes=VMEM_LIMIT),
        name="mlp",
    )(x, gpre, wup, wdn, gpost)


def _rel_bucket(rel):
    nb = REL_BUCKETS // 2
    max_exact = nb // 2
    ret = jnp.where(rel > 0, nb, 0)
    n = jnp.abs(rel)
    large = max_exact + (jnp.log(jnp.maximum(n, 1).astype(f32) / max_exact)
                         / math.log(REL_MAX_DIST / max_exact) * (nb - max_exact)).astype(jnp.int32)
    large = jnp.minimum(large, nb - 1)
    return ret + jnp.where(n < max_exact, n, large)


def _bias_lookup(tab, bucket):
    one_hot = jax.nn.one_hot(bucket, REL_BUCKETS, dtype=f32)
    return jnp.einsum('...b,bc->...c', one_hot, tab.astype(f32), precision=lax.Precision.HIGHEST)


def _pad_in_weights(w_in):
    sizes = (MLA_Q_RANK, MLA_KV_RANK, MLA_ROPE)
    c_q = w_in[..., :sizes[0]]
    c_kv = w_in[..., sizes[0]:sizes[0] + sizes[1]]
    k_r = w_in[..., sizes[0] + sizes[1]:sum(sizes)]
    rest = w_in[..., sum(sizes):]
    fq0 = (FQ_BLK - DQ_BLK) * LANES
    fq1 = (FK_BLK - DQ_BLK) * LANES
    z = lambda n: jnp.zeros(w_in.shape[:-1] + (n,), bf16)
    pieces = [c_q, c_kv, z(MLA_NOPE), k_r, z(LANES - MLA_NOPE - MLA_ROPE),
              rest[..., :fq0], rest[..., fq0:fq1] * (DIFF_SCALE * LOG2E), rest[..., fq1:]]
    return jnp.concatenate([p.astype(bf16) for p in pieces], axis=-1)


def _pad_mla_weights(w_uq, w_ukv):
    depth = w_uq.shape[0]
    wq = jnp.pad(w_uq, ((0, 0), (0, 0), (0, 0), (0, LANES - MLA_NOPE - MLA_ROPE)))
    wk = jnp.pad(w_ukv[..., :MLA_NOPE], ((0, 0), (0, 0), (0, 0), (0, LANES - MLA_NOPE)))
    wv = w_ukv[..., MLA_NOPE:]
    return (wq.reshape(depth, MLA_Q_RANK, MLA_HEADS * LANES).astype(bf16),
            wk.reshape(depth, MLA_KV_RANK, MLA_HEADS * LANES).astype(bf16),
            wv.reshape(depth, MLA_KV_RANK, MLA_HEADS * MLA_V).astype(bf16))


def _rope_tables(positions):
    inv = ROPE_BASE ** (-jnp.arange(ROPE_HALF, dtype=f32) / ROPE_HALF)
    ang = positions.reshape(-1).astype(f32)[:, None] * inv
    cos, sin = jnp.cos(ang), jnp.sin(ang)
    t = ang.shape[0]
    one = jnp.ones((t, MLA_NOPE), f32)
    z = lambda n: jnp.zeros((t, n), f32)
    tail = LANES - MLA_NOPE - MLA_ROPE
    c = jnp.concatenate([one, cos, cos, z(tail)], axis=-1)
    sa = jnp.concatenate([z(MLA_NOPE + ROPE_HALF), sin, z(tail)], axis=-1)
    sb = jnp.concatenate([z(MLA_NOPE), -sin, z(ROPE_HALF + tail)], axis=-1)
    return jnp.concatenate([c, sa, sb], axis=-1)


def _diff_bias_tiles(rel_bias):
    tab = rel_bias[:, DIL_BIAS_COLS:]
    i = jnp.arange(ATT_TK)[:, None]
    j = jnp.arange(ATT_TK)[None, :]
    offs = jnp.array([-2, -1, 0, 1, 2])[:, None, None] * ATT_TK
    bucket = _rel_bucket(offs + j - i)
    tiles = jnp.transpose(_bias_lookup(tab, bucket), (3, 0, 1, 2)) * LOG2E
    return tiles.reshape(2, DIFF_HEADS, N_BIAS_TILES, ATT_TK, ATT_TK)


def _dil_bias_masks(rel_bias):
    i = jnp.arange(DIL_TQ)[:, None]
    c = jnp.arange(DIL_TW)[None, :]
    rel = c - DIL_HALF - i
    out = []
    for g, (_, dil) in enumerate(DIL_PATTERNS):
        tab = rel_bias[:, g * DIL_HEADS:(g + 1) * DIL_HEADS]
        b = jnp.transpose(_bias_lookup(tab, _rel_bucket(rel * dil)), (2, 0, 1))
        out.append(jnp.where((jnp.abs(rel) <= DIL_HALF)[None], b, NEG))
    return jnp.stack(out, axis=0)


def kernel(x, positions, rel_bias, g_mix_pre, w_in, g_q, w_uq, g_kv, w_ukv, lam_q1, lam_k1, lam_q2, lam_k2,
           g_diff_sub, w_branch, w_out, g_mix_post, g_mlp_pre, w_up, w_down, g_mlp_post):
    batch, seq, d = x.shape
    depth = w_in.shape[0]
    assert d == D_MODEL and seq % 1024 == 0 and seq // DIL_PATTERNS[-1][1] >= 2 * DIL_TQ

    w_in_p = _pad_in_weights(w_in)
    wq_p, wk_p, wv_p = _pad_mla_weights(w_uq, w_ukv)
    wb = w_branch.astype(bf16)
    wo = w_out.astype(bf16)
    wup = w_up.astype(bf16)
    wdn = w_down.astype(bf16)
    rope = _rope_tables(positions)
    diff_bias = _diff_bias_tiles(rel_bias)
    dil_bias = _dil_bias_masks(rel_bias)

    lam_init = jnp.array([0.8 - 0.6 * math.exp(-0.3 * l) for l in range(depth)], f32)
    lam_row = jnp.zeros((depth, 1, DIFF_QK), f32).at[:, 0, 0].set(lam_init)
    lamvec = jnp.concatenate([lam_q1[:, None], lam_k1[:, None], lam_q2[:, None], lam_k2[:, None],
                              lam_row, jnp.zeros((depth, 3, DIFF_QK), f32)], axis=1).astype(f32)

    row = lambda v: v.reshape(1, -1).astype(f32)
    xt = x.reshape(batch * seq, d)
    for l in range(depth):
        z = _in_proj(xt, row(g_mix_pre[l]), w_in_p[l])
        q, k, v = _mla_prep(z, row(g_q[l]), row(g_kv[l]), wq_p[l], wk_p[l], wv_p[l], rope)
        o_a = _mla_attn(q, k, v, batch, seq)
        o_b = _dil_attn(z, dil_bias, batch, seq)
        o_c = _diff_attn(lamvec[l], z, diff_bias, row(g_diff_sub[l]), batch, seq)
        xt = _merge(xt, o_a, o_b, o_c, z, wb[l], wo[l], row(g_mix_post[l]))
        xt = _mlp(xt, row(g_mlp_pre[l]), wup[l], wdn[l], row(g_mlp_post[l]))
    return xt.reshape(batch, seq, d)
```

```python
import functools
import math

import jax
import jax.numpy as jnp
import numpy as np
from jax import lax
from jax.experimental import pallas as pl
from jax.experimental.pallas import tpu as pltpu

f32 = jnp.float32
bf16 = jnp.bfloat16

D_MODEL = 1024
MLA_HEADS = 8
MLA_Q_RANK = 256
MLA_KV_RANK = 128
MLA_NOPE = 64
MLA_ROPE = 32
MLA_V = 64
ROPE_BASE = 10000.0
DIL_PATTERNS = ((128, 1), (512, 4), (2048, 16))
DIL_GROUPS = 3
DIL_HEADS = 4
DIL_QK = 64
DIL_V = 128
DIL_HALF = 64
DIFF_HEADS = 4
DIFF_QK = 64
DIFF_V = 128
REL_BUCKETS = 32
REL_MAX_DIST = 128
DIL_BIAS_COLS = DIL_GROUPS * DIL_HEADS
N_BRANCH = 3
BRANCH_W = 512
D_FF = 4 * D_MODEL
EPS = 1e-6
NEG = -1e30

LANES = 128
A_COLS = 512
QKV_COLS = 3584
GATE_COLS = N_BRANCH * D_MODEL
IN_PAD_COLS = A_COLS + QKV_COLS + GATE_COLS
DQ_BLK, DK_BLK, DV_BLK, FQ_BLK, FK_BLK, FV_BLK = (A_COLS // LANES + o for o in (0, 6, 12, 16, 20, 24))
GATE_BLK = (A_COLS + QKV_COLS) // D_MODEL

VMEM_LIMIT = 48 * 1024 * 1024
VMEM_LIMIT_DIL = 56 * 1024 * 1024


def _rms(x, g):
    return x * lax.rsqrt(jnp.mean(x * x, axis=-1, keepdims=True) + EPS) * g


def _nt_dot(a, b):
    return lax.dot_general(a, b, (((1,), (1,)), ((), ())), preferred_element_type=f32)


IN_TM = 1024
IN_TN = 1792


def _in_proj_kernel(x_ref, g_ref, w_ref, z_ref, h_ref):
    @pl.when(pl.program_id(1) == 0)
    def _():
        h_ref[...] = _rms(x_ref[...], g_ref[...]).astype(bf16)

    z_ref[...] = jnp.dot(h_ref[...], w_ref[...], preferred_element_type=f32).astype(z_ref.dtype)


def _in_proj(x, g, w):
    t = x.shape[0]
    return pl.pallas_call(
        _in_proj_kernel,
        grid=(t // IN_TM, IN_PAD_COLS // IN_TN),
        in_specs=[
            pl.BlockSpec((IN_TM, D_MODEL), lambda i, j: (i, 0)),
            pl.BlockSpec((1, D_MODEL), lambda i, j: (0, 0)),
            pl.BlockSpec((D_MODEL, IN_TN), lambda i, j: (0, j)),
        ],
        out_specs=pl.BlockSpec((IN_TM, IN_TN), lambda i, j: (i, j)),
        out_shape=jax.ShapeDtypeStruct((t, IN_PAD_COLS), bf16),
        scratch_shapes=[pltpu.VMEM((IN_TM, D_MODEL), bf16)],
        compiler_params=pltpu.CompilerParams(
            dimension_semantics=("arbitrary", "arbitrary"), vmem_limit_bytes=VMEM_LIMIT),
        name="in_proj",
    )(x, g, w)


PREP_TM = 512
MLA_SCALE = (MLA_NOPE + MLA_ROPE) ** -0.5
LOG2E = math.log2(math.e)
ROPE_HALF = MLA_ROPE // 2


def _rope_lanes(x, rope):
    c = rope[:, 0:LANES]
    sa = rope[:, LANES:2 * LANES]
    sb = rope[:, 2 * LANES:3 * LANES]
    return (x * c + pltpu.roll(x, ROPE_HALF, 1) * sa + pltpu.roll(x, LANES - ROPE_HALF, 1) * sb)


def _mla_prep_kernel(a_ref, gq_ref, gkv_ref, wq_ref, wk_ref, wv_ref, rope_ref, q_ref, k_ref, v_ref):
    a = a_ref[...].astype(f32)
    rope = rope_ref[...]
    cq = _rms(a[:, :MLA_Q_RANK], gq_ref[...]).astype(bf16)
    ckv = _rms(a[:, MLA_Q_RANK:MLA_Q_RANK + MLA_KV_RANK], gkv_ref[...]).astype(bf16)
    k_rope = _rope_lanes(a[:, A_COLS - LANES:], rope)
    qf = jnp.dot(cq, wq_ref[...], preferred_element_type=f32)
    kf = jnp.dot(ckv, wk_ref[...], preferred_element_type=f32)
    vf = jnp.dot(ckv, wv_ref[...], preferred_element_type=f32).astype(bf16)
    ones = jnp.ones((vf.shape[0], LANES), bf16)
    lane = lax.broadcasted_iota(jnp.int32, ones.shape, 1)
    for j in range(MLA_HEADS // 2):
        pair = vf[:, j * LANES:(j + 1) * LANES]
        v_ref[:, 2 * j * LANES:(2 * j + 1) * LANES] = jnp.where(lane < MLA_V, pair, ones)
        v_ref[:, (2 * j + 1) * LANES:(2 * j + 2) * LANES] = jnp.where(lane >= MLA_V, pair, ones)
    for h in range(MLA_HEADS):
        sl = slice(h * LANES, (h + 1) * LANES)
        q_ref[:, sl] = (_rope_lanes(qf[:, sl], rope) * (MLA_SCALE * LOG2E)).astype(bf16)
        k_ref[:, sl] = (kf[:, sl] + k_rope).astype(bf16)


def _mla_prep(a, gq, gkv, wq, wk, wv, rope):
    t = a.shape[0]
    hq = MLA_HEADS * LANES
    hv = MLA_HEADS * MLA_V
    const = lambda i: (0, 0)
    return pl.pallas_call(
        _mla_prep_kernel,
        grid=(t // PREP_TM,),
        in_specs=[
            pl.BlockSpec((PREP_TM, A_COLS), lambda i: (i, 0)),
            pl.BlockSpec((1, MLA_Q_RANK), const),
            pl.BlockSpec((1, MLA_KV_RANK), const),
            pl.BlockSpec((MLA_Q_RANK, hq), const),
            pl.BlockSpec((MLA_KV_RANK, hq), const),
            pl.BlockSpec((MLA_KV_RANK, hv), const),
            pl.BlockSpec((PREP_TM, 3 * LANES), lambda i: (i, 0)),
        ],
        out_specs=[
            pl.BlockSpec((PREP_TM, hq), lambda i: (i, 0)),
            pl.BlockSpec((PREP_TM, hq), lambda i: (i, 0)),
            pl.BlockSpec((PREP_TM, hq), lambda i: (i, 0)),
        ],
        out_shape=[
            jax.ShapeDtypeStruct((t, hq), bf16),
            jax.ShapeDtypeStruct((t, hq), bf16),
            jax.ShapeDtypeStruct((t, hq), bf16),
        ],
        compiler_params=pltpu.CompilerParams(
            dimension_semantics=("arbitrary",), vmem_limit_bytes=VMEM_LIMIT),
        name="mla_prep",
    )(a, gq, gkv, wq, wk, wv, rope)


ATT_TQ = 512
ATT_TK = 256


MLA_TK = 256
DIFF_TK = 512


def _lane_fold(x, op):
    out = x[:, :LANES]
    for u in range(1, x.shape[1] // LANES):
        out = op(out, x[:, u * LANES:(u + 1) * LANES])
    return out


def _flash_streams(streams, seq, tk):
    n_chunks = seq // tk
    state = [None] * len(streams)
    for c in range(n_chunks):
        rows = slice(c * tk, (c + 1) * tk)
        for i, (q, k_ref, k_cols, v_ref, v_cols, bias_fn) in enumerate(streams):
            s = _nt_dot(q, k_ref[rows, k_cols])
            if bias_fn is not None:
                s = s + bias_fn(c)
            m_c = jnp.max(_lane_fold(s, jnp.maximum), axis=-1, keepdims=True)
            if c == 0:
                m_new = m_c
            else:
                m_old, acc_old = state[i]
                m_new = jnp.maximum(m_old, m_c)
            p = jnp.exp2(s - m_new).astype(bf16)
            pv = jnp.dot(p, v_ref[rows, v_cols], preferred_element_type=f32)
            state[i] = (m_new, pv if c == 0 else jnp.exp2(m_old - m_new) * acc_old + pv)
    return [acc for _, acc in state]


def _mla_attn_kernel(q_ref, k_ref, v_ref, o_ref):
    tq = q_ref.shape[0]
    streams = [(q_ref[:, hh * LANES:(hh + 1) * LANES], k_ref, slice(hh * LANES, (hh + 1) * LANES),
                v_ref, slice(hh * LANES, (hh + 1) * LANES), None) for hh in range(2)]
    acc_a, acc_b = _flash_streams(streams, k_ref.shape[0], MLA_TK)
    lane = lax.broadcasted_iota(jnp.int32, (tq, LANES), 1)
    o_ref[...] = jnp.where(lane < MLA_V, acc_a / acc_a[:, MLA_V:MLA_V + 1],
                           acc_b / acc_b[:, 0:1]).astype(o_ref.dtype)


def _mla_attn(q, k, v, batch, seq):
    t = q.shape[0]
    nq = seq // ATT_TQ
    n_pairs = MLA_HEADS // 2
    return pl.pallas_call(
        _mla_attn_kernel,
        grid=(batch, n_pairs, nq),
        in_specs=[
            pl.BlockSpec((ATT_TQ, 2 * LANES), lambda b, j, i: (b * nq + i, j)),
            pl.BlockSpec((seq, 2 * LANES), lambda b, j, i: (b, j)),
            pl.BlockSpec((seq, 2 * LANES), lambda b, j, i: (b, j)),
        ],
        out_specs=pl.BlockSpec((ATT_TQ, LANES), lambda b, j, i: (b * nq + i, j)),
        out_shape=jax.ShapeDtypeStruct((t, MLA_HEADS * MLA_V), bf16),
        compiler_params=pltpu.CompilerParams(
            dimension_semantics=("arbitrary", "arbitrary", "arbitrary"), vmem_limit_bytes=VMEM_LIMIT),
        name="mla_attn",
    )(q, k, v)


DIFF_TQ = 512
DIFF_SCALE = DIFF_QK ** -0.5
N_BIAS_TILES = 5


def _diff_attn_kernel(lam_ref, q0_ref, q1_ref, k0_ref, k1_ref, v_ref, bias_ref, gsub_ref, o_ref, vaug_ref):
    qi = pl.program_id(2)
    tq = q0_ref.shape[0]
    seq = v_ref.shape[0]

    @pl.when(qi == 0)
    def _():
        for hh in range(2):
            vaug_ref[hh, :, :DIFF_V] = v_ref[:, hh * DIFF_V:(hh + 1) * DIFF_V]
            vaug_ref[hh, :, DIFF_V:] = jnp.ones((seq, DIFF_V), bf16)

    lv = lam_ref[...]
    lam_init = lv[4:5, 0:1]
    lam = (jnp.exp(jnp.sum(lv[0:1] * lv[1:2], axis=-1, keepdims=True))
           - jnp.exp(jnp.sum(lv[2:3] * lv[3:4], axis=-1, keepdims=True)) + lam_init)
    lane = lax.broadcasted_iota(jnp.int32, (tq, LANES), 1)
    q_refs = (q0_ref, q1_ref)
    k_refs = (k0_ref, k1_ref)
    all_cols = slice(0, LANES)
    tiles_per_chunk = DIFF_TK // ATT_TK
    streams = []
    for hh in range(2):
        head_lanes = (lane >= DIFF_QK) == (hh == 1)
        for m in range(2):
            q = q_refs[m][...]
            q = jnp.where(head_lanes, q, jnp.zeros_like(q))

            def bias_fn(c, m=m, hh=hh):
                def tile(r, cc):
                    offset = (c * tiles_per_chunk + cc) - (qi * (tq // ATT_TK) + r)
                    return bias_ref[m, hh, jnp.clip(offset, -2, 2) + 2]
                return jnp.concatenate(
                    [jnp.concatenate([tile(r, cc) for cc in range(tiles_per_chunk)], axis=1)
                     for r in range(tq // ATT_TK)], axis=0)

            streams.append((q, k_refs[m], all_cols, vaug_ref.at[hh], slice(0, 2 * DIFF_V), bias_fn))
    outs = [acc[:, :DIFF_V] / acc[:, DIFF_V:DIFF_V + 1] for acc in _flash_streams(streams, seq, DIFF_TK)]
    for hh in range(2):
        o = outs[2 * hh] - lam * outs[2 * hh + 1]
        o = _rms(o, gsub_ref[...]) * (1.0 - lam_init)
        o_ref[:, hh * DIFF_V:(hh + 1) * DIFF_V] = o.astype(o_ref.dtype)


def _diff_attn(lamvec, qkv, bias_tiles, gsub, batch, seq):
    t = qkv.shape[0]
    nq = seq // DIFF_TQ
    n_pairs = DIFF_HEADS // 2
    return pl.pallas_call(
        _diff_attn_kernel,
        grid=(batch, n_pairs, nq),
        in_specs=[
            pl.BlockSpec((8, DIFF_QK), lambda b, j, i: (0, 0)),
            pl.BlockSpec((DIFF_TQ, LANES), lambda b, j, i: (b * nq + i, FQ_BLK + j)),
            pl.BlockSpec((DIFF_TQ, LANES), lambda b, j, i: (b * nq + i, FQ_BLK + n_pairs + j)),
            pl.BlockSpec((seq, LANES), lambda b, j, i: (b, FK_BLK + j)),
            pl.BlockSpec((seq, LANES), lambda b, j, i: (b, FK_BLK + n_pairs + j)),
            pl.BlockSpec((seq, 2 * DIFF_V), lambda b, j, i: (b, FV_BLK // 2 + j)),
            pl.BlockSpec((2, 2, N_BIAS_TILES, ATT_TK, ATT_TK), lambda b, j, i: (0, j, 0, 0, 0)),
            pl.BlockSpec((1, DIFF_V), lambda b, j, i: (0, 0)),
        ],
        out_specs=pl.BlockSpec((DIFF_TQ, 2 * DIFF_V), lambda b, j, i: (b * nq + i, j)),
        out_shape=jax.ShapeDtypeStruct((t, DIFF_HEADS * DIFF_V), bf16),
        scratch_shapes=[pltpu.VMEM((2, seq, 2 * DIFF_V), bf16)],
        compiler_params=pltpu.CompilerParams(
            dimension_semantics=("arbitrary", "arbitrary", "arbitrary"), vmem_limit_bytes=VMEM_LIMIT),
        name="diff_attn",
    )(lamvec, qkv, qkv, qkv, qkv, qkv, bias_tiles, gsub)


DIL_TQ = 128
DIL_TW = DIL_TQ + 2 * DIL_HALF
DIL_SCALE = DIL_QK ** -0.5
DIL_UNROLL = 8


def _regroup_start_stride(r, seq, dil_from, dil_to):
    return (r % dil_from) * (seq // dil_from) + r // dil_from, dil_to // dil_from


def _residue_major(src_ref, dst_ref, stage_ref, dil_from, dil_to):
    seq, width = src_ref.shape
    sub_len = seq // dil_to
    for c in range(width // LANES):
        csl = slice(c * LANES, (c + 1) * LANES)
        stage_ref[c] = src_ref[:, csl].astype(f32)
        for r in range(dil_to):
            start, stride = _regroup_start_stride(r, seq, dil_from, dil_to)
            dst_ref[r * sub_len:(r + 1) * sub_len, csl] = (
                stage_ref[c, pl.ds(start, sub_len, stride=stride), :].astype(dst_ref.dtype))


def _regroup_state(state_ref, stage_ref, dil_from, dil_to):
    _, seq, _ = state_ref.shape
    sub_len = seq // dil_to
    for hh in range(2):
        stage_ref[0] = state_ref[hh]
        for r in range(dil_to):
            start, stride = _regroup_start_stride(r, seq, dil_from, dil_to)
            state_ref[hh, r * sub_len:(r + 1) * sub_len, :] = stage_ref[0, pl.ds(start, sub_len, stride=stride), :]


def _ungroup_state(state_ref, stage_ref, dil_from, dil_to):
    _, seq, _ = state_ref.shape
    sub_len = seq // dil_to
    for hh in range(2):
        for r in range(dil_to):
            start, stride = _regroup_start_stride(r, seq, dil_from, dil_to)
            stage_ref[0, pl.ds(start, sub_len, stride=stride), :] = state_ref[hh, r * sub_len:(r + 1) * sub_len, :]
        state_ref[hh] = stage_ref[0]


def _dil_attn_kernel(q0_ref, q1_ref, q2_ref, k0_ref, k1_ref, k2_ref, v_ref, bm_ref, o_ref,
                     stage_ref, qp_ref, kp_ref, vp_ref, m_ref, l_ref, acc_ref):
    seq = v_ref.shape[0]
    n_tiles = seq // DIL_TQ
    q_refs = (q0_ref, q1_ref, q2_ref)
    k_refs = (k0_ref, k1_ref, k2_ref)
    lane = lax.broadcasted_iota(jnp.int32, (DIL_TQ, LANES), 1)
    col = lax.broadcasted_iota(jnp.int32, (DIL_TQ, DIL_TW), 1)
    dils = [dil for _, dil in DIL_PATTERNS]
    assert dils[0] == 1 and all(b % a == 0 for a, b in zip(dils, dils[1:]))
    for g, dil in enumerate(dils):
        if g == 0:
            q_src, k_src, v_src = q_refs[g], k_refs[g], v_ref
        else:
            _residue_major(q_refs[g], qp_ref, stage_ref, 1, dil)
            _residue_major(k_refs[g], kp_ref, stage_ref, 1, dil)
            _residue_major(v_src, vp_ref, stage_ref, dils[g - 1], dil)
            for state_ref in (m_ref, l_ref, acc_ref):
                _regroup_state(state_ref, stage_ref, dils[g - 1], dil)
            q_src, k_src, v_src = qp_ref, kp_ref, vp_ref
        tiles_per_residue = n_tiles // dil

        def tile(n, carry, g=g, dil=dil, q_src=q_src, k_src=k_src, v_src=v_src,
                 tiles_per_residue=tiles_per_residue):
            base = pl.multiple_of(n * DIL_TQ, DIL_TQ)
            t_in = n % tiles_per_residue
            lo = pl.multiple_of(jnp.maximum(base - DIL_HALF, 0), DIL_HALF)
            hi = pl.multiple_of(jnp.minimum(base + DIL_TQ, seq - DIL_HALF), DIL_HALF)
            c_lo = jnp.where(t_in == 0, DIL_HALF, 0)
            c_hi = jnp.where(t_in == tiles_per_residue - 1, DIL_HALF + DIL_TQ, DIL_TW)
            valid = (col >= c_lo) & (col < c_hi)
            rows = pl.ds(base, DIL_TQ)
            qt = q_src[rows, :]
            kw = jnp.concatenate([k_src[pl.ds(lo, DIL_HALF), :], k_src[pl.ds(base, DIL_TQ), :],
                                  k_src[pl.ds(hi, DIL_HALF), :]], axis=0)
            vw = jnp.concatenate([v_src[pl.ds(lo, DIL_HALF), :], v_src[pl.ds(base, DIL_TQ), :],
                                  v_src[pl.ds(hi, DIL_HALF), :]], axis=0)
            for hh in range(2):
                qh = jnp.where((lane >= DIL_QK) == (hh == 1), qt * DIL_SCALE, jnp.zeros_like(qt))
                hsl = slice(hh * DIL_V, (hh + 1) * DIL_V)
                s = _nt_dot(qh, kw) + bm_ref[g, hh]
                s = jnp.where(valid, s, NEG)
                m_t = jnp.max(s, axis=-1, keepdims=True)
                p = jnp.exp(s - m_t)
                l_t = jnp.broadcast_to(jnp.sum(p, axis=-1, keepdims=True), (DIL_TQ, DIL_V))
                u_t = jnp.dot(p.astype(bf16), vw[:, hsl], preferred_element_type=f32)
                m_t = jnp.broadcast_to(m_t, (DIL_TQ, DIL_V))
                if g == 0:
                    m_ref[hh, rows, :] = m_t
                    l_ref[hh, rows, :] = l_t
                    acc_ref[hh, rows, :] = u_t
                else:
                    m_old = m_ref[hh, rows, :]
                    m_new = jnp.maximum(m_old, m_t)
                    a_old = jnp.exp(m_old - m_new)
                    a_t = jnp.exp(m_t - m_new)
                    m_ref[hh, rows, :] = m_new
                    l_ref[hh, rows, :] = a_old * l_ref[hh, rows, :] + a_t * l_t
                    acc_ref[hh, rows, :] = a_old * acc_ref[hh, rows, :] + a_t * u_t
            return carry

        def tile_group(i, carry, tile=tile):
            for u in range(DIL_UNROLL):
                tile(i * DIL_UNROLL + u, carry)
            return carry

        lax.fori_loop(0, n_tiles // DIL_UNROLL, tile_group, 0)
    for hh in range(2):
        acc_ref[hh] = acc_ref[hh] / l_ref[hh]
    for g in range(len(dils) - 1, 0, -1):
        _ungroup_state(acc_ref, stage_ref, dils[g - 1], dils[g])
    for hh in range(2):
        o_ref[:, hh * DIL_V:(hh + 1) * DIL_V] = acc_ref[hh].astype(o_ref.dtype)


def _dil_attn(qkv, biasmask, batch, seq):
    t = qkv.shape[0]
    n_pairs = DIL_HEADS // 2
    pair_w = 2 * DIL_V

    def qk_spec(blk0, g):
        return pl.BlockSpec((seq, LANES), lambda b, j: (b, blk0 + g * n_pairs + j))

    state = pltpu.VMEM((2, seq, DIL_V), f32)
    return pl.pallas_call(
        _dil_attn_kernel,
        grid=(batch, n_pairs),
        in_specs=[qk_spec(DQ_BLK, g) for g in range(DIL_GROUPS)]
        + [qk_spec(DK_BLK, g) for g in range(DIL_GROUPS)]
        + [pl.BlockSpec((seq, pair_w), lambda b, j: (b, DV_BLK // 2 + j)),
           pl.BlockSpec((DIL_GROUPS, 2, DIL_TQ, DIL_TW), lambda b, j: (0, j, 0, 0))],
        out_specs=pl.BlockSpec((seq, pair_w), lambda b, j: (b, j)),
        out_shape=jax.ShapeDtypeStruct((t, DIL_HEADS * DIL_V), bf16),
        scratch_shapes=[
            pltpu.VMEM((2, seq, LANES), f32),
            pltpu.VMEM((seq, LANES), bf16),
            pltpu.VMEM((seq, LANES), bf16),
            pltpu.VMEM((seq, pair_w), bf16),
            state, state, state,
        ],
        compiler_params=pltpu.CompilerParams(
            dimension_semantics=("arbitrary", "arbitrary"), vmem_limit_bytes=VMEM_LIMIT_DIL),
        name="dil_attn",
    )(qkv, qkv, qkv, qkv, qkv, qkv, qkv, biasmask)


MERGE_TM = 512


def _merge_kernel(x_ref, oa_ref, ob_ref, oc_ref, gz0_ref, gz1_ref, gz2_ref, wb_ref, wo_ref, g_ref, out_ref):
    branches = (oa_ref[...], ob_ref[...], oc_ref[...])
    gate_refs = (gz0_ref, gz1_ref, gz2_ref)
    merged = None
    for n in range(N_BRANCH):
        y = jnp.dot(branches[n], wb_ref[n], preferred_element_type=f32)
        y = jax.nn.sigmoid(gate_refs[n][...].astype(f32)) * y
        merged = y if merged is None else merged + y
    y = jnp.dot(merged.astype(bf16), wo_ref[...], preferred_element_type=f32)
    out_ref[...] = x_ref[...] + _rms(y, g_ref[...])


def _merge(x, oa, ob, oc, z, wb, wo, g):
    t = x.shape[0]
    row = lambda i: (i, 0)
    bw = pl.BlockSpec((MERGE_TM, BRANCH_W), row)
    gate_specs = [pl.BlockSpec((MERGE_TM, D_MODEL), lambda i, n=n: (i, GATE_BLK + n)) for n in range(N_BRANCH)]
    return pl.pallas_call(
        _merge_kernel,
        grid=(t // MERGE_TM,),
        in_specs=[
            pl.BlockSpec((MERGE_TM, D_MODEL), row),
            bw, bw, bw,
            *gate_specs,
            pl.BlockSpec((N_BRANCH, BRANCH_W, D_MODEL), lambda i: (0, 0, 0)),
            pl.BlockSpec((D_MODEL, D_MODEL), lambda i: (0, 0)),
            pl.BlockSpec((1, D_MODEL), lambda i: (0, 0)),
        ],
        out_specs=pl.BlockSpec((MERGE_TM, D_MODEL), row),
        out_shape=jax.ShapeDtypeStruct((t, D_MODEL), f32),
        compiler_params=pltpu.CompilerParams(
            dimension_semantics=("arbitrary",), vmem_limit_bytes=VMEM_LIMIT),
        name="merge",
    )(x, oa, ob, oc, z, z, z, wb, wo, g)


MLP_TM = 1024
MLP_TF = 1024


def _mlp_kernel(x_ref, gpre_ref, wup_ref, wdn_ref, gpost_ref, out_ref, h_ref, acc_ref):
    k = pl.program_id(1)

    @pl.when(k == 0)
    def _():
        h_ref[...] = _rms(x_ref[...], gpre_ref[...]).astype(bf16)
        acc_ref[...] = jnp.zeros_like(acc_ref)

    u = jnp.maximum(jnp.dot(h_ref[...], wup_ref[...], preferred_element_type=f32), 0.0)
    acc_ref[...] += jnp.dot((u * u).astype(bf16), wdn_ref[...], preferred_element_type=f32)

    @pl.when(k == pl.num_programs(1) - 1)
    def _():
        out_ref[...] = x_ref[...] + _rms(acc_ref[...], gpost_ref[...])


def _mlp(x, gpre, wup, wdn, gpost):
    t = x.shape[0]
    return pl.pallas_call(
        _mlp_kernel,
        grid=(t // MLP_TM, D_FF // MLP_TF),
        in_specs=[
            pl.BlockSpec((MLP_TM, D_MODEL), lambda i, k: (i, 0)),
            pl.BlockSpec((1, D_MODEL), lambda i, k: (0, 0)),
            pl.BlockSpec((D_MODEL, MLP_TF), lambda i, k: (0, k)),
            pl.BlockSpec((MLP_TF, D_MODEL), lambda i, k: (k, 0)),
            pl.BlockSpec((1, D_MODEL), lambda i, k: (0, 0)),
        ],
        out_specs=pl.BlockSpec((MLP_TM, D_MODEL), lambda i, k: (i, 0)),
        out_shape=jax.ShapeDtypeStruct((t, D_MODEL), f32),
        scratch_shapes=[pltpu.VMEM((MLP_TM, D_MODEL), bf16), pltpu.VMEM((MLP_TM, D_MODEL), f32)],
        compiler_params=pltpu.CompilerParams(
            dimension_semantics=("arbitrary", "arbitrary"), vmem_limit_bytes=VMEM_LIMIT),
        name="mlp",
    )(x, gpre, wup, wdn, gpost)


def _rel_bucket(rel):
    nb = REL_BUCKETS // 2
    max_exact = nb // 2
    ret = jnp.where(rel > 0, nb, 0)
    n = jnp.abs(rel)
    large = max_exact + (jnp.log(jnp.maximum(n, 1).astype(f32) / max_exact)
                         / math.log(REL_MAX_DIST / max_exact) * (nb - max_exact)).astype(jnp.int32)
    large = jnp.minimum(large, nb - 1)
    return ret + jnp.where(n < max_exact, n, large)


def _bias_lookup(tab, bucket):
    one_hot = jax.nn.one_hot(bucket, REL_BUCKETS, dtype=f32)
    return jnp.einsum('...b,bc->...c', one_hot, tab.astype(f32), precision=lax.Precision.HIGHEST)


def _pad_in_weights(w_in):
    sizes = (MLA_Q_RANK, MLA_KV_RANK, MLA_ROPE)
    c_q = w_in[..., :sizes[0]]
    c_kv = w_in[..., sizes[0]:sizes[0] + sizes[1]]
    k_r = w_in[..., sizes[0] + sizes[1]:sum(sizes)]
    rest = w_in[..., sum(sizes):]
    fq0 = (FQ_BLK - DQ_BLK) * LANES
    fq1 = (FK_BLK - DQ_BLK) * LANES
    z = lambda n: jnp.zeros(w_in.shape[:-1] + (n,), bf16)
    pieces = [c_q, c_kv, z(MLA_NOPE), k_r, z(LANES - MLA_NOPE - MLA_ROPE),
              rest[..., :fq0], rest[..., fq0:fq1] * (DIFF_SCALE * LOG2E), rest[..., fq1:]]
    return jnp.concatenate([p.astype(bf16) for p in pieces], axis=-1)


def _pad_mla_weights(w_uq, w_ukv):
    depth = w_uq.shape[0]
    wq = jnp.pad(w_uq, ((0, 0), (0, 0), (0, 0), (0, LANES - MLA_NOPE - MLA_ROPE)))
    wk = jnp.pad(w_ukv[..., :MLA_NOPE], ((0, 0), (0, 0), (0, 0), (0, LANES - MLA_NOPE)))
    wv = w_ukv[..., MLA_NOPE:]
    return (wq.reshape(depth, MLA_Q_RANK, MLA_HEADS * LANES).astype(bf16),
            wk.reshape(depth, MLA_KV_RANK, MLA_HEADS * LANES).astype(bf16),
            wv.reshape(depth, MLA_KV_RANK, MLA_HEADS * MLA_V).astype(bf16))


def _rope_tables(positions):
    inv = ROPE_BASE ** (-jnp.arange(ROPE_HALF, dtype=f32) / ROPE_HALF)
    ang = positions.reshape(-1).astype(f32)[:, None] * inv
    cos_sin = jnp.concatenate([jnp.cos(ang), jnp.sin(ang)], axis=-1)
    place = np.zeros((2 * ROPE_HALF, 3 * LANES), np.float32)
    base = np.zeros((1, 3 * LANES), np.float32)
    base[0, :MLA_NOPE] = 1.0
    for i in range(ROPE_HALF):
        place[i, MLA_NOPE + i] = 1.0
        place[i, MLA_NOPE + ROPE_HALF + i] = 1.0
        place[ROPE_HALF + i, LANES + MLA_NOPE + ROPE_HALF + i] = 1.0
        place[ROPE_HALF + i, 2 * LANES + MLA_NOPE + i] = -1.0
    return jnp.dot(cos_sin, jnp.asarray(place), precision=lax.Precision.HIGHEST) + jnp.asarray(base)


def _diff_bias_tiles(rel_bias):
    tab = rel_bias[:, DIL_BIAS_COLS:]
    i = jnp.arange(ATT_TK)[:, None]
    j = jnp.arange(ATT_TK)[None, :]
    offs = jnp.array([-2, -1, 0, 1, 2])[:, None, None] * ATT_TK
    bucket = _rel_bucket(offs + j - i)
    tiles = jnp.transpose(_bias_lookup(tab, bucket), (3, 0, 1, 2)) * LOG2E
    return tiles.reshape(2, DIFF_HEADS, N_BIAS_TILES, ATT_TK, ATT_TK)


def _dil_bias_masks(rel_bias):
    i = jnp.arange(DIL_TQ)[:, None]
    c = jnp.arange(DIL_TW)[None, :]
    rel = c - DIL_HALF - i
    out = []
    for g, (_, dil) in enumerate(DIL_PATTERNS):
        tab = rel_bias[:, g * DIL_HEADS:(g + 1) * DIL_HEADS]
        b = jnp.transpose(_bias_lookup(tab, _rel_bucket(rel * dil)), (2, 0, 1))
        out.append(jnp.where((jnp.abs(rel) <= DIL_HALF)[None], b, NEG))
    return jnp.stack(out, axis=0)


def kernel(x, positions, rel_bias, g_mix_pre, w_in, g_q, w_uq, g_kv, w_ukv, lam_q1, lam_k1, lam_q2, lam_k2,
           g_diff_sub, w_branch, w_out, g_mix_post, g_mlp_pre, w_up, w_down, g_mlp_post):
    batch, seq, d = x.shape
    depth = w_in.shape[0]
    assert d == D_MODEL and seq % 1024 == 0 and seq // DIL_PATTERNS[-1][1] >= 2 * DIL_TQ

    w_in_p = _pad_in_weights(w_in)
    wq_p, wk_p, wv_p = _pad_mla_weights(w_uq, w_ukv)
    wb = w_branch.astype(bf16)
    wo = w_out.astype(bf16)
    wup = w_up.astype(bf16)
    wdn = w_down.astype(bf16)
    rope = _rope_tables(positions)
    diff_bias = _diff_bias_tiles(rel_bias)
    dil_bias = _dil_bias_masks(rel_bias)

    lam_init = jnp.array([0.8 - 0.6 * math.exp(-0.3 * l) for l in range(depth)], f32)
    lam_row = jnp.zeros((depth, 1, DIFF_QK), f32).at[:, 0, 0].set(lam_init)
    lamvec = jnp.concatenate([lam_q1[:, None], lam_k1[:, None], lam_q2[:, None], lam_k2[:, None],
                              lam_row, jnp.zeros((depth, 3, DIFF_QK), f32)], axis=1).astype(f32)

    row = lambda v: v.reshape(1, -1).astype(f32)
    xt = x.reshape(batch * seq, d)
    for l in range(depth):
        z = _in_proj(xt, row(g_mix_pre[l]), w_in_p[l])
        q, k, v = _mla_prep(z, row(g_q[l]), row(g_kv[l]), wq_p[l], wk_p[l], wv_p[l], rope)
        o_a = _mla_attn(q, k, v, batch, seq)
        o_b = _dil_attn(z, dil_bias, batch, seq)
        o_c = _diff_attn(lamvec[l], z, diff_bias, row(g_diff_sub[l]), batch, seq)
        xt = _merge(xt, o_a, o_b, o_c, z, wb[l], wo[l], row(g_mix_post[l]))
        xt = _mlp(xt, row(g_mlp_pre[l]), wup[l], wdn[l], row(g_mlp_post[l]))
    return xt.reshape(batch, seq, d)
```

```python
import functools
import math

import jax
import jax.numpy as jnp
import numpy as np
from jax import lax
from jax.experimental import pallas as pl
from jax.experimental.pallas import tpu as pltpu

f32 = jnp.float32
bf16 = jnp.bfloat16

D_MODEL = 1024
MLA_HEADS = 8
MLA_Q_RANK = 256
MLA_KV_RANK = 128
MLA_NOPE = 64
MLA_ROPE = 32
MLA_V = 64
ROPE_BASE = 10000.0
DIL_PATTERNS = ((128, 1), (512, 4), (2048, 16))
DIL_GROUPS = 3
DIL_HEADS = 4
DIL_QK = 64
DIL_V = 128
DIL_HALF = 64
DIFF_HEADS = 4
DIFF_QK = 64
DIFF_V = 128
REL_BUCKETS = 32
REL_MAX_DIST = 128
DIL_BIAS_COLS = DIL_GROUPS * DIL_HEADS
N_BRANCH = 3
BRANCH_W = 512
D_FF = 4 * D_MODEL
EPS = 1e-6
NEG = -1e30

LANES = 128
A_COLS = 512
QKV_COLS = 3584
GATE_COLS = N_BRANCH * D_MODEL
IN_PAD_COLS = A_COLS + QKV_COLS + GATE_COLS
DQ_BLK, DK_BLK, DV_BLK, FQ_BLK, FK_BLK, FV_BLK = (A_COLS // LANES + o for o in (0, 6, 12, 16, 20, 24))
GATE_BLK = (A_COLS + QKV_COLS) // D_MODEL

VMEM_LIMIT = 48 * 1024 * 1024
VMEM_LIMIT_DIL = 56 * 1024 * 1024


def _rms(x, g):
    return x * lax.rsqrt(jnp.mean(x * x, axis=-1, keepdims=True) + EPS) * g


def _nt_dot(a, b):
    return lax.dot_general(a, b, (((1,), (1,)), ((), ())), preferred_element_type=f32)


IN_TM = 1024
IN_TN = 3584


def _in_proj_kernel(x_ref, g_ref, w_ref, z_ref, h_ref):
    @pl.when(pl.program_id(1) == 0)
    def _():
        h_ref[...] = _rms(x_ref[...], g_ref[...]).astype(bf16)

    z_ref[...] = jnp.dot(h_ref[...], w_ref[...], preferred_element_type=f32).astype(z_ref.dtype)


def _in_proj(x, g, w):
    t = x.shape[0]
    return pl.pallas_call(
        _in_proj_kernel,
        grid=(t // IN_TM, IN_PAD_COLS // IN_TN),
        in_specs=[
            pl.BlockSpec((IN_TM, D_MODEL), lambda i, j: (i, 0)),
            pl.BlockSpec((1, D_MODEL), lambda i, j: (0, 0)),
            pl.BlockSpec((D_MODEL, IN_TN), lambda i, j: (0, j)),
        ],
        out_specs=pl.BlockSpec((IN_TM, IN_TN), lambda i, j: (i, j)),
        out_shape=jax.ShapeDtypeStruct((t, IN_PAD_COLS), bf16),
        scratch_shapes=[pltpu.VMEM((IN_TM, D_MODEL), bf16)],
        compiler_params=pltpu.CompilerParams(
            dimension_semantics=("arbitrary", "arbitrary"), vmem_limit_bytes=VMEM_LIMIT),
        name="in_proj",
    )(x, g, w)


PREP_TM = 512
MLA_SCALE = (MLA_NOPE + MLA_ROPE) ** -0.5
LOG2E = math.log2(math.e)
ROPE_HALF = MLA_ROPE // 2


def _rope_lanes(x, rope):
    c = rope[:, 0:LANES]
    sa = rope[:, LANES:2 * LANES]
    sb = rope[:, 2 * LANES:3 * LANES]
    return (x * c + pltpu.roll(x, ROPE_HALF, 1) * sa + pltpu.roll(x, LANES - ROPE_HALF, 1) * sb)


def _mla_prep_kernel(a_ref, gq_ref, gkv_ref, wq_ref, wk_ref, wv_ref, rope_ref, q_ref, k_ref, v_ref):
    a = a_ref[...].astype(f32)
    rope = rope_ref[...]
    cq = _rms(a[:, :MLA_Q_RANK], gq_ref[...]).astype(bf16)
    ckv = _rms(a[:, MLA_Q_RANK:MLA_Q_RANK + MLA_KV_RANK], gkv_ref[...]).astype(bf16)
    k_rope = _rope_lanes(a[:, A_COLS - LANES:], rope)
    qf = jnp.dot(cq, wq_ref[...], preferred_element_type=f32)
    kf = jnp.dot(ckv, wk_ref[...], preferred_element_type=f32)
    vf = jnp.dot(ckv, wv_ref[...], preferred_element_type=f32).astype(bf16)
    ones = jnp.ones((vf.shape[0], LANES), bf16)
    lane = lax.broadcasted_iota(jnp.int32, ones.shape, 1)
    for j in range(MLA_HEADS // 2):
        pair = vf[:, j * LANES:(j + 1) * LANES]
        v_ref[:, 2 * j * LANES:(2 * j + 1) * LANES] = jnp.where(lane < MLA_V, pair, ones)
        v_ref[:, (2 * j + 1) * LANES:(2 * j + 2) * LANES] = jnp.where(lane >= MLA_V, pair, ones)
    for h in range(MLA_HEADS):
        sl = slice(h * LANES, (h + 1) * LANES)
        q_ref[:, sl] = (_rope_lanes(qf[:, sl], rope) * (MLA_SCALE * LOG2E)).astype(bf16)
        k_ref[:, sl] = (kf[:, sl] + k_rope).astype(bf16)


def _mla_prep(a, gq, gkv, wq, wk, wv, rope):
    t = a.shape[0]
    hq = MLA_HEADS * LANES
    hv = MLA_HEADS * MLA_V
    const = lambda i: (0, 0)
    return pl.pallas_call(
        _mla_prep_kernel,
        grid=(t // PREP_TM,),
        in_specs=[
            pl.BlockSpec((PREP_TM, A_COLS), lambda i: (i, 0)),
            pl.BlockSpec((1, MLA_Q_RANK), const),
            pl.BlockSpec((1, MLA_KV_RANK), const),
            pl.BlockSpec((MLA_Q_RANK, hq), const),
            pl.BlockSpec((MLA_KV_RANK, hq), const),
            pl.BlockSpec((MLA_KV_RANK, hv), const),
            pl.BlockSpec((PREP_TM, 3 * LANES), lambda i: (i, 0)),
        ],
        out_specs=[
            pl.BlockSpec((PREP_TM, hq), lambda i: (i, 0)),
            pl.BlockSpec((PREP_TM, hq), lambda i: (i, 0)),
            pl.BlockSpec((PREP_TM, hq), lambda i: (i, 0)),
        ],
        out_shape=[
            jax.ShapeDtypeStruct((t, hq), bf16),
            jax.ShapeDtypeStruct((t, hq), bf16),
            jax.ShapeDtypeStruct((t, hq), bf16),
        ],
        compiler_params=pltpu.CompilerParams(
            dimension_semantics=("arbitrary",), vmem_limit_bytes=VMEM_LIMIT),
        name="mla_prep",
    )(a, gq, gkv, wq, wk, wv, rope)


ATT_TQ = 512
ATT_TK = 256


MLA_TK = 256
DIFF_TK = 512


def _lane_fold(x, op):
    out = x[:, :LANES]
    for u in range(1, x.shape[1] // LANES):
        out = op(out, x[:, u * LANES:(u + 1) * LANES])
    return out


def _flash_streams(streams, seq, tk):
    n_chunks = seq // tk
    state = [None] * len(streams)
    for c in range(n_chunks):
        rows = slice(c * tk, (c + 1) * tk)
        for i, (q, k_ref, k_cols, v_ref, v_cols, bias_fn) in enumerate(streams):
            s = _nt_dot(q, k_ref[rows, k_cols])
            if bias_fn is not None:
                s = s + bias_fn(c)
            m_c = jnp.max(_lane_fold(s, jnp.maximum), axis=-1, keepdims=True)
            if c == 0:
                m_new = m_c
            else:
                m_old, acc_old = state[i]
                m_new = jnp.maximum(m_old, m_c)
            p = jnp.exp2(s - m_new).astype(bf16)
            pv = jnp.dot(p, v_ref[rows, v_cols], preferred_element_type=f32)
            state[i] = (m_new, pv if c == 0 else jnp.exp2(m_old - m_new) * acc_old + pv)
    return [acc for _, acc in state]


def _mla_attn_kernel(q_ref, k_ref, v_ref, o_ref):
    tq = q_ref.shape[0]
    streams = [(q_ref[:, hh * LANES:(hh + 1) * LANES], k_ref, slice(hh * LANES, (hh + 1) * LANES),
                v_ref, slice(hh * LANES, (hh + 1) * LANES), None) for hh in range(2)]
    acc_a, acc_b = _flash_streams(streams, k_ref.shape[0], MLA_TK)
    lane = lax.broadcasted_iota(jnp.int32, (tq, LANES), 1)
    o_ref[...] = jnp.where(lane < MLA_V, acc_a / acc_a[:, MLA_V:MLA_V + 1],
                           acc_b / acc_b[:, 0:1]).astype(o_ref.dtype)


def _mla_attn(q, k, v, batch, seq):
    t = q.shape[0]
    nq = seq // ATT_TQ
    n_pairs = MLA_HEADS // 2
    return pl.pallas_call(
        _mla_attn_kernel,
        grid=(batch, n_pairs, nq),
        in_specs=[
            pl.BlockSpec((ATT_TQ, 2 * LANES), lambda b, j, i: (b * nq + i, j)),
            pl.BlockSpec((seq, 2 * LANES), lambda b, j, i: (b, j)),
            pl.BlockSpec((seq, 2 * LANES), lambda b, j, i: (b, j)),
        ],
        out_specs=pl.BlockSpec((ATT_TQ, LANES), lambda b, j, i: (b * nq + i, j)),
        out_shape=jax.ShapeDtypeStruct((t, MLA_HEADS * MLA_V), bf16),
        compiler_params=pltpu.CompilerParams(
            dimension_semantics=("arbitrary", "arbitrary", "arbitrary"), vmem_limit_bytes=VMEM_LIMIT),
        name="mla_attn",
    )(q, k, v)


DIFF_TQ = 512
DIFF_SCALE = DIFF_QK ** -0.5
N_BIAS_TILES = 5


def _diff_attn_kernel(lam_ref, q0_ref, q1_ref, k0_ref, k1_ref, v_ref, bias_ref, gsub_ref, o_ref, vaug_ref):
    qi = pl.program_id(2)
    tq = q0_ref.shape[0]
    seq = v_ref.shape[0]

    @pl.when(qi == 0)
    def _():
        for hh in range(2):
            vaug_ref[hh, :, :DIFF_V] = v_ref[:, hh * DIFF_V:(hh + 1) * DIFF_V]
            vaug_ref[hh, :, DIFF_V:] = jnp.ones((seq, DIFF_V), bf16)

    lv = lam_ref[...]
    lam_init = lv[4:5, 0:1]
    lam = (jnp.exp(jnp.sum(lv[0:1] * lv[1:2], axis=-1, keepdims=True))
           - jnp.exp(jnp.sum(lv[2:3] * lv[3:4], axis=-1, keepdims=True)) + lam_init)
    lane = lax.broadcasted_iota(jnp.int32, (tq, LANES), 1)
    q_refs = (q0_ref, q1_ref)
    k_refs = (k0_ref, k1_ref)
    all_cols = slice(0, LANES)
    tiles_per_chunk = DIFF_TK // ATT_TK
    streams = []
    for hh in range(2):
        head_lanes = (lane >= DIFF_QK) == (hh == 1)
        for m in range(2):
            q = q_refs[m][...]
            q = jnp.where(head_lanes, q, jnp.zeros_like(q))

            def bias_fn(c, m=m, hh=hh):
                def tile(r, cc):
                    offset = (c * tiles_per_chunk + cc) - (qi * (tq // ATT_TK) + r)
                    return bias_ref[m, hh, jnp.clip(offset, -2, 2) + 2]
                return jnp.concatenate(
                    [jnp.concatenate([tile(r, cc) for cc in range(tiles_per_chunk)], axis=1)
                     for r in range(tq // ATT_TK)], axis=0)

            streams.append((q, k_refs[m], all_cols, vaug_ref.at[hh], slice(0, 2 * DIFF_V), bias_fn))
    outs = [acc[:, :DIFF_V] / acc[:, DIFF_V:DIFF_V + 1] for acc in _flash_streams(streams, seq, DIFF_TK)]
    for hh in range(2):
        o = outs[2 * hh] - lam * outs[2 * hh + 1]
        o = _rms(o, gsub_ref[...]) * (1.0 - lam_init)
        o_ref[:, hh * DIFF_V:(hh + 1) * DIFF_V] = o.astype(o_ref.dtype)


def _diff_attn(lamvec, qkv, bias_tiles, gsub, batch, seq):
    t = qkv.shape[0]
    nq = seq // DIFF_TQ
    n_pairs = DIFF_HEADS // 2
    return pl.pallas_call(
        _diff_attn_kernel,
        grid=(batch, n_pairs, nq),
        in_specs=[
            pl.BlockSpec((8, DIFF_QK), lambda b, j, i: (0, 0)),
            pl.BlockSpec((DIFF_TQ, LANES), lambda b, j, i: (b * nq + i, FQ_BLK + j)),
            pl.BlockSpec((DIFF_TQ, LANES), lambda b, j, i: (b * nq + i, FQ_BLK + n_pairs + j)),
            pl.BlockSpec((seq, LANES), lambda b, j, i: (b, FK_BLK + j)),
            pl.BlockSpec((seq, LANES), lambda b, j, i: (b, FK_BLK + n_pairs + j)),
            pl.BlockSpec((seq, 2 * DIFF_V), lambda b, j, i: (b, FV_BLK // 2 + j)),
            pl.BlockSpec((2, 2, N_BIAS_TILES, ATT_TK, ATT_TK), lambda b, j, i: (0, j, 0, 0, 0)),
            pl.BlockSpec((1, DIFF_V), lambda b, j, i: (0, 0)),
        ],
        out_specs=pl.BlockSpec((DIFF_TQ, 2 * DIFF_V), lambda b, j, i: (b * nq + i, j)),
        out_shape=jax.ShapeDtypeStruct((t, DIFF_HEADS * DIFF_V), bf16),
        scratch_shapes=[pltpu.VMEM((2, seq, 2 * DIFF_V), bf16)],
        compiler_params=pltpu.CompilerParams(
            dimension_semantics=("arbitrary", "arbitrary", "arbitrary"), vmem_limit_bytes=VMEM_LIMIT),
        name="diff_attn",
    )(lamvec, qkv, qkv, qkv, qkv, qkv, bias_tiles, gsub)


DIL_TQ = 128
DIL_TW = DIL_TQ + 2 * DIL_HALF
DIL_SCALE = DIL_QK ** -0.5
DIL_UNROLL = 8


def _regroup_start_stride(r, seq, dil_from, dil_to):
    return (r % dil_from) * (seq // dil_from) + r // dil_from, dil_to // dil_from


def _residue_major(src_ref, dst_ref, stage_ref, dil_from, dil_to):
    seq, width = src_ref.shape
    sub_len = seq // dil_to
    for c in range(width // LANES):
        csl = slice(c * LANES, (c + 1) * LANES)
        stage_ref[c] = src_ref[:, csl].astype(f32)
        for r in range(dil_to):
            start, stride = _regroup_start_stride(r, seq, dil_from, dil_to)
            dst_ref[r * sub_len:(r + 1) * sub_len, csl] = (
                stage_ref[c, pl.ds(start, sub_len, stride=stride), :].astype(dst_ref.dtype))


def _regroup_state(state_ref, stage_ref, dil_from, dil_to):
    _, seq, _ = state_ref.shape
    sub_len = seq // dil_to
    for hh in range(2):
        stage_ref[0] = state_ref[hh]
        for r in range(dil_to):
            start, stride = _regroup_start_stride(r, seq, dil_from, dil_to)
            state_ref[hh, r * sub_len:(r + 1) * sub_len, :] = stage_ref[0, pl.ds(start, sub_len, stride=stride), :]


def _ungroup_state(state_ref, stage_ref, dil_from, dil_to):
    _, seq, _ = state_ref.shape
    sub_len = seq // dil_to
    for hh in range(2):
        for r in range(dil_to):
            start, stride = _regroup_start_stride(r, seq, dil_from, dil_to)
            stage_ref[0, pl.ds(start, sub_len, stride=stride), :] = state_ref[hh, r * sub_len:(r + 1) * sub_len, :]
        state_ref[hh] = stage_ref[0]


def _dil_attn_kernel(q0_ref, q1_ref, q2_ref, k0_ref, k1_ref, k2_ref, v_ref, bm_ref, o_ref,
                     stage_ref, qp_ref, kp_ref, vp_ref, m_ref, l_ref, acc_ref):
    seq = v_ref.shape[0]
    n_tiles = seq // DIL_TQ
    q_refs = (q0_ref, q1_ref, q2_ref)
    k_refs = (k0_ref, k1_ref, k2_ref)
    lane = lax.broadcasted_iota(jnp.int32, (DIL_TQ, LANES), 1)
    col = lax.broadcasted_iota(jnp.int32, (DIL_TQ, DIL_TW), 1)
    dils = [dil for _, dil in DIL_PATTERNS]
    assert dils[0] == 1 and all(b % a == 0 for a, b in zip(dils, dils[1:]))
    for g, dil in enumerate(dils):
        if g == 0:
            q_src, k_src, v_src = q_refs[g], k_refs[g], v_ref
        else:
            _residue_major(q_refs[g], qp_ref, stage_ref, 1, dil)
            _residue_major(k_refs[g], kp_ref, stage_ref, 1, dil)
            _residue_major(v_src, vp_ref, stage_ref, dils[g - 1], dil)
            for state_ref in (m_ref, l_ref, acc_ref):
                _regroup_state(state_ref, stage_ref, dils[g - 1], dil)
            q_src, k_src, v_src = qp_ref, kp_ref, vp_ref
        tiles_per_residue = n_tiles // dil

        def tile(n, carry, g=g, dil=dil, q_src=q_src, k_src=k_src, v_src=v_src,
                 tiles_per_residue=tiles_per_residue):
            base = pl.multiple_of(n * DIL_TQ, DIL_TQ)
            t_in = n % tiles_per_residue
            lo = pl.multiple_of(jnp.maximum(base - DIL_HALF, 0), DIL_HALF)
            hi = pl.multiple_of(jnp.minimum(base + DIL_TQ, seq - DIL_HALF), DIL_HALF)
            c_lo = jnp.where(t_in == 0, DIL_HALF, 0)
            c_hi = jnp.where(t_in == tiles_per_residue - 1, DIL_HALF + DIL_TQ, DIL_TW)
            valid = (col >= c_lo) & (col < c_hi)
            rows = pl.ds(base, DIL_TQ)
            qt = q_src[rows, :]
            kw = jnp.concatenate([k_src[pl.ds(lo, DIL_HALF), :], k_src[pl.ds(base, DIL_TQ), :],
                                  k_src[pl.ds(hi, DIL_HALF), :]], axis=0)
            vw = jnp.concatenate([v_src[pl.ds(lo, DIL_HALF), :], v_src[pl.ds(base, DIL_TQ), :],
                                  v_src[pl.ds(hi, DIL_HALF), :]], axis=0)
            for hh in range(2):
                qh = jnp.where((lane >= DIL_QK) == (hh == 1), qt * DIL_SCALE, jnp.zeros_like(qt))
                hsl = slice(hh * DIL_V, (hh + 1) * DIL_V)
                s = _nt_dot(qh, kw) + bm_ref[g, hh]
                s = jnp.where(valid, s, NEG)
                m_t = jnp.max(s, axis=-1, keepdims=True)
                p = jnp.exp(s - m_t)
                l_t = jnp.broadcast_to(jnp.sum(p, axis=-1, keepdims=True), (DIL_TQ, DIL_V))
                u_t = jnp.dot(p.astype(bf16), vw[:, hsl], preferred_element_type=f32)
                m_t = jnp.broadcast_to(m_t, (DIL_TQ, DIL_V))
                if g == 0:
                    m_ref[hh, rows, :] = m_t
                    l_ref[hh, rows, :] = l_t
                    acc_ref[hh, rows, :] = u_t
                else:
                    m_old = m_ref[hh, rows, :]
                    m_new = jnp.maximum(m_old, m_t)
                    a_old = jnp.exp(m_old - m_new)
                    a_t = jnp.exp(m_t - m_new)
                    m_ref[hh, rows, :] = m_new
                    l_ref[hh, rows, :] = a_old * l_ref[hh, rows, :] + a_t * l_t
                    acc_ref[hh, rows, :] = a_old * acc_ref[hh, rows, :] + a_t * u_t
            return carry

        def tile_group(i, carry, tile=tile):
            for u in range(DIL_UNROLL):
                tile(i * DIL_UNROLL + u, carry)
            return carry

        lax.fori_loop(0, n_tiles // DIL_UNROLL, tile_group, 0)
    for hh in range(2):
        acc_ref[hh] = acc_ref[hh] / l_ref[hh]
    for g in range(len(dils) - 1, 0, -1):
        _ungroup_state(acc_ref, stage_ref, dils[g - 1], dils[g])
    for hh in range(2):
        o_ref[:, hh * DIL_V:(hh + 1) * DIL_V] = acc_ref[hh].astype(o_ref.dtype)


def _dil_attn(qkv, biasmask, batch, seq):
    t = qkv.shape[0]
    n_pairs = DIL_HEADS // 2
    pair_w = 2 * DIL_V

    def qk_spec(blk0, g):
        return pl.BlockSpec((seq, LANES), lambda b, j: (b, blk0 + g * n_pairs + j))

    state = pltpu.VMEM((2, seq, DIL_V), f32)
    return pl.pallas_call(
        _dil_attn_kernel,
        grid=(batch, n_pairs),
        in_specs=[qk_spec(DQ_BLK, g) for g in range(DIL_GROUPS)]
        + [qk_spec(DK_BLK, g) for g in range(DIL_GROUPS)]
        + [pl.BlockSpec((seq, pair_w), lambda b, j: (b, DV_BLK // 2 + j)),
           pl.BlockSpec((DIL_GROUPS, 2, DIL_TQ, DIL_TW), lambda b, j: (0, j, 0, 0))],
        out_specs=pl.BlockSpec((seq, pair_w), lambda b, j: (b, j)),
        out_shape=jax.ShapeDtypeStruct((t, DIL_HEADS * DIL_V), bf16),
        scratch_shapes=[
            pltpu.VMEM((2, seq, LANES), f32),
            pltpu.VMEM((seq, LANES), bf16),
            pltpu.VMEM((seq, LANES), bf16),
            pltpu.VMEM((seq, pair_w), bf16),
            state, state, state,
        ],
        compiler_params=pltpu.CompilerParams(
            dimension_semantics=("arbitrary", "arbitrary"), vmem_limit_bytes=VMEM_LIMIT_DIL),
        name="dil_attn",
    )(qkv, qkv, qkv, qkv, qkv, qkv, qkv, biasmask)


MERGE_TM = 512


def _merge_kernel(x_ref, oa_ref, ob_ref, oc_ref, gz0_ref, gz1_ref, gz2_ref, wb_ref, wo_ref, g_ref, out_ref):
    branches = (oa_ref[...], ob_ref[...], oc_ref[...])
    gate_refs = (gz0_ref, gz1_ref, gz2_ref)
    merged = None
    for n in range(N_BRANCH):
        y = jnp.dot(branches[n], wb_ref[n], preferred_element_type=f32)
        y = jax.nn.sigmoid(gate_refs[n][...].astype(f32)) * y
        merged = y if merged is None else merged + y
    y = jnp.dot(merged.astype(bf16), wo_ref[...], preferred_element_type=f32)
    out_ref[...] = x_ref[...] + _rms(y, g_ref[...])


def _merge(x, oa, ob, oc, z, wb, wo, g):
    t = x.shape[0]
    row = lambda i: (i, 0)
    bw = pl.BlockSpec((MERGE_TM, BRANCH_W), row)
    gate_specs = [pl.BlockSpec((MERGE_TM, D_MODEL), lambda i, n=n: (i, GATE_BLK + n)) for n in range(N_BRANCH)]
    return pl.pallas_call(
        _merge_kernel,
        grid=(t // MERGE_TM,),
        in_specs=[
            pl.BlockSpec((MERGE_TM, D_MODEL), row),
            bw, bw, bw,
            *gate_specs,
            pl.BlockSpec((N_BRANCH, BRANCH_W, D_MODEL), lambda i: (0, 0, 0)),
            pl.BlockSpec((D_MODEL, D_MODEL), lambda i: (0, 0)),
            pl.BlockSpec((1, D_MODEL), lambda i: (0, 0)),
        ],
        out_specs=pl.BlockSpec((MERGE_TM, D_MODEL), row),
        out_shape=jax.ShapeDtypeStruct((t, D_MODEL), f32),
        compiler_params=pltpu.CompilerParams(
            dimension_semantics=("arbitrary",), vmem_limit_bytes=VMEM_LIMIT),
        name="merge",
    )(x, oa, ob, oc, z, z, z, wb, wo, g)


MLP_TM = 1024
MLP_TF = 1024


def _mlp_kernel(x_ref, gpre_ref, wup_ref, wdn_ref, gpost_ref, out_ref):
    x = x_ref[...]
    h = _rms(x, gpre_ref[...]).astype(bf16)
    acc = None
    for f in range(D_FF // MLP_TF):
        fsl = slice(f * MLP_TF, (f + 1) * MLP_TF)
        u = jnp.maximum(jnp.dot(h, wup_ref[:, fsl], preferred_element_type=f32), 0.0)
        y = jnp.dot((u * u).astype(bf16), wdn_ref[fsl, :], preferred_element_type=f32)
        acc = y if acc is None else acc + y
    out_ref[...] = x + _rms(acc, gpost_ref[...])


def _mlp(x, gpre, wup, wdn, gpost):
    t = x.shape[0]
    resident = pl.Buffered(1)
    return pl.pallas_call(
        _mlp_kernel,
        grid=(t // MLP_TM,),
        in_specs=[
            pl.BlockSpec((MLP_TM, D_MODEL), lambda i: (i, 0)),
            pl.BlockSpec((1, D_MODEL), lambda i: (0, 0)),
            pl.BlockSpec((D_MODEL, D_FF), lambda i: (0, 0), pipeline_mode=resident),
            pl.BlockSpec((D_FF, D_MODEL), lambda i: (0, 0), pipeline_mode=resident),
            pl.BlockSpec((1, D_MODEL), lambda i: (0, 0)),
        ],
        out_specs=pl.BlockSpec((MLP_TM, D_MODEL), lambda i: (i, 0)),
        out_shape=jax.ShapeDtypeStruct((t, D_MODEL), f32),
        compiler_params=pltpu.CompilerParams(
            dimension_semantics=("arbitrary",), vmem_limit_bytes=VMEM_LIMIT_DIL),
        name="mlp",
    )(x, gpre, wup, wdn, gpost)


def _rel_bucket(rel):
    nb = REL_BUCKETS // 2
    max_exact = nb // 2
    ret = jnp.where(rel > 0, nb, 0)
    n = jnp.abs(rel)
    large = max_exact + (jnp.log(jnp.maximum(n, 1).astype(f32) / max_exact)
                         / math.log(REL_MAX_DIST / max_exact) * (nb - max_exact)).astype(jnp.int32)
    large = jnp.minimum(large, nb - 1)
    return ret + jnp.where(n < max_exact, n, large)


def _bias_lookup(tab, bucket):
    one_hot = jax.nn.one_hot(bucket, REL_BUCKETS, dtype=f32)
    return jnp.einsum('...b,bc->...c', one_hot, tab.astype(f32), precision=lax.Precision.HIGHEST)


def _pad_in_weights(w_in):
    sizes = (MLA_Q_RANK, MLA_KV_RANK, MLA_ROPE)
    c_q = w_in[..., :sizes[0]]
    c_kv = w_in[..., sizes[0]:sizes[0] + sizes[1]]
    k_r = w_in[..., sizes[0] + sizes[1]:sum(sizes)]
    rest = w_in[..., sum(sizes):]
    fq0 = (FQ_BLK - DQ_BLK) * LANES
    fq1 = (FK_BLK - DQ_BLK) * LANES
    z = lambda n: jnp.zeros(w_in.shape[:-1] + (n,), bf16)
    pieces = [c_q, c_kv, z(MLA_NOPE), k_r, z(LANES - MLA_NOPE - MLA_ROPE),
              rest[..., :fq0], rest[..., fq0:fq1] * (DIFF_SCALE * LOG2E), rest[..., fq1:]]
    return jnp.concatenate([p.astype(bf16) for p in pieces], axis=-1)


def _pad_mla_weights(w_uq, w_ukv):
    depth = w_uq.shape[0]
    wq = jnp.pad(w_uq, ((0, 0), (0, 0), (0, 0), (0, LANES - MLA_NOPE - MLA_ROPE)))
    wk = jnp.pad(w_ukv[..., :MLA_NOPE], ((0, 0), (0, 0), (0, 0), (0, LANES - MLA_NOPE)))
    wv = w_ukv[..., MLA_NOPE:]
    return (wq.reshape(depth, MLA_Q_RANK, MLA_HEADS * LANES).astype(bf16),
            wk.reshape(depth, MLA_KV_RANK, MLA_HEADS * LANES).astype(bf16),
            wv.reshape(depth, MLA_KV_RANK, MLA_HEADS * MLA_V).astype(bf16))


def _rope_tables(positions):
    inv = ROPE_BASE ** (-jnp.arange(ROPE_HALF, dtype=f32) / ROPE_HALF)
    ang = positions.reshape(-1).astype(f32)[:, None] * inv
    cos_sin = jnp.concatenate([jnp.cos(ang), jnp.sin(ang)], axis=-1)
    place = np.zeros((2 * ROPE_HALF, 3 * LANES), np.float32)
    base = np.zeros((1, 3 * LANES), np.float32)
    base[0, :MLA_NOPE] = 1.0
    for i in range(ROPE_HALF):
        place[i, MLA_NOPE + i] = 1.0
        place[i, MLA_NOPE + ROPE_HALF + i] = 1.0
        place[ROPE_HALF + i, LANES + MLA_NOPE + ROPE_HALF + i] = 1.0
        place[ROPE_HALF + i, 2 * LANES + MLA_NOPE + i] = -1.0
    return jnp.dot(cos_sin, jnp.asarray(place), precision=lax.Precision.HIGHEST) + jnp.asarray(base)


def _diff_bias_tiles(rel_bias):
    tab = rel_bias[:, DIL_BIAS_COLS:]
    i = jnp.arange(ATT_TK)[:, None]
    j = jnp.arange(ATT_TK)[None, :]
    offs = jnp.array([-2, -1, 0, 1, 2])[:, None, None] * ATT_TK
    bucket = _rel_bucket(offs + j - i)
    tiles = jnp.transpose(_bias_lookup(tab, bucket), (3, 0, 1, 2)) * LOG2E
    return tiles.reshape(2, DIFF_HEADS, N_BIAS_TILES, ATT_TK, ATT_TK)


def _dil_bias_masks(rel_bias):
    i = jnp.arange(DIL_TQ)[:, None]
    c = jnp.arange(DIL_TW)[None, :]
    rel = c - DIL_HALF - i
    out = []
    for g, (_, dil) in enumerate(DIL_PATTERNS):
        tab = rel_bias[:, g * DIL_HEADS:(g + 1) * DIL_HEADS]
        b = jnp.transpose(_bias_lookup(tab, _rel_bucket(rel * dil)), (2, 0, 1))
        out.append(jnp.where((jnp.abs(rel) <= DIL_HALF)[None], b, NEG))
    return jnp.stack(out, axis=0)


def kernel(x, positions, rel_bias, g_mix_pre, w_in, g_q, w_uq, g_kv, w_ukv, lam_q1, lam_k1, lam_q2, lam_k2,
           g_diff_sub, w_branch, w_out, g_mix_post, g_mlp_pre, w_up, w_down, g_mlp_post):
    batch, seq, d = x.shape
    depth = w_in.shape[0]
    assert d == D_MODEL and seq % 1024 == 0 and seq // DIL_PATTERNS[-1][1] >= 2 * DIL_TQ

    w_in_p = _pad_in_weights(w_in)
    wq_p, wk_p, wv_p = _pad_mla_weights(w_uq, w_ukv)
    wb = w_branch.astype(bf16)
    wo = w_out.astype(bf16)
    wup = w_up.astype(bf16)
    wdn = w_down.astype(bf16)
    rope = _rope_tables(positions)
    diff_bias = _diff_bias_tiles(rel_bias)
    dil_bias = _dil_bias_masks(rel_bias)

    lam_init = jnp.array([0.8 - 0.6 * math.exp(-0.3 * l) for l in range(depth)], f32)
    lam_row = jnp.zeros((depth, 1, DIFF_QK), f32).at[:, 0, 0].set(lam_init)
    lamvec = jnp.concatenate([lam_q1[:, None], lam_k1[:, None], lam_q2[:, None], lam_k2[:, None],
                              lam_row, jnp.zeros((depth, 3, DIFF_QK), f32)], axis=1).astype(f32)

    row = lambda v: v.reshape(1, -1).astype(f32)
    xt = x.reshape(batch * seq, d)
    for l in range(depth):
        z = _in_proj(xt, row(g_mix_pre[l]), w_in_p[l])
        q, k, v = _mla_prep(z, row(g_q[l]), row(g_kv[l]), wq_p[l], wk_p[l], wv_p[l], rope)
        o_a = _mla_attn(q, k, v, batch, seq)
        o_b = _dil_attn(z, dil_bias, batch, seq)
        o_c = _diff_attn(lamvec[l], z, diff_bias, row(g_diff_sub[l]), batch, seq)
        xt = _merge(xt, o_a, o_b, o_c, z, wb[l], wo[l], row(g_mix_post[l]))
        xt = _mlp(xt, row(g_mlp_pre[l]), wup[l], wdn[l], row(g_mlp_post[l]))
    return xt.reshape(batch, seq, d)
```

```python
import math

import jax
import jax.numpy as jnp
import numpy as np
from jax import lax
from jax.experimental import pallas as pl
from jax.experimental.pallas import tpu as pltpu

f32 = jnp.float32
bf16 = jnp.bfloat16

D_MODEL = 1024
MLA_HEADS = 8
MLA_Q_RANK = 256
MLA_KV_RANK = 128
MLA_NOPE = 64
MLA_ROPE = 32
MLA_V = 64
ROPE_BASE = 10000.0
DIL_PATTERNS = ((128, 1), (512, 4), (2048, 16))
DIL_GROUPS = 3
DIL_HEADS = 4
DIL_QK = 64
DIL_V = 128
DIL_HALF = 64
DIFF_HEADS = 4
DIFF_QK = 64
DIFF_V = 128
REL_BUCKETS = 32
REL_MAX_DIST = 128
DIL_BIAS_COLS = DIL_GROUPS * DIL_HEADS
N_BRANCH = 3
BRANCH_W = 512
D_FF = 4 * D_MODEL
EPS = 1e-6
NEG = -1e30

LANES = 128
A_COLS = 512
QKV_COLS = 3584
GATE_COLS = N_BRANCH * D_MODEL
IN_PAD_COLS = A_COLS + QKV_COLS + GATE_COLS
DQ_BLK, DK_BLK, DV_BLK, FQ_BLK, FK_BLK, FV_BLK = (A_COLS // LANES + o for o in (0, 6, 12, 16, 20, 24))
GATE_BLK = (A_COLS + QKV_COLS) // D_MODEL

VMEM_LIMIT = 48 * 1024 * 1024
VMEM_LIMIT_LARGE = 56 * 1024 * 1024


def _rms(x, g):
    return x * lax.rsqrt(jnp.mean(x * x, axis=-1, keepdims=True) + EPS) * g


def _nt_dot(a, b):
    return lax.dot_general(a, b, (((1,), (1,)), ((), ())), preferred_element_type=f32)


IN_TM = 1024
IN_TN = 3584


def _in_proj_kernel(x_ref, g_ref, w_ref, z_ref, h_ref):
    @pl.when(pl.program_id(1) == 0)
    def _():
        h_ref[...] = _rms(x_ref[...], g_ref[...]).astype(bf16)

    z_ref[...] = jnp.dot(h_ref[...], w_ref[...], preferred_element_type=f32).astype(z_ref.dtype)


def _in_proj(x, g, w):
    t = x.shape[0]
    return pl.pallas_call(
        _in_proj_kernel,
        grid=(t // IN_TM, IN_PAD_COLS // IN_TN),
        in_specs=[
            pl.BlockSpec((IN_TM, D_MODEL), lambda i, j: (i, 0)),
            pl.BlockSpec((1, D_MODEL), lambda i, j: (0, 0)),
            pl.BlockSpec((D_MODEL, IN_TN), lambda i, j: (0, j)),
        ],
        out_specs=pl.BlockSpec((IN_TM, IN_TN), lambda i, j: (i, j)),
        out_shape=jax.ShapeDtypeStruct((t, IN_PAD_COLS), bf16),
        scratch_shapes=[pltpu.VMEM((IN_TM, D_MODEL), bf16)],
        compiler_params=pltpu.CompilerParams(
            dimension_semantics=("arbitrary", "arbitrary"), vmem_limit_bytes=VMEM_LIMIT),
        name="in_proj",
    )(x, g, w)


PREP_TM = 1024
MLA_SCALE = (MLA_NOPE + MLA_ROPE) ** -0.5
LOG2E = math.log2(math.e)
ROPE_HALF = MLA_ROPE // 2


def _rope_lanes(x, rope):
    c = rope[:, 0:LANES]
    sa = rope[:, LANES:2 * LANES]
    sb = rope[:, 2 * LANES:3 * LANES]
    return (x * c + pltpu.roll(x, ROPE_HALF, 1) * sa + pltpu.roll(x, LANES - ROPE_HALF, 1) * sb)


def _mla_prep_kernel(a_ref, gq_ref, gkv_ref, wq_ref, wk_ref, wv_ref, rope_ref, q_ref, k_ref, v_ref):
    a = a_ref[...].astype(f32)
    rope = rope_ref[...]
    cq = _rms(a[:, :MLA_Q_RANK], gq_ref[...]).astype(bf16)
    ckv = _rms(a[:, MLA_Q_RANK:MLA_Q_RANK + MLA_KV_RANK], gkv_ref[...]).astype(bf16)
    k_rope = _rope_lanes(a[:, A_COLS - LANES:], rope)
    qf = jnp.dot(cq, wq_ref[...], preferred_element_type=f32)
    kf = jnp.dot(ckv, wk_ref[...], preferred_element_type=f32)
    vf = jnp.dot(ckv, wv_ref[...], preferred_element_type=f32).astype(bf16)
    ones = jnp.ones((vf.shape[0], LANES), bf16)
    lane = lax.broadcasted_iota(jnp.int32, ones.shape, 1)
    for j in range(MLA_HEADS // 2):
        pair = vf[:, j * LANES:(j + 1) * LANES]
        v_ref[:, 2 * j * LANES:(2 * j + 1) * LANES] = jnp.where(lane < MLA_V, pair, ones)
        v_ref[:, (2 * j + 1) * LANES:(2 * j + 2) * LANES] = jnp.where(lane >= MLA_V, pair, ones)
    for h in range(MLA_HEADS):
        sl = slice(h * LANES, (h + 1) * LANES)
        q_ref[:, sl] = (_rope_lanes(qf[:, sl], rope) * (MLA_SCALE * LOG2E)).astype(bf16)
        k_ref[:, sl] = (kf[:, sl] + k_rope).astype(bf16)


def _mla_prep(a, gq, gkv, wq, wk, wv, rope):
    t = a.shape[0]
    hq = MLA_HEADS * LANES
    hv = MLA_HEADS * MLA_V
    const = lambda i: (0, 0)
    return pl.pallas_call(
        _mla_prep_kernel,
        grid=(t // PREP_TM,),
        in_specs=[
            pl.BlockSpec((PREP_TM, A_COLS), lambda i: (i, 0)),
            pl.BlockSpec((1, MLA_Q_RANK), const),
            pl.BlockSpec((1, MLA_KV_RANK), const),
            pl.BlockSpec((MLA_Q_RANK, hq), const),
            pl.BlockSpec((MLA_KV_RANK, hq), const),
            pl.BlockSpec((MLA_KV_RANK, hv), const),
            pl.BlockSpec((PREP_TM, 3 * LANES), lambda i: (i, 0)),
        ],
        out_specs=[
            pl.BlockSpec((PREP_TM, hq), lambda i: (i, 0)),
            pl.BlockSpec((PREP_TM, hq), lambda i: (i, 0)),
            pl.BlockSpec((PREP_TM, hq), lambda i: (i, 0)),
        ],
        out_shape=[
            jax.ShapeDtypeStruct((t, hq), bf16),
            jax.ShapeDtypeStruct((t, hq), bf16),
            jax.ShapeDtypeStruct((t, hq), bf16),
        ],
        compiler_params=pltpu.CompilerParams(
            dimension_semantics=("arbitrary",), vmem_limit_bytes=VMEM_LIMIT),
        name="mla_prep",
    )(a, gq, gkv, wq, wk, wv, rope)


ATT_TQ = 512
ATT_TK = 256


MLA_TK = 256
DIFF_TK = 512


def _lane_fold(x, op):
    out = x[:, :LANES]
    for u in range(1, x.shape[1] // LANES):
        out = op(out, x[:, u * LANES:(u + 1) * LANES])
    return out


def _flash_streams(streams, seq, tk):
    n_chunks = seq // tk
    state = [None] * len(streams)
    for c in range(n_chunks):
        rows = slice(c * tk, (c + 1) * tk)
        for i, (q, k_ref, k_cols, v_ref, v_cols, bias_fn) in enumerate(streams):
            s = _nt_dot(q, k_ref[rows, k_cols])
            if bias_fn is not None:
                s = s + bias_fn(c)
            m_c = jnp.max(_lane_fold(s, jnp.maximum), axis=-1, keepdims=True)
            if c == 0:
                m_new = m_c
            else:
                m_old, acc_old = state[i]
                m_new = jnp.maximum(m_old, m_c)
            p = jnp.exp2(s - m_new).astype(bf16)
            pv = jnp.dot(p, v_ref[rows, v_cols], preferred_element_type=f32)
            state[i] = (m_new, pv if c == 0 else jnp.exp2(m_old - m_new) * acc_old + pv)
    return [acc for _, acc in state]


MLA_STEP_HEADS = 2


def _mla_attn_kernel(q_ref, k_ref, v_ref, o_ref):
    tq = q_ref.shape[0]
    streams = [(q_ref[:, h * LANES:(h + 1) * LANES], k_ref, slice(h * LANES, (h + 1) * LANES),
                v_ref, slice(h * LANES, (h + 1) * LANES), None) for h in range(MLA_STEP_HEADS)]
    accs = _flash_streams(streams, k_ref.shape[0], MLA_TK)
    lane = lax.broadcasted_iota(jnp.int32, (tq, LANES), 1)
    for j in range(MLA_STEP_HEADS // 2):
        acc_a, acc_b = accs[2 * j], accs[2 * j + 1]
        o_ref[:, j * LANES:(j + 1) * LANES] = jnp.where(
            lane < MLA_V, acc_a / acc_a[:, MLA_V:MLA_V + 1], acc_b / acc_b[:, 0:1]).astype(o_ref.dtype)


def _mla_attn(q, k, v, batch, seq):
    t = q.shape[0]
    nq = seq // ATT_TQ
    n_groups = MLA_HEADS // MLA_STEP_HEADS
    w = MLA_STEP_HEADS * LANES
    return pl.pallas_call(
        _mla_attn_kernel,
        grid=(batch, n_groups, nq),
        in_specs=[
            pl.BlockSpec((ATT_TQ, w), lambda b, j, i: (b * nq + i, j)),
            pl.BlockSpec((seq, w), lambda b, j, i: (b, j)),
            pl.BlockSpec((seq, w), lambda b, j, i: (b, j)),
        ],
        out_specs=pl.BlockSpec((ATT_TQ, w // 2), lambda b, j, i: (b * nq + i, j)),
        out_shape=jax.ShapeDtypeStruct((t, MLA_HEADS * MLA_V), bf16),
        compiler_params=pltpu.CompilerParams(
            dimension_semantics=("arbitrary", "arbitrary", "arbitrary"), vmem_limit_bytes=VMEM_LIMIT),
        name="mla_attn",
    )(q, k, v)


DIFF_TQ = 512
DIFF_SCALE = DIFF_QK ** -0.5
N_BIAS_TILES = 5


def _diff_attn_kernel(lam_ref, q0_ref, q1_ref, k0_ref, k1_ref, v_ref, bias_ref, gsub_ref, o_ref, vaug_ref):
    qi = pl.program_id(2)
    tq = q0_ref.shape[0]
    seq = v_ref.shape[0]

    @pl.when(qi == 0)
    def _():
        for hh in range(2):
            vaug_ref[hh, :, :DIFF_V] = v_ref[:, hh * DIFF_V:(hh + 1) * DIFF_V]
            vaug_ref[hh, :, DIFF_V:] = jnp.ones((seq, DIFF_V), bf16)

    lv = lam_ref[...]
    lam_init = lv[4:5, 0:1]
    lam = (jnp.exp(jnp.sum(lv[0:1] * lv[1:2], axis=-1, keepdims=True))
           - jnp.exp(jnp.sum(lv[2:3] * lv[3:4], axis=-1, keepdims=True)) + lam_init)
    lane = lax.broadcasted_iota(jnp.int32, (tq, LANES), 1)
    q_refs = (q0_ref, q1_ref)
    k_refs = (k0_ref, k1_ref)
    all_cols = slice(0, LANES)
    tiles_per_chunk = DIFF_TK // ATT_TK
    streams = []
    for hh in range(2):
        head_lanes = (lane >= DIFF_QK) == (hh == 1)
        for m in range(2):
            q = q_refs[m][...]
            q = jnp.where(head_lanes, q, jnp.zeros_like(q))

            def bias_fn(c, m=m, hh=hh):
                def tile(r, cc):
                    offset = (c * tiles_per_chunk + cc) - (qi * (tq // ATT_TK) + r)
                    return bias_ref[m, hh, jnp.clip(offset, -2, 2) + 2]
                return jnp.concatenate(
                    [jnp.concatenate([tile(r, cc) for cc in range(tiles_per_chunk)], axis=1)
                     for r in range(tq // ATT_TK)], axis=0)

            streams.append((q, k_refs[m], all_cols, vaug_ref.at[hh], slice(0, 2 * DIFF_V), bias_fn))
    outs = [acc[:, :DIFF_V] / acc[:, DIFF_V:DIFF_V + 1] for acc in _flash_streams(streams, seq, DIFF_TK)]
    for hh in range(2):
        o = outs[2 * hh] - lam * outs[2 * hh + 1]
        o = _rms(o, gsub_ref[...]) * (1.0 - lam_init)
        o_ref[:, hh * DIFF_V:(hh + 1) * DIFF_V] = o.astype(o_ref.dtype)


def _diff_attn(lamvec, qkv, bias_tiles, gsub, batch, seq):
    t = qkv.shape[0]
    nq = seq // DIFF_TQ
    n_pairs = DIFF_HEADS // 2
    return pl.pallas_call(
        _diff_attn_kernel,
        grid=(batch, n_pairs, nq),
        in_specs=[
            pl.BlockSpec((8, DIFF_QK), lambda b, j, i: (0, 0)),
            pl.BlockSpec((DIFF_TQ, LANES), lambda b, j, i: (b * nq + i, FQ_BLK + j)),
            pl.BlockSpec((DIFF_TQ, LANES), lambda b, j, i: (b * nq + i, FQ_BLK + n_pairs + j)),
            pl.BlockSpec((seq, LANES), lambda b, j, i: (b, FK_BLK + j)),
            pl.BlockSpec((seq, LANES), lambda b, j, i: (b, FK_BLK + n_pairs + j)),
            pl.BlockSpec((seq, 2 * DIFF_V), lambda b, j, i: (b, FV_BLK // 2 + j)),
            pl.BlockSpec((2, 2, N_BIAS_TILES, ATT_TK, ATT_TK), lambda b, j, i: (0, j, 0, 0, 0)),
            pl.BlockSpec((1, DIFF_V), lambda b, j, i: (0, 0)),
        ],
        out_specs=pl.BlockSpec((DIFF_TQ, 2 * DIFF_V), lambda b, j, i: (b * nq + i, j)),
        out_shape=jax.ShapeDtypeStruct((t, DIFF_HEADS * DIFF_V), bf16),
        scratch_shapes=[pltpu.VMEM((2, seq, 2 * DIFF_V), bf16)],
        compiler_params=pltpu.CompilerParams(
            dimension_semantics=("arbitrary", "arbitrary", "arbitrary"), vmem_limit_bytes=VMEM_LIMIT),
        name="diff_attn",
    )(lamvec, qkv, qkv, qkv, qkv, qkv, bias_tiles, gsub)


DIL_TQ = 128
DIL_TW = DIL_TQ + 2 * DIL_HALF
DIL_SCALE = DIL_QK ** -0.5
DIL_UNROLL = 16


def _regroup_start_stride(r, seq, dil_from, dil_to):
    return (r % dil_from) * (seq // dil_from) + r // dil_from, dil_to // dil_from


def _residue_major(src_ref, dst_ref, stage_ref, dil_from, dil_to):
    seq, width = src_ref.shape
    sub_len = seq // dil_to
    for c in range(width // LANES):
        csl = slice(c * LANES, (c + 1) * LANES)
        stage_ref[c] = src_ref[:, csl].astype(f32)
        for r in range(dil_to):
            start, stride = _regroup_start_stride(r, seq, dil_from, dil_to)
            dst_ref[r * sub_len:(r + 1) * sub_len, csl] = (
                stage_ref[c, pl.ds(start, sub_len, stride=stride), :].astype(dst_ref.dtype))


def _regroup_state(state_ref, stage_ref, dil_from, dil_to):
    _, seq, _ = state_ref.shape
    sub_len = seq // dil_to
    for hh in range(2):
        stage_ref[0] = state_ref[hh]
        for r in range(dil_to):
            start, stride = _regroup_start_stride(r, seq, dil_from, dil_to)
            state_ref[hh, r * sub_len:(r + 1) * sub_len, :] = stage_ref[0, pl.ds(start, sub_len, stride=stride), :]


def _ungroup_state(state_ref, stage_ref, dil_from, dil_to):
    _, seq, _ = state_ref.shape
    sub_len = seq // dil_to
    for hh in range(2):
        for r in range(dil_to):
            start, stride = _regroup_start_stride(r, seq, dil_from, dil_to)
            stage_ref[0, pl.ds(start, sub_len, stride=stride), :] = state_ref[hh, r * sub_len:(r + 1) * sub_len, :]
        state_ref[hh] = stage_ref[0]


def _dil_attn_kernel(q0_ref, q1_ref, q2_ref, k0_ref, k1_ref, k2_ref, v_ref, bm_ref, o_ref,
                     stage_ref, qp_ref, kp_ref, vp_ref, m_ref, l_ref, acc_ref):
    seq = v_ref.shape[0]
    n_tiles = seq // DIL_TQ
    q_refs = (q0_ref, q1_ref, q2_ref)
    k_refs = (k0_ref, k1_ref, k2_ref)
    lane = lax.broadcasted_iota(jnp.int32, (DIL_TQ, LANES), 1)
    col = lax.broadcasted_iota(jnp.int32, (DIL_TQ, DIL_TW), 1)
    dils = [dil for _, dil in DIL_PATTERNS]
    assert dils[0] == 1 and all(b % a == 0 for a, b in zip(dils, dils[1:]))
    for g, dil in enumerate(dils):
        if g == 0:
            q_src, k_src, v_src = q_refs[g], k_refs[g], v_ref
        else:
            _residue_major(q_refs[g], qp_ref, stage_ref, 1, dil)
            _residue_major(k_refs[g], kp_ref, stage_ref, 1, dil)
            _residue_major(v_src, vp_ref, stage_ref, dils[g - 1], dil)
            for state_ref in (m_ref, l_ref, acc_ref):
                _regroup_state(state_ref, stage_ref, dils[g - 1], dil)
            q_src, k_src, v_src = qp_ref, kp_ref, vp_ref
        tiles_per_residue = n_tiles // dil

        def tile(n, carry, g=g, dil=dil, q_src=q_src, k_src=k_src, v_src=v_src,
                 tiles_per_residue=tiles_per_residue):
            base = pl.multiple_of(n * DIL_TQ, DIL_TQ)
            t_in = n % tiles_per_residue
            lo = pl.multiple_of(jnp.maximum(base - DIL_HALF, 0), DIL_HALF)
            hi = pl.multiple_of(jnp.minimum(base + DIL_TQ, seq - DIL_HALF), DIL_HALF)
            c_lo = jnp.where(t_in == 0, DIL_HALF, 0)
            c_hi = jnp.where(t_in == tiles_per_residue - 1, DIL_HALF + DIL_TQ, DIL_TW)
            valid = (col >= c_lo) & (col < c_hi)
            rows = pl.ds(base, DIL_TQ)
            qt = q_src[rows, :]
            kw = jnp.concatenate([k_src[pl.ds(lo, DIL_HALF), :], k_src[pl.ds(base, DIL_TQ), :],
                                  k_src[pl.ds(hi, DIL_HALF), :]], axis=0)
            vw = jnp.concatenate([v_src[pl.ds(lo, DIL_HALF), :], v_src[pl.ds(base, DIL_TQ), :],
                                  v_src[pl.ds(hi, DIL_HALF), :]], axis=0)
            for hh in range(2):
                qh = jnp.where((lane >= DIL_QK) == (hh == 1), qt * DIL_SCALE, jnp.zeros_like(qt))
                hsl = slice(hh * DIL_V, (hh + 1) * DIL_V)
                s = _nt_dot(qh, kw) + bm_ref[g, hh]
                s = jnp.where(valid, s, NEG)
                m_t = jnp.max(s, axis=-1, keepdims=True)
                p = jnp.exp(s - m_t)
                l_t = jnp.broadcast_to(jnp.sum(p, axis=-1, keepdims=True), (DIL_TQ, DIL_V))
                u_t = jnp.dot(p.astype(bf16), vw[:, hsl], preferred_element_type=f32)
                m_t = jnp.broadcast_to(m_t, (DIL_TQ, DIL_V))
                if g == 0:
                    m_ref[hh, rows, :] = m_t
                    l_ref[hh, rows, :] = l_t
                    acc_ref[hh, rows, :] = u_t
                else:
                    m_old = m_ref[hh, rows, :]
                    m_new = jnp.maximum(m_old, m_t)
                    a_old = jnp.exp(m_old - m_new)
                    a_t = jnp.exp(m_t - m_new)
                    m_ref[hh, rows, :] = m_new
                    l_ref[hh, rows, :] = a_old * l_ref[hh, rows, :] + a_t * l_t
                    acc_ref[hh, rows, :] = a_old * acc_ref[hh, rows, :] + a_t * u_t
            return carry

        def tile_group(i, carry, tile=tile):
            for u in range(DIL_UNROLL):
                tile(i * DIL_UNROLL + u, carry)
            return carry

        lax.fori_loop(0, n_tiles // DIL_UNROLL, tile_group, 0)
    for hh in range(2):
        acc_ref[hh] = acc_ref[hh] / l_ref[hh]
    for g in range(len(dils) - 1, 0, -1):
        _ungroup_state(acc_ref, stage_ref, dils[g - 1], dils[g])
    for hh in range(2):
        o_ref[:, hh * DIL_V:(hh + 1) * DIL_V] = acc_ref[hh].astype(o_ref.dtype)


def _dil_attn(qkv, biasmask, batch, seq):
    t = qkv.shape[0]
    n_pairs = DIL_HEADS // 2
    pair_w = 2 * DIL_V

    def qk_spec(blk0, g):
        return pl.BlockSpec((seq, LANES), lambda b, j: (b, blk0 + g * n_pairs + j))

    state = pltpu.VMEM((2, seq, DIL_V), f32)
    return pl.pallas_call(
        _dil_attn_kernel,
        grid=(batch, n_pairs),
        in_specs=[qk_spec(DQ_BLK, g) for g in range(DIL_GROUPS)]
        + [qk_spec(DK_BLK, g) for g in range(DIL_GROUPS)]
        + [pl.BlockSpec((seq, pair_w), lambda b, j: (b, DV_BLK // 2 + j)),
           pl.BlockSpec((DIL_GROUPS, 2, DIL_TQ, DIL_TW), lambda b, j: (0, j, 0, 0))],
        out_specs=pl.BlockSpec((seq, pair_w), lambda b, j: (b, j)),
        out_shape=jax.ShapeDtypeStruct((t, DIL_HEADS * DIL_V), bf16),
        scratch_shapes=[
            pltpu.VMEM((2, seq, LANES), f32),
            pltpu.VMEM((seq, LANES), bf16),
            pltpu.VMEM((seq, LANES), bf16),
            pltpu.VMEM((seq, pair_w), bf16),
            state, state, state,
        ],
        compiler_params=pltpu.CompilerParams(
            dimension_semantics=("arbitrary", "arbitrary"), vmem_limit_bytes=VMEM_LIMIT_LARGE),
        name="dil_attn",
    )(qkv, qkv, qkv, qkv, qkv, qkv, qkv, biasmask)


MERGE_TM = 1024


def _merge_kernel(x_ref, oa_ref, ob_ref, oc_ref, gz0_ref, gz1_ref, gz2_ref, wb_ref, wo_ref, g_ref, out_ref):
    branches = (oa_ref[...], ob_ref[...], oc_ref[...])
    gate_refs = (gz0_ref, gz1_ref, gz2_ref)
    merged = None
    for n in range(N_BRANCH):
        y = jnp.dot(branches[n], wb_ref[n], preferred_element_type=f32)
        y = jax.nn.sigmoid(gate_refs[n][...].astype(f32)) * y
        merged = y if merged is None else merged + y
    y = jnp.dot(merged.astype(bf16), wo_ref[...], preferred_element_type=f32)
    out_ref[...] = x_ref[...] + _rms(y, g_ref[...])


def _merge(x, oa, ob, oc, z, wb, wo, g):
    t = x.shape[0]
    row = lambda i: (i, 0)
    bw = pl.BlockSpec((MERGE_TM, BRANCH_W), row)
    gate_specs = [pl.BlockSpec((MERGE_TM, D_MODEL), lambda i, n=n: (i, GATE_BLK + n)) for n in range(N_BRANCH)]
    return pl.pallas_call(
        _merge_kernel,
        grid=(t // MERGE_TM,),
        in_specs=[
            pl.BlockSpec((MERGE_TM, D_MODEL), row),
            bw, bw, bw,
            *gate_specs,
            pl.BlockSpec((N_BRANCH, BRANCH_W, D_MODEL), lambda i: (0, 0, 0)),
            pl.BlockSpec((D_MODEL, D_MODEL), lambda i: (0, 0)),
            pl.BlockSpec((1, D_MODEL), lambda i: (0, 0)),
        ],
        out_specs=pl.BlockSpec((MERGE_TM, D_MODEL), row),
        out_shape=jax.ShapeDtypeStruct((t, D_MODEL), f32),
        compiler_params=pltpu.CompilerParams(
            dimension_semantics=("arbitrary",), vmem_limit_bytes=VMEM_LIMIT),
        name="merge",
    )(x, oa, ob, oc, z, z, z, wb, wo, g)


MLP_TM = 1024
MLP_TF = 1024


def _mlp_kernel(x_ref, gpre_ref, wup_ref, wdn_ref, gpost_ref, out_ref):
    x = x_ref[...]
    h = _rms(x, gpre_ref[...]).astype(bf16)
    acc = None
    for f in range(D_FF // MLP_TF):
        fsl = slice(f * MLP_TF, (f + 1) * MLP_TF)
        u = jnp.maximum(jnp.dot(h, wup_ref[:, fsl], preferred_element_type=f32), 0.0)
        y = jnp.dot((u * u).astype(bf16), wdn_ref[fsl, :], preferred_element_type=f32)
        acc = y if acc is None else acc + y
    out_ref[...] = x + _rms(acc, gpost_ref[...])


def _mlp(x, gpre, wup, wdn, gpost):
    t = x.shape[0]
    resident = pl.Buffered(1)
    return pl.pallas_call(
        _mlp_kernel,
        grid=(t // MLP_TM,),
        in_specs=[
            pl.BlockSpec((MLP_TM, D_MODEL), lambda i: (i, 0)),
            pl.BlockSpec((1, D_MODEL), lambda i: (0, 0)),
            pl.BlockSpec((D_MODEL, D_FF), lambda i: (0, 0), pipeline_mode=resident),
            pl.BlockSpec((D_FF, D_MODEL), lambda i: (0, 0), pipeline_mode=resident),
            pl.BlockSpec((1, D_MODEL), lambda i: (0, 0)),
        ],
        out_specs=pl.BlockSpec((MLP_TM, D_MODEL), lambda i: (i, 0)),
        out_shape=jax.ShapeDtypeStruct((t, D_MODEL), f32),
        compiler_params=pltpu.CompilerParams(
            dimension_semantics=("arbitrary",), vmem_limit_bytes=VMEM_LIMIT_LARGE),
        name="mlp",
    )(x, gpre, wup, wdn, gpost)


def _rel_bucket(rel):
    nb = REL_BUCKETS // 2
    max_exact = nb // 2
    ret = jnp.where(rel > 0, nb, 0)
    n = jnp.abs(rel)
    large = max_exact + (jnp.log(jnp.maximum(n, 1).astype(f32) / max_exact)
                         / math.log(REL_MAX_DIST / max_exact) * (nb - max_exact)).astype(jnp.int32)
    large = jnp.minimum(large, nb - 1)
    return ret + jnp.where(n < max_exact, n, large)


def _bias_lookup(tab, bucket):
    one_hot = jax.nn.one_hot(bucket, REL_BUCKETS, dtype=f32)
    return jnp.einsum('...b,bc->...c', one_hot, tab.astype(f32), precision=lax.Precision.HIGHEST)


def _pad_in_weights(w_in):
    sizes = (MLA_Q_RANK, MLA_KV_RANK, MLA_ROPE)
    c_q = w_in[..., :sizes[0]]
    c_kv = w_in[..., sizes[0]:sizes[0] + sizes[1]]
    k_r = w_in[..., sizes[0] + sizes[1]:sum(sizes)]
    rest = w_in[..., sum(sizes):]
    fq0 = (FQ_BLK - DQ_BLK) * LANES
    fq1 = (FK_BLK - DQ_BLK) * LANES
    z = lambda n: jnp.zeros(w_in.shape[:-1] + (n,), bf16)
    pieces = [c_q, c_kv, z(MLA_NOPE), k_r, z(LANES - MLA_NOPE - MLA_ROPE),
              rest[..., :fq0], rest[..., fq0:fq1] * (DIFF_SCALE * LOG2E), rest[..., fq1:]]
    return jnp.concatenate([p.astype(bf16) for p in pieces], axis=-1)


def _pad_mla_weights(w_uq, w_ukv):
    depth = w_uq.shape[0]
    wq = jnp.pad(w_uq, ((0, 0), (0, 0), (0, 0), (0, LANES - MLA_NOPE - MLA_ROPE)))
    wk = jnp.pad(w_ukv[..., :MLA_NOPE], ((0, 0), (0, 0), (0, 0), (0, LANES - MLA_NOPE)))
    wv = w_ukv[..., MLA_NOPE:]
    return (wq.reshape(depth, MLA_Q_RANK, MLA_HEADS * LANES).astype(bf16),
            wk.reshape(depth, MLA_KV_RANK, MLA_HEADS * LANES).astype(bf16),
            wv.reshape(depth, MLA_KV_RANK, MLA_HEADS * MLA_V).astype(bf16))


def _rope_tables(positions):
    inv = ROPE_BASE ** (-jnp.arange(ROPE_HALF, dtype=f32) / ROPE_HALF)
    ang = positions.reshape(-1).astype(f32)[:, None] * inv
    cos_sin = jnp.concatenate([jnp.cos(ang), jnp.sin(ang)], axis=-1)
    place = np.zeros((2 * ROPE_HALF, 3 * LANES), np.float32)
    base = np.zeros((1, 3 * LANES), np.float32)
    base[0, :MLA_NOPE] = 1.0
    for i in range(ROPE_HALF):
        place[i, MLA_NOPE + i] = 1.0
        place[i, MLA_NOPE + ROPE_HALF + i] = 1.0
        place[ROPE_HALF + i, LANES + MLA_NOPE + ROPE_HALF + i] = 1.0
        place[ROPE_HALF + i, 2 * LANES + MLA_NOPE + i] = -1.0
    return jnp.dot(cos_sin, jnp.asarray(place), precision=lax.Precision.HIGHEST) + jnp.asarray(base)


def _diff_bias_tiles(rel_bias):
    tab = rel_bias[:, DIL_BIAS_COLS:]
    i = jnp.arange(ATT_TK)[:, None]
    j = jnp.arange(ATT_TK)[None, :]
    offs = jnp.array([-2, -1, 0, 1, 2])[:, None, None] * ATT_TK
    bucket = _rel_bucket(offs + j - i)
    tiles = jnp.transpose(_bias_lookup(tab, bucket), (3, 0, 1, 2)) * LOG2E
    return tiles.reshape(2, DIFF_HEADS, N_BIAS_TILES, ATT_TK, ATT_TK)


def _dil_bias_masks(rel_bias):
    i = jnp.arange(DIL_TQ)[:, None]
    c = jnp.arange(DIL_TW)[None, :]
    rel = c - DIL_HALF - i
    out = []
    for g, (_, dil) in enumerate(DIL_PATTERNS):
        tab = rel_bias[:, g * DIL_HEADS:(g + 1) * DIL_HEADS]
        b = jnp.transpose(_bias_lookup(tab, _rel_bucket(rel * dil)), (2, 0, 1))
        out.append(jnp.where((jnp.abs(rel) <= DIL_HALF)[None], b, NEG))
    return jnp.stack(out, axis=0)


def kernel(x, positions, rel_bias, g_mix_pre, w_in, g_q, w_uq, g_kv, w_ukv, lam_q1, lam_k1, lam_q2, lam_k2,
           g_diff_sub, w_branch, w_out, g_mix_post, g_mlp_pre, w_up, w_down, g_mlp_post):
    batch, seq, d = x.shape
    depth = w_in.shape[0]
    assert d == D_MODEL and seq % 1024 == 0 and seq // DIL_PATTERNS[-1][1] >= 2 * DIL_TQ

    w_in_p = _pad_in_weights(w_in)
    wq_p, wk_p, wv_p = _pad_mla_weights(w_uq, w_ukv)
    wb = w_branch.astype(bf16)
    wo = w_out.astype(bf16)
    wup = w_up.astype(bf16)
    wdn = w_down.astype(bf16)
    rope = _rope_tables(positions)
    diff_bias = _diff_bias_tiles(rel_bias)
    dil_bias = _dil_bias_masks(rel_bias)

    lam_init = jnp.array([0.8 - 0.6 * math.exp(-0.3 * l) for l in range(depth)], f32)
    lam_row = jnp.zeros((depth, 1, DIFF_QK), f32).at[:, 0, 0].set(lam_init)
    lamvec = jnp.concatenate([lam_q1[:, None], lam_k1[:, None], lam_q2[:, None], lam_k2[:, None],
                              lam_row, jnp.zeros((depth, 3, DIFF_QK), f32)], axis=1).astype(f32)

    row = lambda v: v.reshape(1, -1).astype(f32)
    xt = x.reshape(batch * seq, d)
    for l in range(depth):
        z = _in_proj(xt, row(g_mix_pre[l]), w_in_p[l])
        q, k, v = _mla_prep(z, row(g_q[l]), row(g_kv[l]), wq_p[l], wk_p[l], wv_p[l], rope)
        o_a = _mla_attn(q, k, v, batch, seq)
        o_b = _dil_attn(z, dil_bias, batch, seq)
        o_c = _diff_attn(lamvec[l], z, diff_bias, row(g_diff_sub[l]), batch, seq)
        xt = _merge(xt, o_a, o_b, o_c, z, wb[l], wo[l], row(g_mix_post[l]))
        xt = _mlp(xt, row(g_mlp_pre[l]), wup[l], wdn[l], row(g_mlp_post[l]))
    return xt.reshape(batch, seq, d)
```

```python
import math

import jax
import jax.numpy as jnp
import numpy as np
from jax import lax
from jax.experimental import pallas as pl
from jax.experimental.pallas import tpu as pltpu

f32 = jnp.float32
bf16 = jnp.bfloat16

D_MODEL = 1024
MLA_HEADS = 8
MLA_Q_RANK = 256
MLA_KV_RANK = 128
MLA_NOPE = 64
MLA_ROPE = 32
MLA_V = 64
ROPE_BASE = 10000.0
DIL_PATTERNS = ((128, 1), (512, 4), (2048, 16))
DIL_GROUPS = 3
DIL_HEADS = 4
DIL_QK = 64
DIL_V = 128
DIL_HALF = 64
DIFF_HEADS = 4
DIFF_QK = 64
DIFF_V = 128
REL_BUCKETS = 32
REL_MAX_DIST = 128
DIL_BIAS_COLS = DIL_GROUPS * DIL_HEADS
N_BRANCH = 3
BRANCH_W = 512
D_FF = 4 * D_MODEL
EPS = 1e-6
NEG = -1e30

LANES = 128
A_COLS = 512
QKV_COLS = 3584
GATE_COLS = N_BRANCH * D_MODEL
IN_PAD_COLS = A_COLS + QKV_COLS + GATE_COLS
DQ_BLK, DK_BLK, DV_BLK, FQ_BLK, FK_BLK, FV_BLK = (A_COLS // LANES + o for o in (0, 6, 12, 16, 20, 24))
GATE_BLK = (A_COLS + QKV_COLS) // D_MODEL

VMEM_LIMIT = 48 * 1024 * 1024
VMEM_LIMIT_LARGE = 56 * 1024 * 1024


def _rms(x, g):
    return x * lax.rsqrt(jnp.mean(x * x, axis=-1, keepdims=True) + EPS) * g


def _nt_dot(a, b):
    return lax.dot_general(a, b, (((1,), (1,)), ((), ())), preferred_element_type=f32)


IN_TM = 512
IN_TN = 1792


def _in_proj_kernel(x_ref, g_ref, w_ref, z_ref):
    h = _rms(x_ref[...], g_ref[...]).astype(bf16)
    for j in range(IN_PAD_COLS // IN_TN):
        cols = slice(j * IN_TN, (j + 1) * IN_TN)
        z_ref[:, cols] = jnp.dot(h, w_ref[:, cols], preferred_element_type=f32).astype(z_ref.dtype)


def _in_proj(x, g, w):
    t = x.shape[0]
    return pl.pallas_call(
        _in_proj_kernel,
        grid=(t // IN_TM,),
        in_specs=[
            pl.BlockSpec((IN_TM, D_MODEL), lambda i: (i, 0)),
            pl.BlockSpec((1, D_MODEL), lambda i: (0, 0)),
            pl.BlockSpec((D_MODEL, IN_PAD_COLS), lambda i: (0, 0), pipeline_mode=pl.Buffered(1)),
        ],
        out_specs=pl.BlockSpec((IN_TM, IN_PAD_COLS), lambda i: (i, 0)),
        out_shape=jax.ShapeDtypeStruct((t, IN_PAD_COLS), bf16),
        compiler_params=pltpu.CompilerParams(
            dimension_semantics=("arbitrary",), vmem_limit_bytes=VMEM_LIMIT),
        name="in_proj",
    )(x, g, w)


PREP_TM = 1024
MLA_SCALE = (MLA_NOPE + MLA_ROPE) ** -0.5
LOG2E = math.log2(math.e)
ROPE_HALF = MLA_ROPE // 2


def _rope_lanes(x, rope):
    c = rope[:, 0:LANES]
    sa = rope[:, LANES:2 * LANES]
    sb = rope[:, 2 * LANES:3 * LANES]
    return (x * c + pltpu.roll(x, ROPE_HALF, 1) * sa + pltpu.roll(x, LANES - ROPE_HALF, 1) * sb)


def _mla_prep_kernel(a_ref, gq_ref, gkv_ref, wq_ref, wk_ref, wv_ref, rope_ref, q_ref, k_ref, v_ref):
    a = a_ref[...].astype(f32)
    rope = rope_ref[...]
    cq = _rms(a[:, :MLA_Q_RANK], gq_ref[...]).astype(bf16)
    ckv = _rms(a[:, MLA_Q_RANK:MLA_Q_RANK + MLA_KV_RANK], gkv_ref[...]).astype(bf16)
    k_rope = _rope_lanes(a[:, A_COLS - LANES:], rope)
    qf = jnp.dot(cq, wq_ref[...], preferred_element_type=f32)
    kf = jnp.dot(ckv, wk_ref[...], preferred_element_type=f32)
    vf = jnp.dot(ckv, wv_ref[...], preferred_element_type=f32).astype(bf16)
    ones = jnp.ones((vf.shape[0], LANES), bf16)
    lane = lax.broadcasted_iota(jnp.int32, ones.shape, 1)
    for j in range(MLA_HEADS // 2):
        pair = vf[:, j * LANES:(j + 1) * LANES]
        v_ref[:, 2 * j * LANES:(2 * j + 1) * LANES] = jnp.where(lane < MLA_V, pair, ones)
        v_ref[:, (2 * j + 1) * LANES:(2 * j + 2) * LANES] = jnp.where(lane >= MLA_V, pair, ones)
    for h in range(MLA_HEADS):
        sl = slice(h * LANES, (h + 1) * LANES)
        q_ref[:, sl] = (_rope_lanes(qf[:, sl], rope) * (MLA_SCALE * LOG2E)).astype(bf16)
        k_ref[:, sl] = (kf[:, sl] + k_rope).astype(bf16)


def _mla_prep(a, gq, gkv, wq, wk, wv, rope):
    t = a.shape[0]
    hq = MLA_HEADS * LANES
    hv = MLA_HEADS * MLA_V
    const = lambda i: (0, 0)
    return pl.pallas_call(
        _mla_prep_kernel,
        grid=(t // PREP_TM,),
        in_specs=[
            pl.BlockSpec((PREP_TM, A_COLS), lambda i: (i, 0)),
            pl.BlockSpec((1, MLA_Q_RANK), const),
            pl.BlockSpec((1, MLA_KV_RANK), const),
            pl.BlockSpec((MLA_Q_RANK, hq), const),
            pl.BlockSpec((MLA_KV_RANK, hq), const),
            pl.BlockSpec((MLA_KV_RANK, hv), const),
            pl.BlockSpec((PREP_TM, 3 * LANES), lambda i: (i, 0)),
        ],
        out_specs=[
            pl.BlockSpec((PREP_TM, hq), lambda i: (i, 0)),
            pl.BlockSpec((PREP_TM, hq), lambda i: (i, 0)),
            pl.BlockSpec((PREP_TM, hq), lambda i: (i, 0)),
        ],
        out_shape=[
            jax.ShapeDtypeStruct((t, hq), bf16),
            jax.ShapeDtypeStruct((t, hq), bf16),
            jax.ShapeDtypeStruct((t, hq), bf16),
        ],
        compiler_params=pltpu.CompilerParams(
            dimension_semantics=("arbitrary",), vmem_limit_bytes=VMEM_LIMIT),
        name="mla_prep",
    )(a, gq, gkv, wq, wk, wv, rope)


ATT_TQ = 512
ATT_TK = 256


MLA_TK = 256
DIFF_TK = 512


def _lane_fold(x, op):
    out = x[:, :LANES]
    for u in range(1, x.shape[1] // LANES):
        out = op(out, x[:, u * LANES:(u + 1) * LANES])
    return out


def _flash_streams(streams, seq, tk):
    n_chunks = seq // tk
    state = [None] * len(streams)
    for c in range(n_chunks):
        rows = slice(c * tk, (c + 1) * tk)
        for i, (q, k_ref, k_cols, v_ref, v_cols, bias_fn) in enumerate(streams):
            s = _nt_dot(q, k_ref[rows, k_cols])
            if bias_fn is not None:
                s = s + bias_fn(c)
            m_c = jnp.max(_lane_fold(s, jnp.maximum), axis=-1, keepdims=True)
            if c == 0:
                m_new = m_c
            else:
                m_old, acc_old = state[i]
                m_new = jnp.maximum(m_old, m_c)
            p = jnp.exp2(s - m_new).astype(bf16)
            pv = jnp.dot(p, v_ref[rows, v_cols], preferred_element_type=f32)
            state[i] = (m_new, pv if c == 0 else jnp.exp2(m_old - m_new) * acc_old + pv)
    return [acc for _, acc in state]


MLA_STEP_HEADS = 2


def _mla_attn_kernel(q_ref, k_ref, v_ref, o_ref):
    tq = q_ref.shape[0]
    streams = [(q_ref[:, h * LANES:(h + 1) * LANES], k_ref, slice(h * LANES, (h + 1) * LANES),
                v_ref, slice(h * LANES, (h + 1) * LANES), None) for h in range(MLA_STEP_HEADS)]
    accs = _flash_streams(streams, k_ref.shape[0], MLA_TK)
    lane = lax.broadcasted_iota(jnp.int32, (tq, LANES), 1)
    for j in range(MLA_STEP_HEADS // 2):
        acc_a, acc_b = accs[2 * j], accs[2 * j + 1]
        o_ref[:, j * LANES:(j + 1) * LANES] = jnp.where(
            lane < MLA_V, acc_a / acc_a[:, MLA_V:MLA_V + 1], acc_b / acc_b[:, 0:1]).astype(o_ref.dtype)


def _mla_attn(q, k, v, batch, seq):
    t = q.shape[0]
    nq = seq // ATT_TQ
    n_groups = MLA_HEADS // MLA_STEP_HEADS
    w = MLA_STEP_HEADS * LANES
    return pl.pallas_call(
        _mla_attn_kernel,
        grid=(batch, n_groups, nq),
        in_specs=[
            pl.BlockSpec((ATT_TQ, w), lambda b, j, i: (b * nq + i, j)),
            pl.BlockSpec((seq, w), lambda b, j, i: (b, j)),
            pl.BlockSpec((seq, w), lambda b, j, i: (b, j)),
        ],
        out_specs=pl.BlockSpec((ATT_TQ, w // 2), lambda b, j, i: (b * nq + i, j)),
        out_shape=jax.ShapeDtypeStruct((t, MLA_HEADS * MLA_V), bf16),
        compiler_params=pltpu.CompilerParams(
            dimension_semantics=("arbitrary", "arbitrary", "arbitrary"), vmem_limit_bytes=VMEM_LIMIT),
        name="mla_attn",
    )(q, k, v)


DIFF_TQ = 512
DIFF_SCALE = DIFF_QK ** -0.5
BIAS_FAR = 2
N_BIAS_TILES = 2 * BIAS_FAR + 1


def _diff_attn_kernel(lam_ref, q0_ref, q1_ref, k0_ref, k1_ref, v_ref, bias_ref, gsub_ref, o_ref, vaug_ref):
    qi = pl.program_id(2)
    tq = q0_ref.shape[0]
    seq = v_ref.shape[0]

    @pl.when(qi == 0)
    def _():
        for hh in range(2):
            vaug_ref[hh, :, :DIFF_V] = v_ref[:, hh * DIFF_V:(hh + 1) * DIFF_V]
            vaug_ref[hh, :, DIFF_V:] = jnp.ones((seq, DIFF_V), bf16)

    lv = lam_ref[...]
    lam_init = lv[4:5, 0:1]
    lam = (jnp.exp(jnp.sum(lv[0:1] * lv[1:2], axis=-1, keepdims=True))
           - jnp.exp(jnp.sum(lv[2:3] * lv[3:4], axis=-1, keepdims=True)) + lam_init)
    lane = lax.broadcasted_iota(jnp.int32, (tq, LANES), 1)
    q_refs = (q0_ref, q1_ref)
    k_refs = (k0_ref, k1_ref)
    all_cols = slice(0, LANES)
    tiles_per_chunk = DIFF_TK // ATT_TK
    streams = []
    for hh in range(2):
        head_lanes = (lane >= DIFF_QK) == (hh == 1)
        for m in range(2):
            q = q_refs[m][...]
            q = jnp.where(head_lanes, q, jnp.zeros_like(q))

            def bias_fn(c, m=m, hh=hh):
                def tile(r, cc):
                    offset = (c * tiles_per_chunk + cc) - (qi * (tq // ATT_TK) + r)
                    return bias_ref[m, hh, jnp.clip(offset, -BIAS_FAR, BIAS_FAR) + BIAS_FAR]
                return jnp.concatenate(
                    [jnp.concatenate([tile(r, cc) for cc in range(tiles_per_chunk)], axis=1)
                     for r in range(tq // ATT_TK)], axis=0)

            streams.append((q, k_refs[m], all_cols, vaug_ref.at[hh], slice(0, 2 * DIFF_V), bias_fn))
    outs = [acc[:, :DIFF_V] / acc[:, DIFF_V:DIFF_V + 1] for acc in _flash_streams(streams, seq, DIFF_TK)]
    for hh in range(2):
        o = outs[2 * hh] - lam * outs[2 * hh + 1]
        o = _rms(o, gsub_ref[...]) * (1.0 - lam_init)
        o_ref[:, hh * DIFF_V:(hh + 1) * DIFF_V] = o.astype(o_ref.dtype)


def _diff_attn(lamvec, qkv, bias_tiles, gsub, batch, seq):
    t = qkv.shape[0]
    nq = seq // DIFF_TQ
    n_pairs = DIFF_HEADS // 2
    return pl.pallas_call(
        _diff_attn_kernel,
        grid=(batch, n_pairs, nq),
        in_specs=[
            pl.BlockSpec((8, DIFF_QK), lambda b, j, i: (0, 0)),
            pl.BlockSpec((DIFF_TQ, LANES), lambda b, j, i: (b * nq + i, FQ_BLK + j)),
            pl.BlockSpec((DIFF_TQ, LANES), lambda b, j, i: (b * nq + i, FQ_BLK + n_pairs + j)),
            pl.BlockSpec((seq, LANES), lambda b, j, i: (b, FK_BLK + j)),
            pl.BlockSpec((seq, LANES), lambda b, j, i: (b, FK_BLK + n_pairs + j)),
            pl.BlockSpec((seq, 2 * DIFF_V), lambda b, j, i: (b, FV_BLK // 2 + j)),
            pl.BlockSpec((2, 2, N_BIAS_TILES, ATT_TK, ATT_TK), lambda b, j, i: (0, j, 0, 0, 0)),
            pl.BlockSpec((1, DIFF_V), lambda b, j, i: (0, 0)),
        ],
        out_specs=pl.BlockSpec((DIFF_TQ, 2 * DIFF_V), lambda b, j, i: (b * nq + i, j)),
        out_shape=jax.ShapeDtypeStruct((t, DIFF_HEADS * DIFF_V), bf16),
        scratch_shapes=[pltpu.VMEM((2, seq, 2 * DIFF_V), bf16)],
        compiler_params=pltpu.CompilerParams(
            dimension_semantics=("arbitrary", "arbitrary", "arbitrary"), vmem_limit_bytes=VMEM_LIMIT),
        name="diff_attn",
    )(lamvec, qkv, qkv, qkv, qkv, qkv, bias_tiles, gsub)


DIL_TQ = 128
DIL_TW = DIL_TQ + 2 * DIL_HALF
DIL_SCALE = DIL_QK ** -0.5
DIL_UNROLL = 16


def _regroup_start_stride(r, seq, dil_from, dil_to):
    return (r % dil_from) * (seq // dil_from) + r // dil_from, dil_to // dil_from


def _residue_major(src_ref, dst_ref, stage_ref, dil_from, dil_to):
    seq, width = src_ref.shape
    sub_len = seq // dil_to
    for c in range(width // LANES):
        csl = slice(c * LANES, (c + 1) * LANES)
        stage_ref[c] = src_ref[:, csl].astype(f32)
        for r in range(dil_to):
            start, stride = _regroup_start_stride(r, seq, dil_from, dil_to)
            dst_ref[r * sub_len:(r + 1) * sub_len, csl] = (
                stage_ref[c, pl.ds(start, sub_len, stride=stride), :].astype(dst_ref.dtype))


def _regroup_state(state_ref, stage_ref, dil_from, dil_to):
    _, seq, _ = state_ref.shape
    sub_len = seq // dil_to
    for hh in range(2):
        stage_ref[0] = state_ref[hh]
        for r in range(dil_to):
            start, stride = _regroup_start_stride(r, seq, dil_from, dil_to)
            state_ref[hh, r * sub_len:(r + 1) * sub_len, :] = stage_ref[0, pl.ds(start, sub_len, stride=stride), :]


def _ungroup_state(state_ref, stage_ref, dil_from, dil_to):
    _, seq, _ = state_ref.shape
    sub_len = seq // dil_to
    for hh in range(2):
        for r in range(dil_to):
            start, stride = _regroup_start_stride(r, seq, dil_from, dil_to)
            stage_ref[0, pl.ds(start, sub_len, stride=stride), :] = state_ref[hh, r * sub_len:(r + 1) * sub_len, :]
        state_ref[hh] = stage_ref[0]


def _dil_attn_kernel(q0_ref, q1_ref, q2_ref, k0_ref, k1_ref, k2_ref, v_ref, bm_ref, o_ref,
                     stage_ref, qp_ref, kp_ref, vp_ref, m_ref, l_ref, acc_ref):
    seq = v_ref.shape[0]
    n_tiles = seq // DIL_TQ
    q_refs = (q0_ref, q1_ref, q2_ref)
    k_refs = (k0_ref, k1_ref, k2_ref)
    lane = lax.broadcasted_iota(jnp.int32, (DIL_TQ, LANES), 1)
    col = lax.broadcasted_iota(jnp.int32, (DIL_TQ, DIL_TW), 1)
    dils = [dil for _, dil in DIL_PATTERNS]
    assert dils[0] == 1 and all(b % a == 0 for a, b in zip(dils, dils[1:]))
    for g, dil in enumerate(dils):
        if g == 0:
            q_src, k_src, v_src = q_refs[g], k_refs[g], v_ref
        else:
            _residue_major(q_refs[g], qp_ref, stage_ref, 1, dil)
            _residue_major(k_refs[g], kp_ref, stage_ref, 1, dil)
            _residue_major(v_src, vp_ref, stage_ref, dils[g - 1], dil)
            for state_ref in (m_ref, l_ref, acc_ref):
                _regroup_state(state_ref, stage_ref, dils[g - 1], dil)
            q_src, k_src, v_src = qp_ref, kp_ref, vp_ref
        tiles_per_residue = n_tiles // dil

        def tile(n, carry, g=g, dil=dil, q_src=q_src, k_src=k_src, v_src=v_src,
                 tiles_per_residue=tiles_per_residue):
            base = pl.multiple_of(n * DIL_TQ, DIL_TQ)
            t_in = n % tiles_per_residue
            lo = pl.multiple_of(jnp.maximum(base - DIL_HALF, 0), DIL_HALF)
            hi = pl.multiple_of(jnp.minimum(base + DIL_TQ, seq - DIL_HALF), DIL_HALF)
            c_lo = jnp.where(t_in == 0, DIL_HALF, 0)
            c_hi = jnp.where(t_in == tiles_per_residue - 1, DIL_HALF + DIL_TQ, DIL_TW)
            valid = (col >= c_lo) & (col < c_hi)
            rows = pl.ds(base, DIL_TQ)
            qt = q_src[rows, :]
            kw = jnp.concatenate([k_src[pl.ds(lo, DIL_HALF), :], k_src[pl.ds(base, DIL_TQ), :],
                                  k_src[pl.ds(hi, DIL_HALF), :]], axis=0)
            vw = jnp.concatenate([v_src[pl.ds(lo, DIL_HALF), :], v_src[pl.ds(base, DIL_TQ), :],
                                  v_src[pl.ds(hi, DIL_HALF), :]], axis=0)
            for hh in range(2):
                qh = jnp.where((lane >= DIL_QK) == (hh == 1), qt * DIL_SCALE, jnp.zeros_like(qt))
                hsl = slice(hh * DIL_V, (hh + 1) * DIL_V)
                s = _nt_dot(qh, kw) + bm_ref[g, hh]
                s = jnp.where(valid, s, NEG)
                m_t = jnp.max(s, axis=-1, keepdims=True)
                p = jnp.exp(s - m_t)
                l_t = jnp.broadcast_to(jnp.sum(p, axis=-1, keepdims=True), (DIL_TQ, DIL_V))
                u_t = jnp.dot(p.astype(bf16), vw[:, hsl], preferred_element_type=f32)
                m_t = jnp.broadcast_to(m_t, (DIL_TQ, DIL_V))
                if g == 0:
                    m_ref[hh, rows, :] = m_t
                    l_ref[hh, rows, :] = l_t
                    acc_ref[hh, rows, :] = u_t
                else:
                    m_old = m_ref[hh, rows, :]
                    m_new = jnp.maximum(m_old, m_t)
                    a_old = jnp.exp(m_old - m_new)
                    a_t = jnp.exp(m_t - m_new)
                    m_ref[hh, rows, :] = m_new
                    l_ref[hh, rows, :] = a_old * l_ref[hh, rows, :] + a_t * l_t
                    acc_ref[hh, rows, :] = a_old * acc_ref[hh, rows, :] + a_t * u_t
            return carry

        def tile_group(i, carry, tile=tile):
            for u in range(DIL_UNROLL):
                tile(i * DIL_UNROLL + u, carry)
            return carry

        lax.fori_loop(0, n_tiles // DIL_UNROLL, tile_group, 0)
    for hh in range(2):
        acc_ref[hh] = acc_ref[hh] / l_ref[hh]
    for g in range(len(dils) - 1, 0, -1):
        _ungroup_state(acc_ref, stage_ref, dils[g - 1], dils[g])
    for hh in range(2):
        o_ref[:, hh * DIL_V:(hh + 1) * DIL_V] = acc_ref[hh].astype(o_ref.dtype)


def _dil_attn(qkv, biasmask, batch, seq):
    t = qkv.shape[0]
    n_pairs = DIL_HEADS // 2
    pair_w = 2 * DIL_V

    def qk_spec(blk0, g):
        return pl.BlockSpec((seq, LANES), lambda b, j: (b, blk0 + g * n_pairs + j))

    state = pltpu.VMEM((2, seq, DIL_V), f32)
    return pl.pallas_call(
        _dil_attn_kernel,
        grid=(batch, n_pairs),
        in_specs=[qk_spec(DQ_BLK, g) for g in range(DIL_GROUPS)]
        + [qk_spec(DK_BLK, g) for g in range(DIL_GROUPS)]
        + [pl.BlockSpec((seq, pair_w), lambda b, j: (b, DV_BLK // 2 + j)),
           pl.BlockSpec((DIL_GROUPS, 2, DIL_TQ, DIL_TW), lambda b, j: (0, j, 0, 0))],
        out_specs=pl.BlockSpec((seq, pair_w), lambda b, j: (b, j)),
        out_shape=jax.ShapeDtypeStruct((t, DIL_HEADS * DIL_V), bf16),
        scratch_shapes=[
            pltpu.VMEM((2, seq, LANES), f32),
            pltpu.VMEM((seq, LANES), bf16),
            pltpu.VMEM((seq, LANES), bf16),
            pltpu.VMEM((seq, pair_w), bf16),
            state, state, state,
        ],
        compiler_params=pltpu.CompilerParams(
            dimension_semantics=("arbitrary", "arbitrary"), vmem_limit_bytes=VMEM_LIMIT_LARGE),
        name="dil_attn",
    )(qkv, qkv, qkv, qkv, qkv, qkv, qkv, biasmask)


MERGE_TM = 1024


def _merge_kernel(x_ref, oa_ref, ob_ref, oc_ref, gz0_ref, gz1_ref, gz2_ref, wb_ref, wo_ref, g_ref, out_ref):
    branches = (oa_ref[...], ob_ref[...], oc_ref[...])
    gate_refs = (gz0_ref, gz1_ref, gz2_ref)
    merged = None
    for n in range(N_BRANCH):
        y = jnp.dot(branches[n], wb_ref[n], preferred_element_type=f32)
        y = jax.nn.sigmoid(gate_refs[n][...].astype(f32)) * y
        merged = y if merged is None else merged + y
    y = jnp.dot(merged.astype(bf16), wo_ref[...], preferred_element_type=f32)
    out_ref[...] = x_ref[...] + _rms(y, g_ref[...])


def _merge(x, oa, ob, oc, z, wb, wo, g):
    t = x.shape[0]
    row = lambda i: (i, 0)
    bw = pl.BlockSpec((MERGE_TM, BRANCH_W), row)
    gate_specs = [pl.BlockSpec((MERGE_TM, D_MODEL), lambda i, n=n: (i, GATE_BLK + n)) for n in range(N_BRANCH)]
    return pl.pallas_call(
        _merge_kernel,
        grid=(t // MERGE_TM,),
        in_specs=[
            pl.BlockSpec((MERGE_TM, D_MODEL), row),
            bw, bw, bw,
            *gate_specs,
            pl.BlockSpec((N_BRANCH, BRANCH_W, D_MODEL), lambda i: (0, 0, 0)),
            pl.BlockSpec((D_MODEL, D_MODEL), lambda i: (0, 0)),
            pl.BlockSpec((1, D_MODEL), lambda i: (0, 0)),
        ],
        out_specs=pl.BlockSpec((MERGE_TM, D_MODEL), row),
        out_shape=jax.ShapeDtypeStruct((t, D_MODEL), f32),
        compiler_params=pltpu.CompilerParams(
            dimension_semantics=("arbitrary",), vmem_limit_bytes=VMEM_LIMIT),
        name="merge",
    )(x, oa, ob, oc, z, z, z, wb, wo, g)


MLP_TM = 1024
MLP_TF = 1024


def _mlp_kernel(x_ref, gpre_ref, wup_ref, wdn_ref, gpost_ref, out_ref):
    x = x_ref[...]
    h = _rms(x, gpre_ref[...]).astype(bf16)
    acc = None
    for f in range(D_FF // MLP_TF):
        fsl = slice(f * MLP_TF, (f + 1) * MLP_TF)
        u = jnp.maximum(jnp.dot(h, wup_ref[:, fsl], preferred_element_type=f32), 0.0)
        y = jnp.dot((u * u).astype(bf16), wdn_ref[fsl, :], preferred_element_type=f32)
        acc = y if acc is None else acc + y
    out_ref[...] = x + _rms(acc, gpost_ref[...])


def _mlp(x, gpre, wup, wdn, gpost):
    t = x.shape[0]
    resident = pl.Buffered(1)
    return pl.pallas_call(
        _mlp_kernel,
        grid=(t // MLP_TM,),
        in_specs=[
            pl.BlockSpec((MLP_TM, D_MODEL), lambda i: (i, 0)),
            pl.BlockSpec((1, D_MODEL), lambda i: (0, 0)),
            pl.BlockSpec((D_MODEL, D_FF), lambda i: (0, 0), pipeline_mode=resident),
            pl.BlockSpec((D_FF, D_MODEL), lambda i: (0, 0), pipeline_mode=resident),
            pl.BlockSpec((1, D_MODEL), lambda i: (0, 0)),
        ],
        out_specs=pl.BlockSpec((MLP_TM, D_MODEL), lambda i: (i, 0)),
        out_shape=jax.ShapeDtypeStruct((t, D_MODEL), f32),
        compiler_params=pltpu.CompilerParams(
            dimension_semantics=("arbitrary",), vmem_limit_bytes=VMEM_LIMIT_LARGE),
        name="mlp",
    )(x, gpre, wup, wdn, gpost)


def _rel_bucket(rel):
    nb = REL_BUCKETS // 2
    max_exact = nb // 2
    ret = jnp.where(rel > 0, nb, 0)
    n = jnp.abs(rel)
    large = max_exact + (jnp.log(jnp.maximum(n, 1).astype(f32) / max_exact)
                         / math.log(REL_MAX_DIST / max_exact) * (nb - max_exact)).astype(jnp.int32)
    large = jnp.minimum(large, nb - 1)
    return ret + jnp.where(n < max_exact, n, large)


def _bias_lookup(tab, bucket):
    one_hot = jax.nn.one_hot(bucket, REL_BUCKETS, dtype=f32)
    return jnp.einsum('...b,bc->...c', one_hot, tab.astype(f32), precision=lax.Precision.HIGHEST)


def _pad_in_weights(w_in):
    sizes = (MLA_Q_RANK, MLA_KV_RANK, MLA_ROPE)
    c_q = w_in[..., :sizes[0]]
    c_kv = w_in[..., sizes[0]:sizes[0] + sizes[1]]
    k_r = w_in[..., sizes[0] + sizes[1]:sum(sizes)]
    rest = w_in[..., sum(sizes):]
    fq0 = (FQ_BLK - DQ_BLK) * LANES
    fq1 = (FK_BLK - DQ_BLK) * LANES
    z = lambda n: jnp.zeros(w_in.shape[:-1] + (n,), bf16)
    pieces = [c_q, c_kv, z(MLA_NOPE), k_r, z(LANES - MLA_NOPE - MLA_ROPE),
              rest[..., :fq0], rest[..., fq0:fq1] * (DIFF_SCALE * LOG2E), rest[..., fq1:]]
    return jnp.concatenate([p.astype(bf16) for p in pieces], axis=-1)


def _pad_mla_weights(w_uq, w_ukv):
    depth = w_uq.shape[0]
    wq = jnp.pad(w_uq, ((0, 0), (0, 0), (0, 0), (0, LANES - MLA_NOPE - MLA_ROPE)))
    wk = jnp.pad(w_ukv[..., :MLA_NOPE], ((0, 0), (0, 0), (0, 0), (0, LANES - MLA_NOPE)))
    wv = w_ukv[..., MLA_NOPE:]
    return (wq.reshape(depth, MLA_Q_RANK, MLA_HEADS * LANES).astype(bf16),
            wk.reshape(depth, MLA_KV_RANK, MLA_HEADS * LANES).astype(bf16),
            wv.reshape(depth, MLA_KV_RANK, MLA_HEADS * MLA_V).astype(bf16))


def _rope_tables(positions):
    inv = ROPE_BASE ** (-jnp.arange(ROPE_HALF, dtype=f32) / ROPE_HALF)
    ang = positions.reshape(-1).astype(f32)[:, None] * inv
    cos_sin = jnp.concatenate([jnp.cos(ang), jnp.sin(ang)], axis=-1)
    place = np.zeros((2 * ROPE_HALF, 3 * LANES), np.float32)
    base = np.zeros((1, 3 * LANES), np.float32)
    base[0, :MLA_NOPE] = 1.0
    for i in range(ROPE_HALF):
        place[i, MLA_NOPE + i] = 1.0
        place[i, MLA_NOPE + ROPE_HALF + i] = 1.0
        place[ROPE_HALF + i, LANES + MLA_NOPE + ROPE_HALF + i] = 1.0
        place[ROPE_HALF + i, 2 * LANES + MLA_NOPE + i] = -1.0
    return jnp.dot(cos_sin, jnp.asarray(place), precision=lax.Precision.HIGHEST) + jnp.asarray(base)


def _diff_bias_tiles(rel_bias):
    tab = rel_bias[:, DIL_BIAS_COLS:]
    i = jnp.arange(ATT_TK)[:, None]
    j = jnp.arange(ATT_TK)[None, :]
    offs = jnp.arange(-BIAS_FAR, BIAS_FAR + 1)[:, None, None] * ATT_TK
    bucket = _rel_bucket(offs + j - i)
    tiles = jnp.transpose(_bias_lookup(tab, bucket), (3, 0, 1, 2)) * LOG2E
    return tiles.reshape(2, DIFF_HEADS, N_BIAS_TILES, ATT_TK, ATT_TK)


def _dil_bias_masks(rel_bias):
    i = jnp.arange(DIL_TQ)[:, None]
    c = jnp.arange(DIL_TW)[None, :]
    rel = c - DIL_HALF - i
    out = []
    for g, (_, dil) in enumerate(DIL_PATTERNS):
        tab = rel_bias[:, g * DIL_HEADS:(g + 1) * DIL_HEADS]
        b = jnp.transpose(_bias_lookup(tab, _rel_bucket(rel * dil)), (2, 0, 1))
        out.append(jnp.where((jnp.abs(rel) <= DIL_HALF)[None], b, NEG))
    return jnp.stack(out, axis=0)


def kernel(x, positions, rel_bias, g_mix_pre, w_in, g_q, w_uq, g_kv, w_ukv, lam_q1, lam_k1, lam_q2, lam_k2,
           g_diff_sub, w_branch, w_out, g_mix_post, g_mlp_pre, w_up, w_down, g_mlp_post):
    batch, seq, d = x.shape
    depth = w_in.shape[0]
    assert d == D_MODEL and seq % 1024 == 0 and seq // DIL_PATTERNS[-1][1] >= 2 * DIL_TQ

    w_in_p = _pad_in_weights(w_in)
    wq_p, wk_p, wv_p = _pad_mla_weights(w_uq, w_ukv)
    wb = w_branch.astype(bf16)
    wo = w_out.astype(bf16)
    wup = w_up.astype(bf16)
    wdn = w_down.astype(bf16)
    rope = _rope_tables(positions)
    diff_bias = _diff_bias_tiles(rel_bias)
    dil_bias = _dil_bias_masks(rel_bias)

    lam_init = jnp.array([0.8 - 0.6 * math.exp(-0.3 * l) for l in range(depth)], f32)
    lam_row = jnp.zeros((depth, 1, DIFF_QK), f32).at[:, 0, 0].set(lam_init)
    lamvec = jnp.concatenate([lam_q1[:, None], lam_k1[:, None], lam_q2[:, None], lam_k2[:, None],
                              lam_row, jnp.zeros((depth, 3, DIFF_QK), f32)], axis=1).astype(f32)

    row = lambda v: v.reshape(1, -1).astype(f32)
    xt = x.reshape(batch * seq, d)
    for l in range(depth):
        z = _in_proj(xt, row(g_mix_pre[l]), w_in_p[l])
        q, k, v = _mla_prep(z, row(g_q[l]), row(g_kv[l]), wq_p[l], wk_p[l], wv_p[l], rope)
        o_a = _mla_attn(q, k, v, batch, seq)
        o_b = _dil_attn(z, dil_bias, batch, seq)
        o_c = _diff_attn(lamvec[l], z, diff_bias, row(g_diff_sub[l]), batch, seq)
        xt = _merge(xt, o_a, o_b, o_c, z, wb[l], wo[l], row(g_mix_post[l]))
        xt = _mlp(xt, row(g_mlp_pre[l]), wup[l], wdn[l], row(g_mlp_post[l]))
    return xt.reshape(batch, seq, d)
```

```python
import math

import jax
import jax.numpy as jnp
import numpy as np
from jax import lax
from jax.experimental import pallas as pl
from jax.experimental.pallas import tpu as pltpu

f32 = jnp.float32
bf16 = jnp.bfloat16

D_MODEL = 1024
MLA_HEADS = 8
MLA_Q_RANK = 256
MLA_KV_RANK = 128
MLA_NOPE = 64
MLA_ROPE = 32
MLA_V = 64
ROPE_BASE = 10000.0
DIL_PATTERNS = ((128, 1), (512, 4), (2048, 16))
DIL_GROUPS = 3
DIL_HEADS = 4
DIL_QK = 64
DIL_V = 128
DIL_HALF = 64
DIFF_HEADS = 4
DIFF_QK = 64
DIFF_V = 128
REL_BUCKETS = 32
REL_MAX_DIST = 128
DIL_BIAS_COLS = DIL_GROUPS * DIL_HEADS
N_BRANCH = 3
BRANCH_W = 512
D_FF = 4 * D_MODEL
EPS = 1e-6
NEG = -1e30

LANES = 128
A_COLS = 512
QKV_COLS = 3584
GATE_COLS = N_BRANCH * D_MODEL
IN_PAD_COLS = A_COLS + QKV_COLS + GATE_COLS
DQ_BLK, DK_BLK, DV_BLK, FQ_BLK, FK_BLK, FV_BLK = (A_COLS // LANES + o for o in (0, 6, 12, 16, 20, 24))
GATE_BLK = (A_COLS + QKV_COLS) // D_MODEL

VMEM_LIMIT = 48 * 1024 * 1024
VMEM_LIMIT_LARGE = 56 * 1024 * 1024


def _rms(x, g):
    return x * lax.rsqrt(jnp.mean(x * x, axis=-1, keepdims=True) + EPS) * g


def _nt_dot(a, b):
    return lax.dot_general(a, b, (((1,), (1,)), ((), ())), preferred_element_type=f32)


IN_TM = 512
IN_TN = 1792


def _in_proj_kernel(x_ref, g_ref, w_ref, z_ref):
    h = _rms(x_ref[...], g_ref[...]).astype(bf16)
    for j in range(IN_PAD_COLS // IN_TN):
        cols = slice(j * IN_TN, (j + 1) * IN_TN)
        z_ref[:, cols] = jnp.dot(h, w_ref[:, cols], preferred_element_type=f32).astype(z_ref.dtype)


def _in_proj(x, g, w):
    t = x.shape[0]
    return pl.pallas_call(
        _in_proj_kernel,
        grid=(t // IN_TM,),
        in_specs=[
            pl.BlockSpec((IN_TM, D_MODEL), lambda i: (i, 0)),
            pl.BlockSpec((1, D_MODEL), lambda i: (0, 0)),
            pl.BlockSpec((D_MODEL, IN_PAD_COLS), lambda i: (0, 0), pipeline_mode=pl.Buffered(1)),
        ],
        out_specs=pl.BlockSpec((IN_TM, IN_PAD_COLS), lambda i: (i, 0)),
        out_shape=jax.ShapeDtypeStruct((t, IN_PAD_COLS), bf16),
        compiler_params=pltpu.CompilerParams(
            dimension_semantics=("arbitrary",), vmem_limit_bytes=VMEM_LIMIT),
        name="in_proj",
    )(x, g, w)


PREP_TM = 1024
MLA_SCALE = (MLA_NOPE + MLA_ROPE) ** -0.5
LOG2E = math.log2(math.e)
ROPE_HALF = MLA_ROPE // 2


def _rope_lanes(x, rope):
    c = rope[:, 0:LANES]
    sa = rope[:, LANES:2 * LANES]
    sb = rope[:, 2 * LANES:3 * LANES]
    return (x * c + pltpu.roll(x, ROPE_HALF, 1) * sa + pltpu.roll(x, LANES - ROPE_HALF, 1) * sb)


def _mla_prep_kernel(a_ref, gq_ref, gkv_ref, wq_ref, wk_ref, wv_ref, rope_ref, q_ref, k_ref, v_ref):
    a = a_ref[...].astype(f32)
    rope = rope_ref[...]
    cq = _rms(a[:, :MLA_Q_RANK], gq_ref[...]).astype(bf16)
    ckv = _rms(a[:, MLA_Q_RANK:MLA_Q_RANK + MLA_KV_RANK], gkv_ref[...]).astype(bf16)
    k_rope = _rope_lanes(a[:, A_COLS - LANES:], rope)
    qf2 = jnp.dot(cq, wq_ref[...], preferred_element_type=f32)
    hq = MLA_HEADS * LANES
    qf, q_partner = qf2[:, :hq], qf2[:, hq:]
    cos_tab = rope[:, 0:LANES]
    sin_tab = rope[:, LANES:2 * LANES] - rope[:, 2 * LANES:3 * LANES]
    kf = jnp.dot(ckv, wk_ref[...], preferred_element_type=f32)
    vf = jnp.dot(ckv, wv_ref[...], preferred_element_type=f32).astype(bf16)
    ones = jnp.ones((vf.shape[0], LANES), bf16)
    lane = lax.broadcasted_iota(jnp.int32, ones.shape, 1)
    for j in range(MLA_HEADS // 2):
        pair = vf[:, j * LANES:(j + 1) * LANES]
        v_ref[:, 2 * j * LANES:(2 * j + 1) * LANES] = jnp.where(lane < MLA_V, pair, ones)
        v_ref[:, (2 * j + 1) * LANES:(2 * j + 2) * LANES] = jnp.where(lane >= MLA_V, pair, ones)
    for h in range(MLA_HEADS):
        sl = slice(h * LANES, (h + 1) * LANES)
        q_rot = qf[:, sl] * cos_tab + q_partner[:, sl] * sin_tab
        q_ref[:, sl] = (q_rot * (MLA_SCALE * LOG2E)).astype(bf16)
        k_ref[:, sl] = (kf[:, sl] + k_rope).astype(bf16)


def _mla_prep(a, gq, gkv, wq, wk, wv, rope):
    t = a.shape[0]
    hq = MLA_HEADS * LANES
    hv = MLA_HEADS * MLA_V
    const = lambda i: (0, 0)
    return pl.pallas_call(
        _mla_prep_kernel,
        grid=(t // PREP_TM,),
        in_specs=[
            pl.BlockSpec((PREP_TM, A_COLS), lambda i: (i, 0)),
            pl.BlockSpec((1, MLA_Q_RANK), const),
            pl.BlockSpec((1, MLA_KV_RANK), const),
            pl.BlockSpec((MLA_Q_RANK, 2 * hq), const),
            pl.BlockSpec((MLA_KV_RANK, hq), const),
            pl.BlockSpec((MLA_KV_RANK, hv), const),
            pl.BlockSpec((PREP_TM, 3 * LANES), lambda i: (i, 0)),
        ],
        out_specs=[
            pl.BlockSpec((PREP_TM, hq), lambda i: (i, 0)),
            pl.BlockSpec((PREP_TM, hq), lambda i: (i, 0)),
            pl.BlockSpec((PREP_TM, hq), lambda i: (i, 0)),
        ],
        out_shape=[
            jax.ShapeDtypeStruct((t, hq), bf16),
            jax.ShapeDtypeStruct((t, hq), bf16),
            jax.ShapeDtypeStruct((t, hq), bf16),
        ],
        compiler_params=pltpu.CompilerParams(
            dimension_semantics=("arbitrary",), vmem_limit_bytes=VMEM_LIMIT),
        name="mla_prep",
    )(a, gq, gkv, wq, wk, wv, rope)


ATT_TQ = 512
ATT_TK = 256


MLA_TK = 256
DIFF_TK = 512


def _lane_fold(x, op):
    out = x[:, :LANES]
    for u in range(1, x.shape[1] // LANES):
        out = op(out, x[:, u * LANES:(u + 1) * LANES])
    return out


def _flash_streams(streams, seq, tk):
    n_chunks = seq // tk
    state = [None] * len(streams)
    for c in range(n_chunks):
        rows = slice(c * tk, (c + 1) * tk)
        for i, (q, k_ref, k_cols, v_ref, v_cols, bias_fn) in enumerate(streams):
            s = _nt_dot(q, k_ref[rows, k_cols])
            if bias_fn is not None:
                s = s + bias_fn(c)
            m_c = jnp.max(_lane_fold(s, jnp.maximum), axis=-1, keepdims=True)
            if c == 0:
                m_new = m_c
            else:
                m_old, acc_old = state[i]
                m_new = jnp.maximum(m_old, m_c)
            p = jnp.exp2(s - m_new).astype(bf16)
            pv = jnp.dot(p, v_ref[rows, v_cols], preferred_element_type=f32)
            state[i] = (m_new, pv if c == 0 else jnp.exp2(m_old - m_new) * acc_old + pv)
    return [acc for _, acc in state]


MLA_STEP_HEADS = 2


def _mla_attn_kernel(q_ref, k_ref, v_ref, o_ref):
    tq = q_ref.shape[0]
    streams = [(q_ref[:, h * LANES:(h + 1) * LANES], k_ref, slice(h * LANES, (h + 1) * LANES),
                v_ref, slice(h * LANES, (h + 1) * LANES), None) for h in range(MLA_STEP_HEADS)]
    accs = _flash_streams(streams, k_ref.shape[0], MLA_TK)
    lane = lax.broadcasted_iota(jnp.int32, (tq, LANES), 1)
    for j in range(MLA_STEP_HEADS // 2):
        acc_a, acc_b = accs[2 * j], accs[2 * j + 1]
        o_ref[:, j * LANES:(j + 1) * LANES] = jnp.where(
            lane < MLA_V, acc_a / acc_a[:, MLA_V:MLA_V + 1], acc_b / acc_b[:, 0:1]).astype(o_ref.dtype)


def _mla_attn(q, k, v, batch, seq):
    t = q.shape[0]
    nq = seq // ATT_TQ
    n_groups = MLA_HEADS // MLA_STEP_HEADS
    w = MLA_STEP_HEADS * LANES
    return pl.pallas_call(
        _mla_attn_kernel,
        grid=(batch, n_groups, nq),
        in_specs=[
            pl.BlockSpec((ATT_TQ, w), lambda b, j, i: (b * nq + i, j)),
            pl.BlockSpec((seq, w), lambda b, j, i: (b, j)),
            pl.BlockSpec((seq, w), lambda b, j, i: (b, j)),
        ],
        out_specs=pl.BlockSpec((ATT_TQ, w // 2), lambda b, j, i: (b * nq + i, j)),
        out_shape=jax.ShapeDtypeStruct((t, MLA_HEADS * MLA_V), bf16),
        compiler_params=pltpu.CompilerParams(
            dimension_semantics=("arbitrary", "arbitrary", "arbitrary"), vmem_limit_bytes=VMEM_LIMIT),
        name="mla_attn",
    )(q, k, v)


DIFF_TQ = 512
DIFF_SCALE = DIFF_QK ** -0.5
BIAS_FAR = 2
N_BIAS_TILES = 2 * BIAS_FAR + 1


def _diff_attn_kernel(lam_ref, q0_ref, q1_ref, k0_ref, k1_ref, v_ref, bias_ref, gsub_ref, o_ref, vaug_ref):
    qi = pl.program_id(2)
    tq = q0_ref.shape[0]
    seq = v_ref.shape[0]

    @pl.when(qi == 0)
    def _():
        for hh in range(2):
            vaug_ref[hh, :, :DIFF_V] = v_ref[:, hh * DIFF_V:(hh + 1) * DIFF_V]
            vaug_ref[hh, :, DIFF_V:] = jnp.ones((seq, DIFF_V), bf16)

    lv = lam_ref[...]
    lam_init = lv[4:5, 0:1]
    lam = (jnp.exp(jnp.sum(lv[0:1] * lv[1:2], axis=-1, keepdims=True))
           - jnp.exp(jnp.sum(lv[2:3] * lv[3:4], axis=-1, keepdims=True)) + lam_init)
    lane = lax.broadcasted_iota(jnp.int32, (tq, LANES), 1)
    q_refs = (q0_ref, q1_ref)
    k_refs = (k0_ref, k1_ref)
    all_cols = slice(0, LANES)
    tiles_per_chunk = DIFF_TK // ATT_TK
    streams = []
    for hh in range(2):
        head_lanes = (lane >= DIFF_QK) == (hh == 1)
        for m in range(2):
            q = q_refs[m][...]
            q = jnp.where(head_lanes, q, jnp.zeros_like(q))

            def bias_fn(c, m=m, hh=hh):
                def tile(r, cc):
                    offset = (c * tiles_per_chunk + cc) - (qi * (tq // ATT_TK) + r)
                    return bias_ref[m, hh, jnp.clip(offset, -BIAS_FAR, BIAS_FAR) + BIAS_FAR]
                return jnp.concatenate(
                    [jnp.concatenate([tile(r, cc) for cc in range(tiles_per_chunk)], axis=1)
                     for r in range(tq // ATT_TK)], axis=0)

            streams.append((q, k_refs[m], all_cols, vaug_ref.at[hh], slice(0, 2 * DIFF_V), bias_fn))
    outs = [acc[:, :DIFF_V] / acc[:, DIFF_V:DIFF_V + 1] for acc in _flash_streams(streams, seq, DIFF_TK)]
    for hh in range(2):
        o = outs[2 * hh] - lam * outs[2 * hh + 1]
        o = _rms(o, gsub_ref[...]) * (1.0 - lam_init)
        o_ref[:, hh * DIFF_V:(hh + 1) * DIFF_V] = o.astype(o_ref.dtype)


def _diff_attn(lamvec, qkv, bias_tiles, gsub, batch, seq):
    t = qkv.shape[0]
    nq = seq // DIFF_TQ
    n_pairs = DIFF_HEADS // 2
    return pl.pallas_call(
        _diff_attn_kernel,
        grid=(batch, n_pairs, nq),
        in_specs=[
            pl.BlockSpec((8, DIFF_QK), lambda b, j, i: (0, 0)),
            pl.BlockSpec((DIFF_TQ, LANES), lambda b, j, i: (b * nq + i, FQ_BLK + j)),
            pl.BlockSpec((DIFF_TQ, LANES), lambda b, j, i: (b * nq + i, FQ_BLK + n_pairs + j)),
            pl.BlockSpec((seq, LANES), lambda b, j, i: (b, FK_BLK + j)),
            pl.BlockSpec((seq, LANES), lambda b, j, i: (b, FK_BLK + n_pairs + j)),
            pl.BlockSpec((seq, 2 * DIFF_V), lambda b, j, i: (b, FV_BLK // 2 + j)),
            pl.BlockSpec((2, 2, N_BIAS_TILES, ATT_TK, ATT_TK), lambda b, j, i: (0, j, 0, 0, 0)),
            pl.BlockSpec((1, DIFF_V), lambda b, j, i: (0, 0)),
        ],
        out_specs=pl.BlockSpec((DIFF_TQ, 2 * DIFF_V), lambda b, j, i: (b * nq + i, j)),
        out_shape=jax.ShapeDtypeStruct((t, DIFF_HEADS * DIFF_V), bf16),
        scratch_shapes=[pltpu.VMEM((2, seq, 2 * DIFF_V), bf16)],
        compiler_params=pltpu.CompilerParams(
            dimension_semantics=("arbitrary", "arbitrary", "arbitrary"), vmem_limit_bytes=VMEM_LIMIT),
        name="diff_attn",
    )(lamvec, qkv, qkv, qkv, qkv, qkv, bias_tiles, gsub)


DIL_TQ = 128
DIL_TW = DIL_TQ + 2 * DIL_HALF
DIL_SCALE = DIL_QK ** -0.5
DIL_UNROLL = 16


def _regroup_start_stride(r, seq, dil_from, dil_to):
    return (r % dil_from) * (seq // dil_from) + r // dil_from, dil_to // dil_from


def _residue_major(src_ref, dst_ref, stage_ref, dil_from, dil_to):
    seq, width = src_ref.shape
    sub_len = seq // dil_to
    for c in range(width // LANES):
        csl = slice(c * LANES, (c + 1) * LANES)
        stage_ref[c] = src_ref[:, csl].astype(f32)
        for r in range(dil_to):
            start, stride = _regroup_start_stride(r, seq, dil_from, dil_to)
            dst_ref[r * sub_len:(r + 1) * sub_len, csl] = (
                stage_ref[c, pl.ds(start, sub_len, stride=stride), :].astype(dst_ref.dtype))


def _regroup_state(state_ref, stage_ref, dil_from, dil_to):
    _, seq, _ = state_ref.shape
    sub_len = seq // dil_to
    for hh in range(2):
        stage_ref[0] = state_ref[hh]
        for r in range(dil_to):
            start, stride = _regroup_start_stride(r, seq, dil_from, dil_to)
            state_ref[hh, r * sub_len:(r + 1) * sub_len, :] = stage_ref[0, pl.ds(start, sub_len, stride=stride), :]


def _ungroup_state(state_ref, stage_ref, dil_from, dil_to):
    _, seq, _ = state_ref.shape
    sub_len = seq // dil_to
    for hh in range(2):
        for r in range(dil_to):
            start, stride = _regroup_start_stride(r, seq, dil_from, dil_to)
            stage_ref[0, pl.ds(start, sub_len, stride=stride), :] = state_ref[hh, r * sub_len:(r + 1) * sub_len, :]
        state_ref[hh] = stage_ref[0]


def _dil_attn_kernel(q0_ref, q1_ref, q2_ref, k0_ref, k1_ref, k2_ref, v_ref, bm_ref, o_ref,
                     stage_ref, qp_ref, kp_ref, vp_ref, m_ref, l_ref, acc_ref):
    seq = v_ref.shape[0]
    n_tiles = seq // DIL_TQ
    q_refs = (q0_ref, q1_ref, q2_ref)
    k_refs = (k0_ref, k1_ref, k2_ref)
    lane = lax.broadcasted_iota(jnp.int32, (DIL_TQ, LANES), 1)
    col = lax.broadcasted_iota(jnp.int32, (DIL_TQ, DIL_TW), 1)
    dils = [dil for _, dil in DIL_PATTERNS]
    assert dils[0] == 1 and all(b % a == 0 for a, b in zip(dils, dils[1:]))
    for g, dil in enumerate(dils):
        if g == 0:
            q_src, k_src, v_src = q_refs[g], k_refs[g], v_ref
        else:
            _residue_major(q_refs[g], qp_ref, stage_ref, 1, dil)
            _residue_major(k_refs[g], kp_ref, stage_ref, 1, dil)
            _residue_major(v_src, vp_ref, stage_ref, dils[g - 1], dil)
            for state_ref in (m_ref, l_ref, acc_ref):
                _regroup_state(state_ref, stage_ref, dils[g - 1], dil)
            q_src, k_src, v_src = qp_ref, kp_ref, vp_ref
        tiles_per_residue = n_tiles // dil

        def tile(n, carry, g=g, dil=dil, q_src=q_src, k_src=k_src, v_src=v_src,
                 tiles_per_residue=tiles_per_residue):
            base = pl.multiple_of(n * DIL_TQ, DIL_TQ)
            t_in = n % tiles_per_residue
            lo = pl.multiple_of(jnp.maximum(base - DIL_HALF, 0), DIL_HALF)
            hi = pl.multiple_of(jnp.minimum(base + DIL_TQ, seq - DIL_HALF), DIL_HALF)
            c_lo = jnp.where(t_in == 0, DIL_HALF, 0)
            c_hi = jnp.where(t_in == tiles_per_residue - 1, DIL_HALF + DIL_TQ, DIL_TW)
            valid = (col >= c_lo) & (col < c_hi)
            rows = pl.ds(base, DIL_TQ)
            qt = q_src[rows, :]
            kw = jnp.concatenate([k_src[pl.ds(lo, DIL_HALF), :], k_src[pl.ds(base, DIL_TQ), :],
                                  k_src[pl.ds(hi, DIL_HALF), :]], axis=0)
            vw = jnp.concatenate([v_src[pl.ds(lo, DIL_HALF), :], v_src[pl.ds(base, DIL_TQ), :],
                                  v_src[pl.ds(hi, DIL_HALF), :]], axis=0)
            for hh in range(2):
                qh = jnp.where((lane >= DIL_QK) == (hh == 1), qt * DIL_SCALE, jnp.zeros_like(qt))
                hsl = slice(hh * DIL_V, (hh + 1) * DIL_V)
                s = _nt_dot(qh, kw) + bm_ref[g, hh]
                s = jnp.where(valid, s, NEG)
                m_t = jnp.max(s, axis=-1, keepdims=True)
                p = jnp.exp(s - m_t)
                l_t = jnp.broadcast_to(jnp.sum(p, axis=-1, keepdims=True), (DIL_TQ, DIL_V))
                u_t = jnp.dot(p.astype(bf16), vw[:, hsl], preferred_element_type=f32)
                m_t = jnp.broadcast_to(m_t, (DIL_TQ, DIL_V))
                if g == 0:
                    m_ref[hh, rows, :] = m_t
                    l_ref[hh, rows, :] = l_t
                    acc_ref[hh, rows, :] = u_t
                else:
                    m_old = m_ref[hh, rows, :]
                    m_new = jnp.maximum(m_old, m_t)
                    a_old = jnp.exp(m_old - m_new)
                    a_t = jnp.exp(m_t - m_new)
                    m_ref[hh, rows, :] = m_new
                    l_ref[hh, rows, :] = a_old * l_ref[hh, rows, :] + a_t * l_t
                    acc_ref[hh, rows, :] = a_old * acc_ref[hh, rows, :] + a_t * u_t
            return carry

        def tile_group(i, carry, tile=tile):
            for u in range(DIL_UNROLL):
                tile(i * DIL_UNROLL + u, carry)
            return carry

        lax.fori_loop(0, n_tiles // DIL_UNROLL, tile_group, 0)
    for hh in range(2):
        acc_ref[hh] = acc_ref[hh] / l_ref[hh]
    for g in range(len(dils) - 1, 0, -1):
        _ungroup_state(acc_ref, stage_ref, dils[g - 1], dils[g])
    for hh in range(2):
        o_ref[:, hh * DIL_V:(hh + 1) * DIL_V] = acc_ref[hh].astype(o_ref.dtype)


def _dil_attn(qkv, biasmask, batch, seq):
    t = qkv.shape[0]
    n_pairs = DIL_HEADS // 2
    pair_w = 2 * DIL_V

    def qk_spec(blk0, g):
        return pl.BlockSpec((seq, LANES), lambda b, j: (b, blk0 + g * n_pairs + j))

    state = pltpu.VMEM((2, seq, DIL_V), f32)
    return pl.pallas_call(
        _dil_attn_kernel,
        grid=(batch, n_pairs),
        in_specs=[qk_spec(DQ_BLK, g) for g in range(DIL_GROUPS)]
        + [qk_spec(DK_BLK, g) for g in range(DIL_GROUPS)]
        + [pl.BlockSpec((seq, pair_w), lambda b, j: (b, DV_BLK // 2 + j)),
           pl.BlockSpec((DIL_GROUPS, 2, DIL_TQ, DIL_TW), lambda b, j: (0, j, 0, 0))],
        out_specs=pl.BlockSpec((seq, pair_w), lambda b, j: (b, j)),
        out_shape=jax.ShapeDtypeStruct((t, DIL_HEADS * DIL_V), bf16),
        scratch_shapes=[
            pltpu.VMEM((2, seq, LANES), f32),
            pltpu.VMEM((seq, LANES), bf16),
            pltpu.VMEM((seq, LANES), bf16),
            pltpu.VMEM((seq, pair_w), bf16),
            state, state, state,
        ],
        compiler_params=pltpu.CompilerParams(
            dimension_semantics=("arbitrary", "arbitrary"), vmem_limit_bytes=VMEM_LIMIT_LARGE),
        name="dil_attn",
    )(qkv, qkv, qkv, qkv, qkv, qkv, qkv, biasmask)


MERGE_TM = 1024


def _merge_kernel(x_ref, oa_ref, ob_ref, oc_ref, gz0_ref, gz1_ref, gz2_ref, wb_ref, wo_ref, g_ref, out_ref):
    branches = (oa_ref[...], ob_ref[...], oc_ref[...])
    gate_refs = (gz0_ref, gz1_ref, gz2_ref)
    merged = None
    for n in range(N_BRANCH):
        y = jnp.dot(branches[n], wb_ref[n], preferred_element_type=f32)
        y = jax.nn.sigmoid(gate_refs[n][...].astype(f32)) * y
        merged = y if merged is None else merged + y
    y = jnp.dot(merged.astype(bf16), wo_ref[...], preferred_element_type=f32)
    out_ref[...] = x_ref[...] + _rms(y, g_ref[...])


def _merge(x, oa, ob, oc, z, wb, wo, g):
    t = x.shape[0]
    row = lambda i: (i, 0)
    bw = pl.BlockSpec((MERGE_TM, BRANCH_W), row)
    gate_specs = [pl.BlockSpec((MERGE_TM, D_MODEL), lambda i, n=n: (i, GATE_BLK + n)) for n in range(N_BRANCH)]
    return pl.pallas_call(
        _merge_kernel,
        grid=(t // MERGE_TM,),
        in_specs=[
            pl.BlockSpec((MERGE_TM, D_MODEL), row),
            bw, bw, bw,
            *gate_specs,
            pl.BlockSpec((N_BRANCH, BRANCH_W, D_MODEL), lambda i: (0, 0, 0)),
            pl.BlockSpec((D_MODEL, D_MODEL), lambda i: (0, 0)),
            pl.BlockSpec((1, D_MODEL), lambda i: (0, 0)),
        ],
        out_specs=pl.BlockSpec((MERGE_TM, D_MODEL), row),
        out_shape=jax.ShapeDtypeStruct((t, D_MODEL), f32),
        compiler_params=pltpu.CompilerParams(
            dimension_semantics=("arbitrary",), vmem_limit_bytes=VMEM_LIMIT),
        name="merge",
    )(x, oa, ob, oc, z, z, z, wb, wo, g)


MLP_TM = 1024
MLP_TF = 1024


def _mlp_kernel(x_ref, gpre_ref, wup_ref, wdn_ref, gpost_ref, out_ref):
    x = x_ref[...]
    h = _rms(x, gpre_ref[...]).astype(bf16)
    acc = None
    for f in range(D_FF // MLP_TF):
        fsl = slice(f * MLP_TF, (f + 1) * MLP_TF)
        u = jnp.maximum(jnp.dot(h, wup_ref[:, fsl], preferred_element_type=f32), 0.0)
        y = jnp.dot((u * u).astype(bf16), wdn_ref[fsl, :], preferred_element_type=f32)
        acc = y if acc is None else acc + y
    out_ref[...] = x + _rms(acc, gpost_ref[...])


def _mlp(x, gpre, wup, wdn, gpost):
    t = x.shape[0]
    resident = pl.Buffered(1)
    return pl.pallas_call(
        _mlp_kernel,
        grid=(t // MLP_TM,),
        in_specs=[
            pl.BlockSpec((MLP_TM, D_MODEL), lambda i: (i, 0)),
            pl.BlockSpec((1, D_MODEL), lambda i: (0, 0)),
            pl.BlockSpec((D_MODEL, D_FF), lambda i: (0, 0), pipeline_mode=resident),
            pl.BlockSpec((D_FF, D_MODEL), lambda i: (0, 0), pipeline_mode=resident),
            pl.BlockSpec((1, D_MODEL), lambda i: (0, 0)),
        ],
        out_specs=pl.BlockSpec((MLP_TM, D_MODEL), lambda i: (i, 0)),
        out_shape=jax.ShapeDtypeStruct((t, D_MODEL), f32),
        compiler_params=pltpu.CompilerParams(
            dimension_semantics=("arbitrary",), vmem_limit_bytes=VMEM_LIMIT_LARGE),
        name="mlp",
    )(x, gpre, wup, wdn, gpost)


def _rel_bucket(rel):
    nb = REL_BUCKETS // 2
    max_exact = nb // 2
    ret = jnp.where(rel > 0, nb, 0)
    n = jnp.abs(rel)
    large = max_exact + (jnp.log(jnp.maximum(n, 1).astype(f32) / max_exact)
                         / math.log(REL_MAX_DIST / max_exact) * (nb - max_exact)).astype(jnp.int32)
    large = jnp.minimum(large, nb - 1)
    return ret + jnp.where(n < max_exact, n, large)


def _bias_lookup(tab, bucket):
    one_hot = jax.nn.one_hot(bucket, REL_BUCKETS, dtype=f32)
    return jnp.einsum('...b,bc->...c', one_hot, tab.astype(f32), precision=lax.Precision.HIGHEST)


def _pad_in_weights(w_in):
    sizes = (MLA_Q_RANK, MLA_KV_RANK, MLA_ROPE)
    c_q = w_in[..., :sizes[0]]
    c_kv = w_in[..., sizes[0]:sizes[0] + sizes[1]]
    k_r = w_in[..., sizes[0] + sizes[1]:sum(sizes)]
    rest = w_in[..., sum(sizes):]
    fq0 = (FQ_BLK - DQ_BLK) * LANES
    fq1 = (FK_BLK - DQ_BLK) * LANES
    z = lambda n: jnp.zeros(w_in.shape[:-1] + (n,), bf16)
    pieces = [c_q, c_kv, z(MLA_NOPE), k_r, z(LANES - MLA_NOPE - MLA_ROPE),
              rest[..., :fq0], rest[..., fq0:fq1] * (DIFF_SCALE * LOG2E), rest[..., fq1:]]
    return jnp.concatenate([p.astype(bf16) for p in pieces], axis=-1)


def _pad_mla_weights(w_uq, w_ukv):
    depth = w_uq.shape[0]
    wq = jnp.pad(w_uq, ((0, 0), (0, 0), (0, 0), (0, LANES - MLA_NOPE - MLA_ROPE)))
    wk = jnp.pad(w_ukv[..., :MLA_NOPE], ((0, 0), (0, 0), (0, 0), (0, LANES - MLA_NOPE)))
    wv = w_ukv[..., MLA_NOPE:]
    x1 = wq[..., MLA_NOPE:MLA_NOPE + ROPE_HALF]
    x2 = wq[..., MLA_NOPE + ROPE_HALF:MLA_NOPE + MLA_ROPE]
    partner = jnp.concatenate([jnp.zeros_like(wq[..., :MLA_NOPE]), -x2, x1,
                               jnp.zeros_like(wq[..., MLA_NOPE + MLA_ROPE:])], axis=-1)
    wq = jnp.concatenate([wq.reshape(depth, MLA_Q_RANK, MLA_HEADS * LANES),
                          partner.reshape(depth, MLA_Q_RANK, MLA_HEADS * LANES)], axis=-1)
    return (wq.astype(bf16),
            wk.reshape(depth, MLA_KV_RANK, MLA_HEADS * LANES).astype(bf16),
            wv.reshape(depth, MLA_KV_RANK, MLA_HEADS * MLA_V).astype(bf16))


def _rope_tables(positions):
    inv = ROPE_BASE ** (-jnp.arange(ROPE_HALF, dtype=f32) / ROPE_HALF)
    ang = positions.reshape(-1).astype(f32)[:, None] * inv
    cos_sin = jnp.concatenate([jnp.cos(ang), jnp.sin(ang)], axis=-1)
    place = np.zeros((2 * ROPE_HALF, 3 * LANES), np.float32)
    base = np.zeros((1, 3 * LANES), np.float32)
    base[0, :MLA_NOPE] = 1.0
    for i in range(ROPE_HALF):
        place[i, MLA_NOPE + i] = 1.0
        place[i, MLA_NOPE + ROPE_HALF + i] = 1.0
        place[ROPE_HALF + i, LANES + MLA_NOPE + ROPE_HALF + i] = 1.0
        place[ROPE_HALF + i, 2 * LANES + MLA_NOPE + i] = -1.0
    return jnp.dot(cos_sin, jnp.asarray(place), precision=lax.Precision.HIGHEST) + jnp.asarray(base)


def _diff_bias_tiles(rel_bias):
    tab = rel_bias[:, DIL_BIAS_COLS:]
    i = jnp.arange(ATT_TK)[:, None]
    j = jnp.arange(ATT_TK)[None, :]
    offs = jnp.arange(-BIAS_FAR, BIAS_FAR + 1)[:, None, None] * ATT_TK
    bucket = _rel_bucket(offs + j - i)
    tiles = jnp.transpose(_bias_lookup(tab, bucket), (3, 0, 1, 2)) * LOG2E
    return tiles.reshape(2, DIFF_HEADS, N_BIAS_TILES, ATT_TK, ATT_TK)


def _dil_bias_masks(rel_bias):
    i = jnp.arange(DIL_TQ)[:, None]
    c = jnp.arange(DIL_TW)[None, :]
    rel = c - DIL_HALF - i
    out = []
    for g, (_, dil) in enumerate(DIL_PATTERNS):
        tab = rel_bias[:, g * DIL_HEADS:(g + 1) * DIL_HEADS]
        b = jnp.transpose(_bias_lookup(tab, _rel_bucket(rel * dil)), (2, 0, 1))
        out.append(jnp.where((jnp.abs(rel) <= DIL_HALF)[None], b, NEG))
    return jnp.stack(out, axis=0)


def kernel(x, positions, rel_bias, g_mix_pre, w_in, g_q, w_uq, g_kv, w_ukv, lam_q1, lam_k1, lam_q2, lam_k2,
           g_diff_sub, w_branch, w_out, g_mix_post, g_mlp_pre, w_up, w_down, g_mlp_post):
    batch, seq, d = x.shape
    depth = w_in.shape[0]
    assert d == D_MODEL and (batch * seq) % MLP_TM == 0
    assert seq % (DIL_TQ * DIL_UNROLL) == 0 and seq % (DIL_TQ * DIL_PATTERNS[-1][1]) == 0
    assert seq % ATT_TQ == 0 and seq % DIFF_TQ == 0 and seq % DIFF_TK == 0 and seq % MLA_TK == 0

    w_in_p = _pad_in_weights(w_in)
    wq_p, wk_p, wv_p = _pad_mla_weights(w_uq, w_ukv)
    wb = w_branch.astype(bf16)
    wo = w_out.astype(bf16)
    wup = w_up.astype(bf16)
    wdn = w_down.astype(bf16)
    rope = _rope_tables(positions)
    diff_bias = _diff_bias_tiles(rel_bias)
    dil_bias = _dil_bias_masks(rel_bias)

    lam_init = jnp.array([0.8 - 0.6 * math.exp(-0.3 * l) for l in range(depth)], f32)
    lam_row = jnp.zeros((depth, 1, DIFF_QK), f32).at[:, 0, 0].set(lam_init)
    lamvec = jnp.concatenate([lam_q1[:, None], lam_k1[:, None], lam_q2[:, None], lam_k2[:, None],
                              lam_row, jnp.zeros((depth, 3, DIFF_QK), f32)], axis=1).astype(f32)

    row = lambda v: v.reshape(1, -1).astype(f32)
    xt = x.reshape(batch * seq, d)
    for l in range(depth):
        z = _in_proj(xt, row(g_mix_pre[l]), w_in_p[l])
        q, k, v = _mla_prep(z, row(g_q[l]), row(g_kv[l]), wq_p[l], wk_p[l], wv_p[l], rope)
        o_a = _mla_attn(q, k, v, batch, seq)
        o_b = _dil_attn(z, dil_bias, batch, seq)
        o_c = _diff_attn(lamvec[l], z, diff_bias, row(g_diff_sub[l]), batch, seq)
        xt = _merge(xt, o_a, o_b, o_c, z, wb[l], wo[l], row(g_mix_post[l]))
        xt = _mlp(xt, row(g_mlp_pre[l]), wup[l], wdn[l], row(g_mlp_post[l]))
    return xt.reshape(batch, seq, d)
```
